```python
import jax, jax.numpy as jnp
from jax import lax
import numpy as np

D_MODEL = 1024
BATCH = 8
SEQ = 2048
DEPTH = 1

HEAD_DIM = 64
DIL_CONFIGS = ((128, 1), (512, 4), (2048, 16))
DIL_HEADS = 8
DIL_WIDTH = DIL_HEADS * HEAD_DIM
DIL_BLOCK = 64
NA_HEADS = 8
NA_WIDTH = NA_HEADS * HEAD_DIM
NA_ROWS_MAX = 8
NA_COLS = 16
GRID_W = 64
MEM_LEN = 256
MEM_HEADS = 4
MEM_HEAD_DIM = 128
MEM_WIDTH = MEM_HEADS * MEM_HEAD_DIM
ROPE_THETA = 500000.0
ROPE_DIM = HEAD_DIM // 4
N_BRANCH = 3
BRANCH_WIDTH = 512
EPS = 1e-6
NEG = -1e30

kernel_name = 'hybrid_dilated_neighbourhood_memory_block'


def _in_sizes():
    return ([DIL_WIDTH] * (3 * len(DIL_CONFIGS))
            + [NA_WIDTH] * 3
            + [MEM_WIDTH]
            + [BRANCH_WIDTH] * N_BRANCH
            + [N_BRANCH * D_MODEL])


def _rmsnorm(x, g):
    xf = x.astype(jnp.float32)
    y = xf * lax.rsqrt(jnp.mean(xf * xf, axis=-1, keepdims=True) + EPS)
    return (y * g.astype(jnp.float32)).astype(x.dtype)


def _heads(t, n_heads):
    b, s, w = t.shape
    return t.reshape(b, s, n_heads, w // n_heads).transpose(0, 2, 1, 3)


def _merge_heads(t):
    b, h, s, d = t.shape
    return t.transpose(0, 2, 1, 3).reshape(b, s, h * d)


def _rope_partial(t, pos):
    half = ROPE_DIM // 2
    inv = ROPE_THETA ** (-jnp.arange(half, dtype=jnp.float32) * 2.0 / ROPE_DIM)
    ang = pos[:, None] * inv[None, :]
    cos, sin = jnp.cos(ang), jnp.sin(ang)
    tf = t[..., :ROPE_DIM].astype(jnp.float32)
    t1, t2 = tf[..., :half], tf[..., half:]
    rot = jnp.concatenate([t1 * cos - t2 * sin, t2 * cos + t1 * sin], axis=-1).astype(t.dtype)
    return jnp.concatenate([rot, t[..., ROPE_DIM:]], axis=-1)


def _banded_attention(q, k, v, reach):
    n, L, hd = q.shape
    bq = DIL_BLOCK
    nb = -(-L // bq)
    lp = nb * bq
    qb = jnp.pad(q, ((0, 0), (0, lp - L), (0, 0))).reshape(n, nb, bq, hd)
    kpad = jnp.pad(k, ((0, 0), (bq, lp - L + bq), (0, 0)))
    vpad = jnp.pad(v, ((0, 0), (bq, lp - L + bq), (0, 0)))
    win = jnp.arange(nb)[:, None] * bq + jnp.arange(3 * bq)[None, :]
    kw = kpad[:, win]
    vw = vpad[:, win]
    kpos = win - bq
    qpos = jnp.arange(nb)[:, None] * bq + jnp.arange(bq)[None, :]
    valid = ((kpos[:, None, :] >= 0) & (kpos[:, None, :] < L)
             & (jnp.abs(qpos[:, :, None] - kpos[:, None, :]) <= reach))
    s = jnp.einsum('nbqd,nbkd->nbqk', qb, kw, preferred_element_type=jnp.float32) * (hd ** -0.5)
    s = jnp.where(valid[None], s, NEG)
    m = jnp.max(s, axis=-1, keepdims=True)
    p = jnp.exp(s - m)
    den = jnp.sum(p, axis=-1)
    out = jnp.einsum('nbqk,nbkd->nbqd', p, vw.astype(jnp.float32)) / den[..., None]
    lse = m[..., 0] + jnp.log(den)
    return out.reshape(n, lp, hd)[:, :L], lse.reshape(n, lp)[:, :L]


def _dilated_attention(q, k, v, dilation, reach):
    b, h, s, hd = q.shape
    mlen = s // dilation

    def fold(t):
        return t.reshape(b, h, mlen, dilation, hd).transpose(0, 1, 3, 2, 4).reshape(b * h * dilation, mlen, hd)

    out, lse = _banded_attention(fold(q), fold(k), fold(v), reach)
    out = out.reshape(b, h, dilation, mlen, hd).transpose(0, 1, 3, 2, 4).reshape(b, h, s, hd)
    lse = lse.reshape(b, h, dilation, mlen).transpose(0, 1, 3, 2).reshape(b, h, s)
    return out, lse


def _neighbourhood_attention(q, k, v, rpb):
    b, h, s, hd = q.shape
    rows = s // GRID_W
    kr = min(NA_ROWS_MAX, rows)
    q5 = q.reshape(b, h, rows, GRID_W, hd)
    k5 = k.reshape(b, h, rows, GRID_W, hd)
    v5 = v.reshape(b, h, rows, GRID_W, hd)
    r_ids = jnp.arange(rows)
    r_start = jnp.clip(r_ids - kr // 2, 0, rows - kr)
    row_idx = r_start[:, None] + jnp.arange(kr)[None, :]
    k_rows = k5[:, :, row_idx]
    v_rows = v5[:, :, row_idx]
    c_ids = jnp.arange(GRID_W)
    c_start = jnp.clip(c_ids - NA_COLS // 2, 0, GRID_W - NA_COLS)
    col_mask = (c_ids[None, :] >= c_start[:, None]) & (c_ids[None, :] < c_start[:, None] + NA_COLS)
    dr = row_idx - r_ids[:, None]
    dc = jnp.clip(c_ids[None, :] - c_ids[:, None], -(NA_COLS - 1), NA_COLS - 1)
    bias = rpb[:, dr + NA_ROWS_MAX - 1][..., dc + NA_COLS - 1]
    bias = bias.transpose(0, 1, 3, 2, 4).astype(jnp.float32)
    sc = jnp.einsum('bhrqd,bhrjkd->bhrqjk', q5, k_rows, preferred_element_type=jnp.float32) * (hd ** -0.5)
    sc = jnp.where(col_mask[:, None, :], sc + bias[None], NEG)
    p = jax.nn.softmax(sc, axis=(-2, -1))
    out = jnp.einsum('bhrqjk,bhrjkd->bhrqd', p, v_rows.astype(jnp.float32))
    return out.reshape(b, h, s, hd)


def _cross_attention(q, k, v):
    sc = jnp.einsum('bhqd,bhkd->bhqk', q, k, preferred_element_type=jnp.float32) * (q.shape[-1] ** -0.5)
    p = jax.nn.softmax(sc, axis=-1)
    return jnp.einsum('bhqk,bhkd->bhqd', p, v.astype(jnp.float32))


def setup_inputs(seed: int = 0) -> dict:
    key = jax.random.key(seed)
    ks = jax.random.split(key, 14)
    f32 = jnp.float32
    n_in = int(sum(_in_sizes()))

    def nrm(k, shape, scale):
        return jax.random.normal(k, shape, f32) * scale

    return {
        'x': nrm(ks[0], (BATCH, SEQ, D_MODEL), 1.0),
        'mem': nrm(ks[1], (BATCH, MEM_LEN, D_MODEL), 1.0),
        'pre_norm': 1.0 + nrm(ks[2], (DEPTH, D_MODEL), 0.05),
        'w_in': nrm(ks[3], (DEPTH, D_MODEL, n_in), D_MODEL ** -0.5),
        'merge_bias': nrm(ks[4], (DEPTH, N_BRANCH, D_MODEL), 0.1),
        'na_rpb': nrm(ks[5], (DEPTH, NA_HEADS, 2 * NA_ROWS_MAX - 1, 2 * NA_COLS - 1), 0.1),
        'mem_norm': 1.0 + nrm(ks[6], (DEPTH, D_MODEL), 0.05),
        'w_mem_kv': nrm(ks[7], (DEPTH, D_MODEL, 2 * MEM_WIDTH), D_MODEL ** -0.5),
        'w_branch_a': nrm(ks[8], (DEPTH, BRANCH_WIDTH, D_MODEL), BRANCH_WIDTH ** -0.5),
        'w_branch_b': nrm(ks[9], (DEPTH, BRANCH_WIDTH, D_MODEL), BRANCH_WIDTH ** -0.5),
        'w_branch_c': nrm(ks[10], (DEPTH, BRANCH_WIDTH, D_MODEL), BRANCH_WIDTH ** -0.5),
        'w_out': nrm(ks[11], (DEPTH, D_MODEL, D_MODEL), D_MODEL ** -0.5),
        'post_norm': 1.0 + nrm(ks[12], (DEPTH, D_MODEL), 0.05),
    }


def reference(x, mem, pre_norm, w_in, merge_bias, na_rpb, mem_norm, w_mem_kv,
              w_branch_a, w_branch_b, w_branch_c, w_out, post_norm):
    b, s, _ = x.shape
    pos = jnp.arange(s, dtype=jnp.float32)
    split_at = np.cumsum(_in_sizes())[:-1].tolist()
    n_dil = len(DIL_CONFIGS)
    off = 3 * n_dil
    for layer in range(DEPTH):
        h = _rmsnorm(x, pre_norm[layer])
        parts = jnp.split(h @ w_in[layer], split_at, axis=-1)

        outs, lses = [], []
        for g, (window, dilation) in enumerate(DIL_CONFIGS):
            q = _rope_partial(_heads(parts[3 * g], DIL_HEADS), pos)
            k = _rope_partial(_heads(parts[3 * g + 1], DIL_HEADS), pos)
            v = _heads(parts[3 * g + 2], DIL_HEADS)
            o, l = _dilated_attention(q, k, v, dilation, (window // 2) // dilation)
            outs.append(o)
            lses.append(l)
        wts = jax.nn.softmax(jnp.stack(lses, axis=0), axis=0)
        out_a = _merge_heads(jnp.sum(wts[..., None] * jnp.stack(outs, axis=0), axis=0).astype(x.dtype))

        out_b = _merge_heads(_neighbourhood_attention(
            _heads(parts[off], NA_HEADS), _heads(parts[off + 1], NA_HEADS),
            _heads(parts[off + 2], NA_HEADS), na_rpb[layer]).astype(x.dtype))

        kv_m = _rmsnorm(mem, mem_norm[layer]) @ w_mem_kv[layer]
        k_m, v_m = jnp.split(kv_m, 2, axis=-1)
        out_c = _merge_heads(_cross_attention(
            _heads(parts[off + 3], MEM_HEADS), _heads(k_m, MEM_HEADS),
            _heads(v_m, MEM_HEADS)).astype(x.dtype))

        g_a, g_b, g_c = parts[off + 4], parts[off + 5], parts[off + 6]
        gate_logits = parts[off + 7].reshape(b, s, N_BRANCH, D_MODEL) + merge_bias[layer]
        gates = jax.nn.sigmoid(gate_logits.astype(jnp.float32)).astype(x.dtype)
        y = (gates[:, :, 0] * ((out_a * jax.nn.silu(g_a)) @ w_branch_a[layer])
             + gates[:, :, 1] * ((out_b * jax.nn.silu(g_b)) @ w_branch_b[layer])
             + gates[:, :, 2] * ((out_c * jax.nn.silu(g_c)) @ w_branch_c[layer]))
        y = y @ w_out[layer]
        x = x + _rmsnorm(y, post_norm[layer])
    return x
```

```python
import functools

import jax
import jax.numpy as jnp
import numpy as np
from jax import lax
from jax.experimental import pallas as pl
from jax.experimental.pallas import tpu as pltpu

D_MODEL = 1024
HEAD_DIM = 64
DIL_CONFIGS = ((128, 1), (512, 4), (2048, 16))
WIDTH = 512
NA_ROWS = 8
NA_COLS = 16
GRID_W = 64
MEM_HEADS = 4
MEM_HEAD_DIM = 128
ROPE_THETA = 500000.0
ROPE_DIM = HEAD_DIM // 4
EPS = 1e-6
NEG = -1e30

LANES = 128
QBLK = 128
N_IN = 22 * WIDTH
COL_NA, COL_MEMQ, COL_GATE, COL_MERGE = 9, 12, 13, 16
VMEM_LIMIT = 56 * 1024 * 1024

BF16 = jnp.bfloat16
F32 = jnp.float32


def _proj_kernel(x_ref, g_ref, w_ref, cos_ref, sa_ref, sb_ref, o_ref, h_ref, *, n_rope):
    j = pl.program_id(1)

    @pl.when(j == 0)
    def _():
        xf = x_ref[...]
        ms = jnp.mean(xf * xf, axis=-1, keepdims=True)
        h_ref[...] = (xf * lax.rsqrt(ms + EPS) * g_ref[...]).astype(BF16)

    acc = jnp.dot(h_ref[...], w_ref[...], preferred_element_type=F32)

    if n_rope:
        is_rope = jnp.logical_and(j < n_rope, j % 3 != 2)

        @pl.when(is_rope)
        def _():
            c, sa, sb = cos_ref[...], sa_ref[...], sb_ref[...]
            for t in range(WIDTH // LANES):
                a = acc[:, t * LANES:(t + 1) * LANES]
                rot = (a * c + pltpu.roll(a, LANES - ROPE_DIM // 2, 1) * sa
                       + pltpu.roll(a, ROPE_DIM // 2, 1) * sb)
                o_ref[:, t * LANES:(t + 1) * LANES] = rot.astype(BF16)

        @pl.when(jnp.logical_not(is_rope))
        def _():
            o_ref[...] = acc.astype(BF16)
    else:
        o_ref[...] = acc.astype(BF16)


def _proj(x2d, gain, w_bf16, tables, *, seq, n_rope, tm):
    rows, dm = x2d.shape
    n_out = w_bf16.shape[1]
    cos_t, sa_t, sb_t = tables
    tab_blocks = seq // tm if n_rope else 1
    tab_spec = pl.BlockSpec((tm, LANES), lambda i, j: (i % tab_blocks, 0))
    return pl.pallas_call(
        functools.partial(_proj_kernel, n_rope=n_rope),
        out_shape=jax.ShapeDtypeStruct((rows, n_out), BF16),
        grid=(rows // tm, n_out // WIDTH),
        in_specs=[
            pl.BlockSpec((tm, dm), lambda i, j: (i, 0)),
            pl.BlockSpec((1, dm), lambda i, j: (0, 0)),
            pl.BlockSpec((dm, WIDTH), lambda i, j: (0, j)),
            tab_spec, tab_spec, tab_spec,
        ],
        out_specs=pl.BlockSpec((tm, WIDTH), lambda i, j: (i, j)),
        scratch_shapes=[pltpu.VMEM((tm, dm), BF16)],
        compiler_params=pltpu.CompilerParams(
            dimension_semantics=("arbitrary", "arbitrary"),
            vmem_limit_bytes=VMEM_LIMIT),
        name="proj",
    )(x2d, gain, w_bf16, cos_t, sa_t, sb_t)


def _rope_tables(seq):
    half = ROPE_DIM // 2
    pos = jnp.arange(seq, dtype=F32)
    inv = ROPE_THETA ** (-jnp.arange(half, dtype=F32) * 2.0 / ROPE_DIM)
    ang = pos[:, None] * inv[None, :]
    cos, sin = jnp.cos(ang), jnp.sin(ang)
    ones = jnp.ones((seq, HEAD_DIM - ROPE_DIM), F32)
    zeros = jnp.zeros((seq, HEAD_DIM - ROPE_DIM), F32)
    zh = jnp.zeros((seq, half), F32)
    c = jnp.concatenate([cos, cos, ones], axis=-1)
    sa = jnp.concatenate([-sin, zh, zeros], axis=-1)
    sb = jnp.concatenate([zh, sin, zeros], axis=-1)
    rep = LANES // HEAD_DIM
    return tuple(jnp.tile(t, (1, rep)) for t in (c, sa, sb))


def _dil_kernel(q_ref, k_ref, v_ref, o_ref, lse_ref, *, length, reach):
    kw = min(2 * QBLK, length)
    n_qb = length // QBLK
    left = lax.broadcasted_iota(jnp.int32, (1, LANES), 1) < HEAD_DIM
    rel0 = (lax.broadcasted_iota(jnp.int32, (QBLK, kw), 0)
            - lax.broadcasted_iota(jnp.int32, (QBLK, kw), 1))
    scale = HEAD_DIM ** -0.5

    def tile(qb, hp):
        cs = slice(hp * LANES, (hp + 1) * LANES)
        q0 = pl.multiple_of(qb * QBLK, QBLK)
        ks = pl.multiple_of(jnp.clip(q0 - reach, 0, length - kw), 64)
        q2 = q_ref[pl.ds(q0, QBLK), cs]
        k2 = k_ref[pl.ds(ks, kw), cs]
        v2 = v_ref[pl.ds(ks, kw), cs]
        valid = jnp.abs(rel0 + (q0 - ks)) <= reach
        outs, lses = [], []
        for sel in (left, jnp.logical_not(left)):
            qh = jnp.where(sel, q2, jnp.zeros_like(q2))
            s = lax.dot_general(qh, k2, (((1,), (1,)), ((), ())),
                                preferred_element_type=F32) * scale
            s = jnp.where(valid, s, NEG)
            m = jnp.max(s, axis=-1, keepdims=True)
            p = jnp.exp(s - m)
            den = jnp.sum(p, axis=-1, keepdims=True)
            pv = jnp.dot(p.astype(BF16), v2, preferred_element_type=F32)
            outs.append(pv / den)
            lses.append(m + jnp.log(den))
        o_ref[pl.ds(q0, QBLK), cs] = jnp.where(left, outs[0], outs[1]).astype(BF16)
        lse_ref[pl.ds(q0, QBLK), cs] = jnp.where(
            left, jnp.broadcast_to(lses[0], (QBLK, LANES)),
            jnp.broadcast_to(lses[1], (QBLK, LANES)))

    for hp in range(WIDTH // LANES):
        lax.fori_loop(0, n_qb, lambda qb, c, hp=hp: (tile(qb, hp), c)[1], 0)


def _dil_attn(parts, group, dilation, reach):
    b, s, n_in = parts.shape
    length = s // dilation
    view = parts.reshape(b, length, dilation * n_in)
    tiles = n_in // WIDTH

    def in_spec(kind):
        return pl.BlockSpec((None, length, WIDTH),
                            lambda i, r: (i, 0, r * tiles + 3 * group + kind))

    out_spec = pl.BlockSpec((None, length, WIDTH), lambda i, r: (i, 0, r))
    out, lse = pl.pallas_call(
        functools.partial(_dil_kernel, length=length, reach=reach),
        out_shape=(jax.ShapeDtypeStruct((b, length, dilation * WIDTH), BF16),
                   jax.ShapeDtypeStruct((b, length, dilation * WIDTH), F32)),
        grid=(b, dilation),
        in_specs=[in_spec(0), in_spec(1), in_spec(2)],
        out_specs=(out_spec, out_spec),
        compiler_params=pltpu.CompilerParams(
            dimension_semantics=("arbitrary", "arbitrary"),
            vmem_limit_bytes=VMEM_LIMIT),
        name=f"dil_attn_d{dilation}",
    )(view, view, view)
    return out.reshape(b * s, WIDTH), lse.reshape(b * s, WIDTH)


def _na_kernel(q_ref, k_ref, v_ref, bias_ref, o_ref, *, rows):
    kr = min(NA_ROWS, rows)
    nk = kr * GRID_W
    left = lax.broadcasted_iota(jnp.int32, (1, LANES), 1) < HEAD_DIM
    scale = HEAD_DIM ** -0.5

    def tile(r, hp):
        cs = slice(hp * LANES, (hp + 1) * LANES)
        r_start = jnp.clip(r - kr // 2, 0, rows - kr)
        q0 = pl.multiple_of(r * GRID_W, GRID_W)
        ks = pl.multiple_of(r_start * GRID_W, GRID_W)
        variant = r_start - r + (NA_ROWS - 1)
        q2 = q_ref[pl.ds(q0, GRID_W), cs]
        k2 = k_ref[pl.ds(ks, nk), cs]
        v2 = v_ref[pl.ds(ks, nk), cs]
        outs = []
        for half, sel in enumerate((left, jnp.logical_not(left))):
            qh = jnp.where(sel, q2, jnp.zeros_like(q2))
            s = lax.dot_general(qh, k2, (((1,), (1,)), ((), ())),
                                preferred_element_type=F32) * scale
            s = s + bias_ref[2 * hp + half, variant]
            m = jnp.max(s, axis=-1, keepdims=True)
            p = jnp.exp(s - m)
            den = jnp.sum(p, axis=-1, keepdims=True)
            pv = jnp.dot(p.astype(BF16), v2, preferred_element_type=F32)
            outs.append(pv / den)
        o_ref[pl.ds(q0, GRID_W), cs] = jnp.where(left, outs[0], outs[1]).astype(BF16)

    for hp in range(WIDTH // LANES):
        lax.fori_loop(0, rows, lambda r, c, hp=hp: (tile(r, hp), c)[1], 0)


def _na_bias_tiles(rpb, rows):
    kr = min(NA_ROWS, rows)
    c = np.arange(GRID_W)
    dc = np.clip(c[None, :] - c[:, None], -(NA_COLS - 1), NA_COLS - 1) + NA_COLS - 1
    c_start = np.clip(c - NA_COLS // 2, 0, GRID_W - NA_COLS)
    col_mask = (c[None, :] >= c_start[:, None]) & (c[None, :] < c_start[:, None] + NA_COLS)
    t = jnp.where(col_mask[None, None], rpb[:, :, dc].astype(F32), NEG)
    n_var = 2 * NA_ROWS - kr
    tiles = jnp.stack([t[:, a:a + kr] for a in range(n_var)], axis=1)
    return tiles.transpose(0, 1, 3, 2, 4).reshape(rpb.shape[0], n_var, GRID_W, kr * GRID_W)


def _na_attn(parts, bias_tiles):
    b, s, _ = parts.shape
    rows = s // GRID_W

    def in_spec(kind):
        return pl.BlockSpec((None, s, WIDTH), lambda i: (i, 0, COL_NA + kind))

    out = pl.pallas_call(
        functools.partial(_na_kernel, rows=rows),
        out_shape=jax.ShapeDtypeStruct((b, s, WIDTH), BF16),
        grid=(b,),
        in_specs=[in_spec(0), in_spec(1), in_spec(2),
                  pl.BlockSpec(bias_tiles.shape, lambda i: (0, 0, 0, 0))],
        out_specs=pl.BlockSpec((None, s, WIDTH), lambda i: (i, 0, 0)),
        compiler_params=pltpu.CompilerParams(
            dimension_semantics=("arbitrary",), vmem_limit_bytes=VMEM_LIMIT),
        name="na_attn",
    )(parts, parts, parts, bias_tiles)
    return out.reshape(b * s, WIDTH)


def _mem_kernel(q_ref, kv_ref, o_ref, *, seq):
    scale = MEM_HEAD_DIM ** -0.5
    mem_width = MEM_HEADS * MEM_HEAD_DIM

    def tile(qb, h):
        cs = slice(h * MEM_HEAD_DIM, (h + 1) * MEM_HEAD_DIM)
        q0 = pl.multiple_of(qb * QBLK, QBLK)
        q = q_ref[pl.ds(q0, QBLK), cs]
        k = kv_ref[:, cs]
        v = kv_ref[:, mem_width + h * MEM_HEAD_DIM: mem_width + (h + 1) * MEM_HEAD_DIM]
        s = lax.dot_general(q, k, (((1,), (1,)), ((), ())),
                            preferred_element_type=F32) * scale
        m = jnp.max(s, axis=-1, keepdims=True)
        p = jnp.exp(s - m)
        den = jnp.sum(p, axis=-1, keepdims=True)
        pv = jnp.dot(p.astype(BF16), v, preferred_element_type=F32)
        o_ref[pl.ds(q0, QBLK), cs] = (pv / den).astype(BF16)

    for h in range(MEM_HEADS):
        lax.fori_loop(0, seq // QBLK, lambda qb, c, h=h: (tile(qb, h), c)[1], 0)


def _mem_attn(parts, kv_m):
    b, s, _ = parts.shape
    mem_len, kv_w = kv_m.shape[1:]
    out = pl.pallas_call(
        functools.partial(_mem_kernel, seq=s),
        out_shape=jax.ShapeDtypeStruct((b, s, WIDTH), BF16),
        grid=(b,),
        in_specs=[pl.BlockSpec((None, s, WIDTH), lambda i: (i, 0, COL_MEMQ)),
                  pl.BlockSpec((None, mem_len, kv_w), lambda i: (i, 0, 0))],
        out_specs=pl.BlockSpec((None, s, WIDTH), lambda i: (i, 0, 0)),
        compiler_params=pltpu.CompilerParams(
            dimension_semantics=("arbitrary",), vmem_limit_bytes=VMEM_LIMIT),
        name="mem_attn",
    )(parts, kv_m)
    return out.reshape(b * s, WIDTH)


def _sigmoid(z):
    return 1.0 / (1.0 + jnp.exp(-z))


def _merge_kernel(oa0, oa1, oa2, l0, l1, l2, ob, oc, ga, gb, gc, z0, z1, z2,
                  mb_ref, wa_ref, wb_ref, wc_ref, wo_ref, pn_ref, x_ref, out_ref):
    lse = (l0[...], l1[...], l2[...])
    m = jnp.maximum(jnp.maximum(lse[0], lse[1]), lse[2])
    e = [jnp.exp(l - m) for l in lse]
    num = (e[0] * oa0[...].astype(F32) + e[1] * oa1[...].astype(F32)
           + e[2] * oa2[...].astype(F32))
    out_a = num / (e[0] + e[1] + e[2])

    def branch(o, g_ref, w_ref, z_ref, idx):
        g = g_ref[...].astype(F32)
        u = (o * (g * _sigmoid(g))).astype(BF16)
        yb = jnp.dot(u, w_ref[...], preferred_element_type=F32)
        gate = _sigmoid(z_ref[...].astype(F32) + mb_ref[idx:idx + 1, :])
        return gate * yb

    y = (branch(out_a, ga, wa_ref, z0, 0)
         + branch(ob[...].astype(F32), gb, wb_ref, z1, 1)
         + branch(oc[...].astype(F32), gc, wc_ref, z2, 2))
    z = jnp.dot(y.astype(BF16), wo_ref[...], preferred_element_type=F32)
    ms = jnp.mean(z * z, axis=-1, keepdims=True)
    out_ref[...] = x_ref[...] + z * lax.rsqrt(ms + EPS) * pn_ref[...]


def _merge(x2d, parts2d, dil_outs, dil_lses, out_b, out_c, merge_bias,
           wa, wb, wc, wo, post_norm, *, tm):
    rows, dm = x2d.shape
    row_w = pl.BlockSpec((tm, WIDTH), lambda i: (i, 0))
    merge_tile = COL_MERGE * WIDTH // dm

    def part_w(col):
        return pl.BlockSpec((tm, WIDTH), lambda i: (i, col))

    def part_d(col):
        return pl.BlockSpec((tm, dm), lambda i: (i, col))

    def full(shape):
        return pl.BlockSpec(shape, lambda i: (0, 0))

    return pl.pallas_call(
        _merge_kernel,
        out_shape=jax.ShapeDtypeStruct((rows, dm), F32),
        grid=(rows // tm,),
        in_specs=[row_w] * 8
        + [part_w(COL_GATE), part_w(COL_GATE + 1), part_w(COL_GATE + 2)]
        + [part_d(merge_tile), part_d(merge_tile + 1), part_d(merge_tile + 2)]
        + [full(merge_bias.shape), full(wa.shape), full(wb.shape), full(wc.shape),
           full(wo.shape), full(post_norm.shape), pl.BlockSpec((tm, dm), lambda i: (i, 0))],
        out_specs=pl.BlockSpec((tm, dm), lambda i: (i, 0)),
        compiler_params=pltpu.CompilerParams(
            dimension_semantics=("arbitrary",), vmem_limit_bytes=VMEM_LIMIT),
        name="merge",
    )(*dil_outs, *dil_lses, out_b, out_c, parts2d, parts2d, parts2d,
      parts2d, parts2d, parts2d, merge_bias, wa, wb, wc, wo, post_norm, x2d)


def kernel(x, mem, pre_norm, w_in, merge_bias, na_rpb, mem_norm, w_mem_kv,
           w_branch_a, w_branch_b, w_branch_c, w_out, post_norm):
    b, s, dm = x.shape
    depth = pre_norm.shape[0]
    tables = _rope_tables(s)
    for layer in range(depth):
        x2d = x.reshape(b * s, dm)
        parts2d = _proj(x2d, pre_norm[layer][None], w_in[layer].astype(BF16), tables,
                        seq=s, n_rope=3 * len(DIL_CONFIGS), tm=1024)
        parts = parts2d.reshape(b, s, N_IN)

        dil_outs, dil_lses = [], []
        for g, (window, dilation) in enumerate(DIL_CONFIGS):
            o, l = _dil_attn(parts, g, dilation, (window // 2) // dilation)
            dil_outs.append(o)
            dil_lses.append(l)

        out_b = _na_attn(parts, _na_bias_tiles(na_rpb[layer], s // GRID_W))

        mem2d = mem.reshape(b * mem.shape[1], dm)
        kv_m = _proj(mem2d, mem_norm[layer][None], w_mem_kv[layer].astype(BF16), tables,
                     seq=s, n_rope=0, tm=1024)
        out_c = _mem_attn(parts, kv_m.reshape(b, mem.shape[1], -1))

        y = _merge(x2d, parts2d, dil_outs, dil_lses, out_b, out_c, merge_bias[layer],
                   w_branch_a[layer].astype(BF16), w_branch_b[layer].astype(BF16),
                   w_branch_c[layer].astype(BF16), w_out[layer].astype(BF16),
                   post_norm[layer][None], tm=256)
        x = y.reshape(b, s, dm)
    return x
```

```python
import functools
import math

import jax
import jax.numpy as jnp
import numpy as np
from jax import lax
from jax.experimental import pallas as pl
from jax.experimental.pallas import tpu as pltpu

D_MODEL = 1024
HEAD_DIM = 64
DIL_CONFIGS = ((128, 1), (512, 4), (2048, 16))
WIDTH = 512
NA_ROWS = 8
NA_COLS = 16
GRID_W = 64
MEM_HEADS = 4
MEM_HEAD_DIM = 128
ROPE_THETA = 500000.0
ROPE_DIM = HEAD_DIM // 4
EPS = 1e-6
NEG = -1e30
LOG2E = math.log2(math.e)

LANES = 128
QBLK = 128
CHUNK = 4
VMEM_LIMIT = 56 * 1024 * 1024

NAT_TILES = (0, 1, 2, 9, 10, 11, 14, 15, 18, 19, 20, 21, 12)
P4_TILES = (3, 4, 5)
P16_TILES = (6, 7, 8, 13, 16, 17)
N_NAT, N_P4, N_P16 = len(NAT_TILES), len(P4_TILES), len(P16_TILES)
ROPE_STEPS = (0, 1, N_NAT, N_NAT + 1, N_NAT + N_P4, N_NAT + N_P4 + 1)
DIL_Q_STEPS = (0, N_NAT, N_NAT + N_P4)
NAT_NA, NAT_GATE_B, NAT_MERGE_B, NAT_MEMQ = 3, 6, 8, 12
P16_GATE_A, P16_MERGE_A = 3, 4

BF16 = jnp.bfloat16
F32 = jnp.float32


def _any_of(j, steps):
    hit = j == steps[0]
    for s in steps[1:]:
        hit = jnp.logical_or(hit, j == s)
    return hit


def _proj_kernel(x_ref, g_ref, w_ref, cos_ref, sin_ref, o_nat, o_p4, o_p16, h_ref, hf_ref, *, tm):
    j = pl.program_id(1)

    @pl.when(j == 0)
    def _():
        xf = x_ref[...]
        ms = jnp.mean(xf * xf, axis=-1, keepdims=True)
        hf = xf * lax.rsqrt(ms + EPS) * g_ref[...]
        h_ref[0] = hf.astype(BF16)
        for c in range(hf.shape[1] // LANES):
            hf_ref[c] = hf[:, c * LANES:(c + 1) * LANES]
        for slot, d in ((1, 4), (2, 16)):
            seg = tm // d
            for r in range(d):
                for c in range(hf.shape[1] // LANES):
                    h_ref[slot, r * seg:(r + 1) * seg, c * LANES:(c + 1) * LANES] = (
                        hf_ref[c, pl.ds(r, seg, stride=d), :].astype(BF16))

    slot = (j >= N_NAT).astype(jnp.int32) + (j >= N_NAT + N_P4).astype(jnp.int32)
    acc = jnp.dot(h_ref[slot], w_ref[...], preferred_element_type=F32)
    q_scale = jnp.where(
        jnp.logical_or(_any_of(j, DIL_Q_STEPS), j == NAT_NA), HEAD_DIM ** -0.5 * LOG2E,
        jnp.where(j == NAT_MEMQ, MEM_HEAD_DIM ** -0.5 * LOG2E, 1.0)).astype(F32)

    def store(val_of_tile):
        @pl.when(slot == 0)
        def _():
            for t in range(WIDTH // LANES):
                o_nat[:, t * LANES:(t + 1) * LANES] = val_of_tile(t).astype(BF16)

        for s, d, o_ref in ((1, 4, o_p4), (2, 16, o_p16)):
            @pl.when(slot == s)
            def _(d=d, o_ref=o_ref):
                seg = tm // d
                for t in range(WIDTH // LANES):
                    v = val_of_tile(t).astype(BF16)
                    for r in range(d):
                        o_ref[r, :, t * LANES:(t + 1) * LANES] = v[r * seg:(r + 1) * seg]

    is_rope = _any_of(j, ROPE_STEPS)

    @pl.when(is_rope)
    def _():
        c, s = cos_ref[...], sin_ref[...]
        lane = lax.broadcasted_iota(jnp.int32, (1, LANES), 1)
        first_half = (lane % HEAD_DIM) < ROPE_DIM // 2

        def rotated(t):
            a = acc[:, t * LANES:(t + 1) * LANES]
            partner = jnp.where(first_half, pltpu.roll(a, LANES - ROPE_DIM // 2, 1),
                                pltpu.roll(a, ROPE_DIM // 2, 1))
            return (a * c + partner * s) * q_scale

        store(rotated)

    @pl.when(jnp.logical_not(is_rope))
    def _():
        store(lambda t: acc[:, t * LANES:(t + 1) * LANES] * q_scale)


def _proj(x2d, gain, w_tiles, cos_t, sin_t, *, batch, seq, tm):
    rows, dm = x2d.shape
    per_b = seq // tm
    n_steps = N_NAT + N_P4 + N_P16

    def slot_of(j):
        return (j >= N_NAT).astype(jnp.int32) + (j >= N_NAT + N_P4).astype(jnp.int32)

    tab_spec = pl.BlockSpec((None, tm, LANES), lambda i, j: (slot_of(j), i % per_b, 0))
    return pl.pallas_call(
        functools.partial(_proj_kernel, tm=tm),
        out_shape=(
            jax.ShapeDtypeStruct((rows, N_NAT * WIDTH), BF16),
            jax.ShapeDtypeStruct((batch, 4, seq // 4, N_P4 * WIDTH), BF16),
            jax.ShapeDtypeStruct((batch, 16, seq // 16, N_P16 * WIDTH), BF16)),
        grid=(rows // tm, n_steps),
        in_specs=[
            pl.BlockSpec((tm, dm), lambda i, j: (i, 0)),
            pl.BlockSpec((1, dm), lambda i, j: (0, 0)),
            pl.BlockSpec((dm, WIDTH), lambda i, j: (0, j)),
            tab_spec, tab_spec,
        ],
        out_specs=(
            pl.BlockSpec((tm, WIDTH), lambda i, j: (i, jnp.minimum(j, N_NAT - 1))),
            pl.BlockSpec((None, 4, tm // 4, WIDTH),
                         lambda i, j: (i // per_b, 0, i % per_b, jnp.clip(j - N_NAT, 0, N_P4 - 1))),
            pl.BlockSpec((None, 16, tm // 16, WIDTH),
                         lambda i, j: (i // per_b, 0, i % per_b,
                                       jnp.clip(j - N_NAT - N_P4, 0, N_P16 - 1)))),
        scratch_shapes=[pltpu.VMEM((3, tm, dm), BF16),
                        pltpu.VMEM((dm // LANES, tm, LANES), F32)],
        compiler_params=pltpu.CompilerParams(
            dimension_semantics=("arbitrary", "arbitrary"),
            vmem_limit_bytes=VMEM_LIMIT),
        name="proj",
    )(x2d, gain, w_tiles, cos_t, sin_t)


def _rope_tables(seq, tm):
    half = ROPE_DIM // 2
    pos = jnp.arange(seq, dtype=F32)
    inv = ROPE_THETA ** (-jnp.arange(half, dtype=F32) * 2.0 / ROPE_DIM)
    ang = pos[:, None] * inv[None, :]
    cos, sin = jnp.cos(ang), jnp.sin(ang)
    rest = HEAD_DIM - ROPE_DIM
    c = jnp.concatenate([cos, cos, jnp.ones((seq, rest), F32)], axis=-1)
    s = jnp.concatenate([-sin, sin, jnp.zeros((seq, rest), F32)], axis=-1)

    def orders(t):
        t = jnp.tile(t, (1, LANES // HEAD_DIM))
        out = [t]
        for d in (4, 16):
            out.append(t.reshape(seq // tm, tm // d, d, LANES).transpose(0, 2, 1, 3)
                       .reshape(seq, LANES))
        return jnp.stack(out)

    return orders(c), orders(s)


def _kv_kernel(x_ref, g_ref, w_ref, o_ref):
    xf = x_ref[...]
    ms = jnp.mean(xf * xf, axis=-1, keepdims=True)
    h = (xf * lax.rsqrt(ms + EPS) * g_ref[...]).astype(BF16)
    o_ref[...] = jnp.dot(h, w_ref[...], preferred_element_type=F32).astype(BF16)


def _kv_proj(mem2d, gain, w_bf16, *, tm):
    rows, dm = mem2d.shape
    n_out = w_bf16.shape[1]
    return pl.pallas_call(
        _kv_kernel,
        out_shape=jax.ShapeDtypeStruct((rows, n_out), BF16),
        grid=(rows // tm,),
        in_specs=[pl.BlockSpec((tm, dm), lambda i: (i, 0)),
                  pl.BlockSpec((1, dm), lambda i: (0, 0)),
                  pl.BlockSpec((dm, n_out), lambda i: (0, 0))],
        out_specs=pl.BlockSpec((tm, n_out), lambda i: (i, 0)),
        compiler_params=pltpu.CompilerParams(
            dimension_semantics=("arbitrary",), vmem_limit_bytes=VMEM_LIMIT),
        name="kv_proj",
    )(mem2d, gain, w_bf16)


def _dil_kernel(q0, k0, v0, q1, k1, v1, q2, k2, v2, o_ref,
                og0, lg0, og1, lg1, bias_scr, s_scr, p_scr, *, seq, reach):
    left = lax.broadcasted_iota(jnp.int32, (1, LANES), 1) < HEAD_DIM
    kw_max = 2 * QBLK

    @pl.when(jnp.logical_and(pl.program_id(0) == 0, pl.program_id(1) == 0))
    def _():
        rel = (lax.broadcasted_iota(jnp.int32, (2 * QBLK, kw_max), 0) % QBLK
               - lax.broadcasted_iota(jnp.int32, (2 * QBLK, kw_max), 1))
        for n in range(3):
            bias_scr[n] = jnp.where(jnp.abs(rel + n * reach) <= reach, 0.0, NEG)

    def group(q_ref, k_ref, v_ref, dilation, finish):
        length = seq // dilation
        kw = min(kw_max, length)
        nqb = length // QBLK

        def chunk(cid, carry):
            where = []
            for t in range(CHUNK):
                bi = cid * CHUNK + t
                r = bi // nqb
                row0 = pl.multiple_of((bi % nqb) * QBLK, QBLK)
                ks = pl.multiple_of(jnp.clip(row0 - reach, 0, length - kw), reach)
                where.append((r, row0, ks))
                q2d = q_ref[r, pl.ds(row0, QBLK), :]
                zero = jnp.zeros_like(q2d)
                qq = jnp.concatenate([jnp.where(left, q2d, zero), jnp.where(left, zero, q2d)], axis=0)
                s_scr[t, :, :kw] = lax.dot_general(
                    qq, k_ref[r, pl.ds(ks, kw), :], (((1,), (1,)), ((), ())),
                    preferred_element_type=F32)
            for t, (r, row0, ks) in enumerate(where):
                s = s_scr[t, :, :kw] + bias_scr[(row0 - ks) // reach, :, :kw]
                m = jnp.max(s, axis=-1, keepdims=True)
                p = jnp.exp2(s - m)
                den = jnp.sum(p, axis=-1, keepdims=True)
                p_scr[t, :, :kw] = p.astype(BF16)
                pv = jnp.dot(p_scr[t, :, :kw], v_ref[r, pl.ds(ks, kw), :],
                             preferred_element_type=F32)
                o = pv * (1.0 / den)
                lse = m + jnp.log2(den)
                finish(r, row0,
                       jnp.where(left, o[:QBLK], o[QBLK:]),
                       jnp.where(left, jnp.broadcast_to(lse[:QBLK], (QBLK, LANES)),
                                 jnp.broadcast_to(lse[QBLK:], (QBLK, LANES))))
            return carry

        lax.fori_loop(0, seq // (QBLK * CHUNK), chunk, 0)

    def keep(o_scr, l_scr, length):
        def finish(r, row0, o, lse):
            base = pl.multiple_of(r * length + row0, QBLK)
            o_scr[pl.ds(base, QBLK), :] = o
            l_scr[pl.ds(base, QBLK), :] = lse
        return finish

    group(q0, k0, v0, 1, keep(og0, lg0, seq))
    group(q1, k1, v1, 4, keep(og1, lg1, seq // 4))

    def combine(r, row0, o2, l2):
        o_0 = og0[pl.ds(r, QBLK, stride=16), :]
        l_0 = lg0[pl.ds(r, QBLK, stride=16), :]
        base1 = (r % 4) * (seq // 4) + r // 4
        o_1 = og1[pl.ds(base1, QBLK, stride=4), :]
        l_1 = lg1[pl.ds(base1, QBLK, stride=4), :]
        m = jnp.maximum(jnp.maximum(l_0, l_1), l2)
        e0, e1, e2 = jnp.exp2(l_0 - m), jnp.exp2(l_1 - m), jnp.exp2(l2 - m)
        o_ref[r] = ((e0 * o_0 + e1 * o_1 + e2 * o2) / (e0 + e1 + e2)).astype(BF16)

    group(q2, k2, v2, 16, combine)


def _dil_attn(nat, p4, p16, *, batch, seq, reach):
    n_hp = WIDTH // LANES
    nat4 = nat.reshape(batch, 1, seq, nat.shape[-1])

    def spec(d, kind):
        return pl.BlockSpec((None, d, seq // d, LANES),
                            lambda b, hp: (b, 0, 0, kind * n_hp + hp))

    f32_rows = pltpu.VMEM((seq, LANES), F32)
    return pl.pallas_call(
        functools.partial(_dil_kernel, seq=seq, reach=reach),
        out_shape=jax.ShapeDtypeStruct((batch, 16, seq // 16, WIDTH), BF16),
        grid=(batch, n_hp),
        in_specs=[spec(d, kind) for d in (1, 4, 16) for kind in range(3)],
        out_specs=pl.BlockSpec((None, 16, seq // 16, LANES), lambda b, hp: (b, 0, 0, hp)),
        scratch_shapes=[f32_rows, f32_rows, f32_rows, f32_rows,
                        pltpu.VMEM((3, 2 * QBLK, 2 * QBLK), F32),
                        pltpu.VMEM((CHUNK, 2 * QBLK, 2 * QBLK), F32),
                        pltpu.VMEM((CHUNK, 2 * QBLK, 2 * QBLK), BF16)],
        compiler_params=pltpu.CompilerParams(
            dimension_semantics=("arbitrary", "arbitrary"),
            vmem_limit_bytes=VMEM_LIMIT),
        name="dil_attn",
    )(nat4, nat4, nat4, p4, p4, p4, p16, p16, p16)


def _na_kernel(q_ref, k_ref, v_ref, bias_ref, o_ref, *, rows):
    kr = min(NA_ROWS, rows)
    nk = kr * GRID_W
    left = lax.broadcasted_iota(jnp.int32, (1, LANES), 1) < HEAD_DIM

    def tile(r, hp):
        cs = slice(hp * LANES, (hp + 1) * LANES)
        r_start = jnp.clip(r - kr // 2, 0, rows - kr)
        q0 = pl.multiple_of(r * GRID_W, GRID_W)
        ks = pl.multiple_of(r_start * GRID_W, GRID_W)
        variant = r_start - r + (NA_ROWS - 1)
        q2 = q_ref[pl.ds(q0, GRID_W), cs]
        k2 = k_ref[pl.ds(ks, nk), cs]
        v2 = v_ref[pl.ds(ks, nk), cs]
        outs = []
        for half, sel in enumerate((left, jnp.logical_not(left))):
            qh = jnp.where(sel, q2, jnp.zeros_like(q2))
            s = lax.dot_general(qh, k2, (((1,), (1,)), ((), ())),
                                preferred_element_type=F32)
            s = s + bias_ref[2 * hp + half, variant]
            m = jnp.max(s, axis=-1, keepdims=True)
            p = jnp.exp2(s - m)
            den = jnp.sum(p, axis=-1, keepdims=True)
            pv = jnp.dot(p.astype(BF16), v2, preferred_element_type=F32)
            outs.append(pv / den)
        o_ref[pl.ds(q0, GRID_W), cs] = jnp.where(left, outs[0], outs[1]).astype(BF16)

    for hp in range(WIDTH // LANES):
        lax.fori_loop(0, rows, lambda r, c, hp=hp: (tile(r, hp), c)[1], 0)


def _na_bias_tiles(rpb, rows):
    kr = min(NA_ROWS, rows)
    c = np.arange(GRID_W)
    dc = np.clip(c[None, :] - c[:, None], -(NA_COLS - 1), NA_COLS - 1) + NA_COLS - 1
    c_start = np.clip(c - NA_COLS // 2, 0, GRID_W - NA_COLS)
    col_mask = (c[None, :] >= c_start[:, None]) & (c[None, :] < c_start[:, None] + NA_COLS)
    t = jnp.where(col_mask[None, None], rpb[:, :, dc].astype(F32) * LOG2E, NEG)
    n_var = 2 * NA_ROWS - kr
    tiles = jnp.stack([t[:, a:a + kr] for a in range(n_var)], axis=1)
    return tiles.transpose(0, 1, 3, 2, 4).reshape(rpb.shape[0], n_var, GRID_W, kr * GRID_W)


def _na_attn(nat3, bias_tiles):
    b, s, _ = nat3.shape
    rows = s // GRID_W

    def in_spec(kind):
        return pl.BlockSpec((None, s, WIDTH), lambda i: (i, 0, NAT_NA + kind))

    out = pl.pallas_call(
        functools.partial(_na_kernel, rows=rows),
        out_shape=jax.ShapeDtypeStruct((b, s, WIDTH), BF16),
        grid=(b,),
        in_specs=[in_spec(0), in_spec(1), in_spec(2),
                  pl.BlockSpec(bias_tiles.shape, lambda i: (0, 0, 0, 0))],
        out_specs=pl.BlockSpec((None, s, WIDTH), lambda i: (i, 0, 0)),
        compiler_params=pltpu.CompilerParams(
            dimension_semantics=("arbitrary",), vmem_limit_bytes=VMEM_LIMIT),
        name="na_attn",
    )(nat3, nat3, nat3, bias_tiles)
    return out.reshape(b * s, WIDTH)


def _mem_kernel(q_ref, kv_ref, o_ref, *, seq):
    mem_width = MEM_HEADS * MEM_HEAD_DIM

    def tile(qb, h):
        cs = slice(h * MEM_HEAD_DIM, (h + 1) * MEM_HEAD_DIM)
        q0 = pl.multiple_of(qb * QBLK, QBLK)
        q = q_ref[pl.ds(q0, QBLK), cs]
        k = kv_ref[:, cs]
        v = kv_ref[:, mem_width + h * MEM_HEAD_DIM: mem_width + (h + 1) * MEM_HEAD_DIM]
        s = lax.dot_general(q, k, (((1,), (1,)), ((), ())), preferred_element_type=F32)
        m = jnp.max(s, axis=-1, keepdims=True)
        p = jnp.exp2(s - m)
        den = jnp.sum(p, axis=-1, keepdims=True)
        pv = jnp.dot(p.astype(BF16), v, preferred_element_type=F32)
        o_ref[pl.ds(q0, QBLK), cs] = (pv / den).astype(BF16)

    for h in range(MEM_HEADS):
        lax.fori_loop(0, seq // QBLK, lambda qb, c, h=h: (tile(qb, h), c)[1], 0)


def _mem_attn(nat3, kv_m):
    b, s, _ = nat3.shape
    mem_len, kv_w = kv_m.shape[1:]
    out = pl.pallas_call(
        functools.partial(_mem_kernel, seq=s),
        out_shape=jax.ShapeDtypeStruct((b, s, WIDTH), BF16),
        grid=(b,),
        in_specs=[pl.BlockSpec((None, s, WIDTH), lambda i: (i, 0, NAT_MEMQ)),
                  pl.BlockSpec((None, mem_len, kv_w), lambda i: (i, 0, 0))],
        out_specs=pl.BlockSpec((None, s, WIDTH), lambda i: (i, 0, 0)),
        compiler_params=pltpu.CompilerParams(
            dimension_semantics=("arbitrary",), vmem_limit_bytes=VMEM_LIMIT),
        name="mem_attn",
    )(nat3, kv_m)
    return out.reshape(b * s, WIDTH)


def _sigmoid(z):
    return 1.0 / (1.0 + jnp.exp(-z))


def _merge_kernel(oa, ga, za, ob, oc, gb, gc, zb, zc,
                  mb_ref, wa_ref, wb_ref, wc_ref, wo_ref, pn_ref, x_ref, out_ref, y_scr, *, tm):
    def branch(o, g, w_ref, z, idx):
        g = g.astype(F32)
        u = (o.astype(F32) * (g * _sigmoid(g))).astype(BF16)
        yb = jnp.dot(u, w_ref[...], preferred_element_type=F32)
        return _sigmoid(z.astype(F32) + mb_ref[idx:idx + 1, :]) * yb

    seg = tm // 16
    ya = branch(oa[...].reshape(tm, WIDTH), ga[...].reshape(tm, WIDTH), wa_ref,
                za[...].reshape(tm, D_MODEL), 0)
    n_lane_tiles = D_MODEL // LANES
    for r in range(16):
        for c in range(n_lane_tiles):
            y_scr[c, pl.ds(r, seg, stride=16), :] = ya[r * seg:(r + 1) * seg,
                                                       c * LANES:(c + 1) * LANES]
    ya_nat = jnp.concatenate([y_scr[c] for c in range(n_lane_tiles)], axis=-1)

    y = (ya_nat + branch(ob[...], gb[...], wb_ref, zb[...], 1)
         + branch(oc[...], gc[...], wc_ref, zc[...], 2))
    z = jnp.dot(y.astype(BF16), wo_ref[...], preferred_element_type=F32)
    ms = jnp.mean(z * z, axis=-1, keepdims=True)
    out_ref[...] = x_ref[...] + z * lax.rsqrt(ms + EPS) * pn_ref[...]


def _merge(x2d, nat, p16, out_a16, out_b, out_c, merge_bias, wa, wb, wc, wo, post_norm,
           *, seq, tm):
    rows, dm = x2d.shape
    per_b = seq // tm
    seg = tm // 16

    def p16_spec(width, col):
        return pl.BlockSpec((None, 16, seg, width), lambda i: (i // per_b, 0, i % per_b, col))

    def nat_spec(width, col):
        return pl.BlockSpec((tm, width), lambda i: (i, col))

    def full(shape):
        return pl.BlockSpec(shape, lambda i: (0, 0))

    return pl.pallas_call(
        functools.partial(_merge_kernel, tm=tm),
        out_shape=jax.ShapeDtypeStruct((rows, dm), F32),
        grid=(rows // tm,),
        in_specs=[p16_spec(WIDTH, 0), p16_spec(WIDTH, P16_GATE_A),
                  p16_spec(dm, P16_MERGE_A * WIDTH // dm),
                  nat_spec(WIDTH, 0), nat_spec(WIDTH, 0),
                  nat_spec(WIDTH, NAT_GATE_B), nat_spec(WIDTH, NAT_GATE_B + 1),
                  nat_spec(dm, NAT_MERGE_B * WIDTH // dm),
                  nat_spec(dm, NAT_MERGE_B * WIDTH // dm + 1),
                  full(merge_bias.shape), full(wa.shape), full(wb.shape), full(wc.shape),
                  full(wo.shape), full(post_norm.shape), nat_spec(dm, 0)],
        out_specs=pl.BlockSpec((tm, dm), lambda i: (i, 0)),
        scratch_shapes=[pltpu.VMEM((dm // LANES, tm, LANES), F32)],
        compiler_params=pltpu.CompilerParams(
            dimension_semantics=("arbitrary",), vmem_limit_bytes=VMEM_LIMIT),
        name="merge",
    )(out_a16, p16, p16, out_b, out_c, nat, nat, nat, nat,
      merge_bias, wa, wb, wc, wo, post_norm, x2d)


def kernel(x, mem, pre_norm, w_in, merge_bias, na_rpb, mem_norm, w_mem_kv,
           w_branch_a, w_branch_b, w_branch_c, w_out, post_norm):
    b, s, dm = x.shape
    depth = pre_norm.shape[0]
    reach = (DIL_CONFIGS[0][0] // 2) // DIL_CONFIGS[0][1]
    assert all((w // 2) // d == reach for w, d in DIL_CONFIGS) and 2 * reach == QBLK
    assert tuple(d for _, d in DIL_CONFIGS) == (1, 4, 16) and NAT_MERGE_B * WIDTH % dm == 0
    tm_proj = 1024
    cos_t, sin_t = _rope_tables(s, tm_proj)
    col_tiles = np.concatenate([np.arange(t * WIDTH, (t + 1) * WIDTH)
                                for t in NAT_TILES + P4_TILES + P16_TILES])
    for layer in range(depth):
        x2d = x.reshape(b * s, dm)
        w_tiles = w_in[layer][:, col_tiles].astype(BF16)
        nat, p4, p16 = _proj(x2d, pre_norm[layer][None], w_tiles, cos_t, sin_t,
                             batch=b, seq=s, tm=tm_proj)
        nat3 = nat.reshape(b, s, nat.shape[-1])

        out_a16 = _dil_attn(nat, p4, p16, batch=b, seq=s, reach=reach)
        out_b = _na_attn(nat3, _na_bias_tiles(na_rpb[layer], s // GRID_W))

        mem2d = mem.reshape(b * mem.shape[1], dm)
        kv_m = _kv_proj(mem2d, mem_norm[layer][None], w_mem_kv[layer].astype(BF16), tm=1024)
        out_c = _mem_attn(nat3, kv_m.reshape(b, mem.shape[1], -1))

        y = _merge(x2d, nat, p16, out_a16, out_b, out_c, merge_bias[layer],
                   w_branch_a[layer].astype(BF16), w_branch_b[layer].astype(BF16),
                   w_branch_c[layer].astype(BF16), w_out[layer].astype(BF16),
                   post_norm[layer][None], seq=s, tm=256)
        x = y.reshape(b, s, dm)
    return x
```

```python
import functools
import math

import jax
import jax.numpy as jnp
import numpy as np
from jax import lax
from jax.experimental import pallas as pl
from jax.experimental.pallas import tpu as pltpu

D_MODEL = 1024
HEAD_DIM = 64
DIL_CONFIGS = ((128, 1), (512, 4), (2048, 16))
WIDTH = 512
NA_ROWS = 8
NA_COLS = 16
GRID_W = 64
MEM_HEADS = 4
MEM_HEAD_DIM = 128
ROPE_THETA = 500000.0
ROPE_DIM = HEAD_DIM // 4
ROPE_HALF = ROPE_DIM // 2
EPS = 1e-6
NEG = -1e30
LOG2E = math.log2(math.e)

LANES = 128
QBLK = 128
CHUNK = 4
PROJ_ROWS = 512
VMEM_LIMIT = 56 * 1024 * 1024

_QS = HEAD_DIM ** -0.5 * LOG2E
_MS = MEM_HEAD_DIM ** -0.5 * LOG2E
STEPS = (
    (0, 0, True, _QS), (1, 0, True, 1.0), (2, 0, False, 1.0),
    (9, 0, False, _QS), (10, 0, False, 1.0), (11, 0, False, 1.0),
    (12, 0, False, _MS),
    (14, 0, False, 1.0), (15, 0, False, 1.0),
    (18, 0, False, 1.0), (19, 0, False, 1.0), (20, 0, False, 1.0), (21, 0, False, 1.0),
    (3, 1, True, _QS), (4, 1, True, 1.0), (5, 1, False, 1.0),
    (6, 2, True, _QS), (7, 2, True, 1.0), (8, 2, False, 1.0),
    (13, 2, False, 1.0), (16, 2, False, 1.0), (17, 2, False, 1.0),
)
N_STEPS = len(STEPS)
T_G0, T_NA, T_MEMQ, T_GATE_B, T_MERGE_B = 0, 3, 6, 7, 9
T_G1, T_G2, T_GATE_A, T_MERGE_A = 13, 16, 19, 20
SLOT_DIL = (1, 4, 16)

BF16 = jnp.bfloat16
F32 = jnp.float32


def _pair_layout_columns():
    a, b = np.arange(HEAD_DIM), HEAD_DIM + np.arange(HEAD_DIM)
    return np.concatenate([a[:ROPE_HALF], b[:ROPE_HALF], a[ROPE_DIM:],
                           a[ROPE_HALF:ROPE_DIM], b[ROPE_HALF:ROPE_DIM], b[ROPE_DIM:]])


def _head_a_lanes():
    lane = lax.broadcasted_iota(jnp.int32, (1, LANES), 1)
    return jnp.logical_or(lane < ROPE_HALF,
                          jnp.logical_and(lane >= ROPE_DIM, lane < HEAD_DIM + ROPE_HALF))


def _proj_kernel(scale_ref, x_ref, g_ref, w_ref, cos_ref, sin_ref, o_ref, h_ref, hf_ref, *, tm):
    j = pl.program_id(1)
    n_lane_tiles = x_ref.shape[1] // LANES

    @pl.when(j == 0)
    def _():
        rb = 256
        ssq = [jnp.sum(jnp.square(x_ref[r0:r0 + rb, :]), axis=-1, keepdims=True)
               for r0 in range(0, tm, rb)]
        rs = lax.rsqrt(jnp.concatenate(ssq, axis=0) * (1.0 / x_ref.shape[1]) + EPS)
        for c in range(n_lane_tiles):
            cs = slice(c * LANES, (c + 1) * LANES)
            hf = x_ref[:, cs] * rs * g_ref[:, cs]
            h_ref[0, :, cs] = hf.astype(BF16)
            hf_ref[...] = hf
            for slot, d in ((1, 4), (2, 16)):
                seg = tm // d
                for r in range(d):
                    h_ref[slot, r * seg:(r + 1) * seg, cs] = (
                        hf_ref[pl.ds(r, seg, stride=d), :].astype(BF16))

    slot = (j >= T_G1).astype(jnp.int32) + (j >= T_G2).astype(jnp.int32)
    q_scale = scale_ref[j]
    for r0 in range(0, tm, PROJ_ROWS):
        rows = pl.ds(r0, PROJ_ROWS)
        acc = jnp.dot(h_ref[slot, rows, :], w_ref[...], preferred_element_type=F32)
        c, s = cos_ref[rows, :], sin_ref[rows, :]
        for t in range(WIDTH // LANES):
            a = acc[:, t * LANES:(t + 1) * LANES]
            o_ref[rows, t * LANES:(t + 1) * LANES] = (
                (a * c + pltpu.roll(a, HEAD_DIM, 1) * s) * q_scale).astype(BF16)


def _proj(x2d, gain, w_steps, scales, cos_t, sin_t, *, batch, seq):
    rows, dm = x2d.shape
    tm = seq

    def table_of(j):
        slot = (j >= T_G1).astype(jnp.int32) + (j >= T_G2).astype(jnp.int32)
        is_rope = functools.reduce(
            jnp.logical_or, [j == t for t, st in enumerate(STEPS) if st[2]])
        return jnp.where(is_rope, 1 + slot, 0)

    tab_spec = pl.BlockSpec((None, tm, LANES), lambda i, j, sc: (table_of(j), 0, 0))
    return pl.pallas_call(
        functools.partial(_proj_kernel, tm=tm),
        out_shape=jax.ShapeDtypeStruct((batch, N_STEPS, seq, WIDTH), BF16),
        grid_spec=pltpu.PrefetchScalarGridSpec(
            num_scalar_prefetch=1,
            grid=(rows // tm, N_STEPS),
            in_specs=[
                pl.BlockSpec((tm, dm), lambda i, j, sc: (i, 0)),
                pl.BlockSpec((1, dm), lambda i, j, sc: (0, 0)),
                pl.BlockSpec((dm, WIDTH), lambda i, j, sc: (0, j)),
                tab_spec, tab_spec,
            ],
            out_specs=pl.BlockSpec((None, None, tm, WIDTH), lambda i, j, sc: (i, j, 0, 0)),
            scratch_shapes=[pltpu.VMEM((3, tm, dm), BF16), pltpu.VMEM((tm, LANES), F32)]),
        compiler_params=pltpu.CompilerParams(
            dimension_semantics=("arbitrary", "arbitrary"),
            vmem_limit_bytes=VMEM_LIMIT),
        name="proj",
    )(scales, x2d, gain, w_steps, cos_t, sin_t)


def _rope_tables(seq):
    pos = jnp.arange(seq, dtype=F32)
    inv = ROPE_THETA ** (-jnp.arange(ROPE_HALF, dtype=F32) * 2.0 / ROPE_DIM)
    ang = pos[:, None] * inv[None, :]
    cos, sin = jnp.cos(ang), jnp.sin(ang)
    rest = LANES // 2 - ROPE_DIM
    c = jnp.concatenate([cos, cos, jnp.ones((seq, rest), F32)] * 2, axis=-1)
    s = jnp.concatenate([-sin, -sin, jnp.zeros((seq, rest), F32),
                         sin, sin, jnp.zeros((seq, rest), F32)], axis=-1)

    def orders(t, identity):
        out = [jnp.full((seq, LANES), identity, F32), t]
        for d in SLOT_DIL[1:]:
            out.append(t.reshape(seq // d, d, LANES).transpose(1, 0, 2).reshape(seq, LANES))
        return jnp.stack(out)

    return orders(c, 1.0), orders(s, 0.0)


def _proj_weights(w):
    pair = _pair_layout_columns()
    cols = []
    for tile, _, rope, _ in STEPS:
        base = tile * WIDTH + np.arange(WIDTH)
        if rope:
            base = np.concatenate([tile * WIDTH + p * LANES + pair
                                   for p in range(WIDTH // LANES)])
        cols.append(base)
    cols = np.concatenate(cols)
    pieces, start = [], 0
    for i in range(1, len(cols) + 1):
        if i == len(cols) or cols[i] != cols[i - 1] + 1:
            pieces.append(w[:, int(cols[start]):int(cols[i - 1]) + 1])
            start = i
    return jnp.concatenate(pieces, axis=1).astype(BF16)


def _kv_kernel(x_ref, g_ref, w_ref, o_ref):
    xf = x_ref[...]
    ms = jnp.mean(xf * xf, axis=-1, keepdims=True)
    h = (xf * lax.rsqrt(ms + EPS) * g_ref[...]).astype(BF16)
    o_ref[...] = jnp.dot(h, w_ref[...], preferred_element_type=F32).astype(BF16)


def _kv_proj(mem2d, gain, w_bf16, *, tm):
    rows, dm = mem2d.shape
    n_out = w_bf16.shape[1]
    return pl.pallas_call(
        _kv_kernel,
        out_shape=jax.ShapeDtypeStruct((rows, n_out), BF16),
        grid=(rows // tm,),
        in_specs=[pl.BlockSpec((tm, dm), lambda i: (i, 0)),
                  pl.BlockSpec((1, dm), lambda i: (0, 0)),
                  pl.BlockSpec((dm, n_out), lambda i: (0, 0))],
        out_specs=pl.BlockSpec((tm, n_out), lambda i: (i, 0)),
        compiler_params=pltpu.CompilerParams(
            dimension_semantics=("arbitrary",), vmem_limit_bytes=VMEM_LIMIT),
        name="kv_proj",
    )(mem2d, gain, w_bf16)


def _dil_kernel(q0, k0, v0, q1, k1, v1, q2, k2, v2, o_ref,
                og0, lg0, og1, lg1, bias_scr, s_scr, p_scr, *, seq, reach):
    left = lax.broadcasted_iota(jnp.int32, (1, LANES), 1) < HEAD_DIM
    head_a = _head_a_lanes()
    kw_max = 2 * QBLK

    @pl.when(jnp.logical_and(pl.program_id(0) == 0, pl.program_id(1) == 0))
    def _():
        rel = (lax.broadcasted_iota(jnp.int32, (2 * QBLK, kw_max), 0) % QBLK
               - lax.broadcasted_iota(jnp.int32, (2 * QBLK, kw_max), 1))
        for n in range(3):
            bias_scr[n] = jnp.where(jnp.abs(rel + n * reach) <= reach, 0.0, NEG)

    def group(q_ref, k_ref, v_ref, dilation, finish):
        length = seq // dilation
        kw = min(kw_max, length)
        nqb = length // QBLK

        def chunk(cid, carry):
            where = []
            for t in range(CHUNK):
                bi = cid * CHUNK + t
                r = bi // nqb
                row0 = pl.multiple_of((bi % nqb) * QBLK, QBLK)
                ks = pl.multiple_of(jnp.clip(row0 - reach, 0, length - kw), reach)
                where.append((r, row0, ks))
                q2d = q_ref[r, pl.ds(row0, QBLK), :]
                zero = jnp.zeros_like(q2d)
                qq = jnp.concatenate([jnp.where(head_a, q2d, zero),
                                      jnp.where(head_a, zero, q2d)], axis=0)
                s_scr[t, :, :kw] = lax.dot_general(
                    qq, k_ref[r, pl.ds(ks, kw), :], (((1,), (1,)), ((), ())),
                    preferred_element_type=F32)
            for t, (r, row0, ks) in enumerate(where):
                s = s_scr[t, :, :kw] + bias_scr[(row0 - ks) // reach, :, :kw]
                m = jnp.max(s, axis=-1, keepdims=True)
                p = jnp.exp2(s - m)
                den = jnp.sum(p, axis=-1, keepdims=True)
                p_scr[t, :, :kw] = p.astype(BF16)
                pv = jnp.dot(p_scr[t, :, :kw], v_ref[r, pl.ds(ks, kw), :],
                             preferred_element_type=F32)
                o = pv * (1.0 / den)
                lse = m + jnp.log2(den)
                finish(r, row0,
                       jnp.where(left, o[:QBLK], o[QBLK:]),
                       jnp.where(left, jnp.broadcast_to(lse[:QBLK], (QBLK, LANES)),
                                 jnp.broadcast_to(lse[QBLK:], (QBLK, LANES))))
            return carry

        lax.fori_loop(0, seq // (QBLK * CHUNK), chunk, 0)

    def keep(o_scr, l_scr, length):
        def finish(r, row0, o, lse):
            base = pl.multiple_of(r * length + row0, QBLK)
            o_scr[pl.ds(base, QBLK), :] = o
            l_scr[pl.ds(base, QBLK), :] = lse
        return finish

    group(q0, k0, v0, 1, keep(og0, lg0, seq))
    group(q1, k1, v1, 4, keep(og1, lg1, seq // 4))

    def combine(r, row0, o2, l2):
        o_0 = og0[pl.ds(r, QBLK, stride=16), :]
        l_0 = lg0[pl.ds(r, QBLK, stride=16), :]
        base1 = (r % 4) * (seq // 4) + r // 4
        o_1 = og1[pl.ds(base1, QBLK, stride=4), :]
        l_1 = lg1[pl.ds(base1, QBLK, stride=4), :]
        m = jnp.maximum(jnp.maximum(l_0, l_1), l2)
        e0, e1, e2 = jnp.exp2(l_0 - m), jnp.exp2(l_1 - m), jnp.exp2(l2 - m)
        o_ref[r] = ((e0 * o_0 + e1 * o_1 + e2 * o2) / (e0 + e1 + e2)).astype(BF16)

    group(q2, k2, v2, 16, combine)


def _dil_attn(proj, *, reach):
    batch, _, seq, _ = proj.shape
    n_hp = WIDTH // LANES

    def spec(first_tile, d, kind):
        return pl.BlockSpec((None, None, d, seq // d, LANES),
                            lambda b, hp: (b, first_tile + kind, 0, 0, hp))

    views = {d: proj.reshape(batch, N_STEPS, d, seq // d, WIDTH) for d in SLOT_DIL}
    f32_rows = pltpu.VMEM((seq, LANES), F32)
    return pl.pallas_call(
        functools.partial(_dil_kernel, seq=seq, reach=reach),
        out_shape=jax.ShapeDtypeStruct((batch, 16, seq // 16, WIDTH), BF16),
        grid=(batch, n_hp),
        in_specs=[spec(t0, d, kind) for t0, d in zip((T_G0, T_G1, T_G2), SLOT_DIL)
                  for kind in range(3)],
        out_specs=pl.BlockSpec((None, 16, seq // 16, LANES), lambda b, hp: (b, 0, 0, hp)),
        scratch_shapes=[f32_rows, f32_rows, f32_rows, f32_rows,
                        pltpu.VMEM((3, 2 * QBLK, 2 * QBLK), F32),
                        pltpu.VMEM((CHUNK, 2 * QBLK, 2 * QBLK), F32),
                        pltpu.VMEM((CHUNK, 2 * QBLK, 2 * QBLK), BF16)],
        compiler_params=pltpu.CompilerParams(
            dimension_semantics=("arbitrary", "arbitrary"),
            vmem_limit_bytes=VMEM_LIMIT),
        name="dil_attn",
    )(*[views[d] for d in SLOT_DIL for _ in range(3)])


def _na_kernel(q_ref, k_ref, v_ref, bias_ref, o_ref, s_scr, p_scr, *, rows):
    kr = min(NA_ROWS, rows)
    nk = kr * GRID_W
    left = lax.broadcasted_iota(jnp.int32, (1, LANES), 1) < HEAD_DIM

    def chunk(cid, carry):
        where = []
        for t in range(CHUNK):
            r = cid * CHUNK + t
            r_start = jnp.clip(r - kr // 2, 0, rows - kr)
            row0 = pl.multiple_of(r * GRID_W, GRID_W)
            ks = pl.multiple_of(r_start * GRID_W, GRID_W)
            where.append((row0, ks, r_start - r + (NA_ROWS - 1)))
            q2d = q_ref[pl.ds(row0, GRID_W), :]
            zero = jnp.zeros_like(q2d)
            qq = jnp.concatenate([jnp.where(left, q2d, zero), jnp.where(left, zero, q2d)], axis=0)
            s_scr[t] = lax.dot_general(qq, k_ref[pl.ds(ks, nk), :], (((1,), (1,)), ((), ())),
                                       preferred_element_type=F32)
        for t, (row0, ks, variant) in enumerate(where):
            s = s_scr[t] + jnp.concatenate([bias_ref[0, variant], bias_ref[1, variant]], axis=0)
            m = jnp.max(s, axis=-1, keepdims=True)
            p = jnp.exp2(s - m)
            den = jnp.sum(p, axis=-1, keepdims=True)
            p_scr[t] = p.astype(BF16)
            pv = jnp.dot(p_scr[t], v_ref[pl.ds(ks, nk), :], preferred_element_type=F32)
            o = pv * (1.0 / den)
            o_ref[pl.ds(row0, GRID_W), :] = jnp.where(left, o[:GRID_W], o[GRID_W:]).astype(BF16)
        return carry

    lax.fori_loop(0, rows // CHUNK, chunk, 0)


def _na_bias_tiles(rpb, rows):
    kr = min(NA_ROWS, rows)
    c = np.arange(GRID_W)
    dc = np.clip(c[None, :] - c[:, None], -(NA_COLS - 1), NA_COLS - 1) + NA_COLS - 1
    c_start = np.clip(c - NA_COLS // 2, 0, GRID_W - NA_COLS)
    col_mask = (c[None, :] >= c_start[:, None]) & (c[None, :] < c_start[:, None] + NA_COLS)
    t = jnp.where(col_mask[None, None], rpb[:, :, dc].astype(F32) * LOG2E, NEG)
    n_var = 2 * NA_ROWS - kr
    tiles = jnp.stack([t[:, a:a + kr] for a in range(n_var)], axis=1)
    return tiles.transpose(0, 1, 3, 2, 4).reshape(rpb.shape[0], n_var, GRID_W, kr * GRID_W)


def _na_attn(proj, bias_tiles):
    batch, _, seq, _ = proj.shape
    rows = seq // GRID_W
    nk = min(NA_ROWS, rows) * GRID_W
    n_hp = WIDTH // LANES

    def in_spec(kind):
        return pl.BlockSpec((None, None, seq, LANES), lambda b, hp: (b, T_NA + kind, 0, hp))

    return pl.pallas_call(
        functools.partial(_na_kernel, rows=rows),
        out_shape=jax.ShapeDtypeStruct((batch, seq, WIDTH), BF16),
        grid=(batch, n_hp),
        in_specs=[in_spec(0), in_spec(1), in_spec(2),
                  pl.BlockSpec((2,) + bias_tiles.shape[1:], lambda b, hp: (hp, 0, 0, 0))],
        out_specs=pl.BlockSpec((None, seq, LANES), lambda b, hp: (b, 0, hp)),
        scratch_shapes=[pltpu.VMEM((CHUNK, 2 * GRID_W, nk), F32),
                        pltpu.VMEM((CHUNK, 2 * GRID_W, nk), BF16)],
        compiler_params=pltpu.CompilerParams(
            dimension_semantics=("arbitrary", "arbitrary"), vmem_limit_bytes=VMEM_LIMIT),
        name="na_attn",
    )(proj, proj, proj, bias_tiles)


def _mem_kernel(q_ref, kv_ref, o_ref, s_scr, p_scr, rden_scr, *, seq):
    mem_width = MEM_HEADS * MEM_HEAD_DIM
    rb = 512

    for h in range(MEM_HEADS):
        cs = slice(h * MEM_HEAD_DIM, (h + 1) * MEM_HEAD_DIM)
        vs = slice(mem_width + h * MEM_HEAD_DIM, mem_width + (h + 1) * MEM_HEAD_DIM)
        s_scr[...] = lax.dot_general(q_ref[:, cs], kv_ref[:, cs], (((1,), (1,)), ((), ())),
                                     preferred_element_type=F32)

        def softmax_rows(i, carry):
            for t in range(rb // QBLK):
                r0 = pl.multiple_of(i * rb + t * QBLK, QBLK)
                s = s_scr[pl.ds(r0, QBLK), :]
                m = jnp.max(s, axis=-1, keepdims=True)
                p = jnp.exp2(s - m)
                den = jnp.sum(p, axis=-1, keepdims=True)
                p_scr[pl.ds(r0, QBLK), :] = p.astype(BF16)
                rden_scr[pl.ds(r0, QBLK), :] = jnp.broadcast_to(1.0 / den, (QBLK, LANES))
            return carry

        lax.fori_loop(0, seq // rb, softmax_rows, 0)
        pv = jnp.dot(p_scr[...], kv_ref[:, vs], preferred_element_type=F32)
        o_ref[:, cs] = (pv * rden_scr[...]).astype(BF16)


def _mem_attn(proj, kv_m):
    batch, _, seq, _ = proj.shape
    mem_len, kv_w = kv_m.shape[1:]
    return pl.pallas_call(
        functools.partial(_mem_kernel, seq=seq),
        out_shape=jax.ShapeDtypeStruct((batch, seq, WIDTH), BF16),
        grid=(batch,),
        in_specs=[pl.BlockSpec((None, None, seq, WIDTH), lambda b: (b, T_MEMQ, 0, 0)),
                  pl.BlockSpec((None, mem_len, kv_w), lambda b: (b, 0, 0))],
        out_specs=pl.BlockSpec((None, seq, WIDTH), lambda b: (b, 0, 0)),
        scratch_shapes=[pltpu.VMEM((seq, mem_len), F32), pltpu.VMEM((seq, mem_len), BF16),
                        pltpu.VMEM((seq, LANES), F32)],
        compiler_params=pltpu.CompilerParams(
            dimension_semantics=("arbitrary",), vmem_limit_bytes=VMEM_LIMIT),
        name="mem_attn",
    )(proj, kv_m)


def _sigmoid(z):
    return 1.0 / (1.0 + jnp.exp(-z))


def _merge_kernel(oa, ga, za0, za1, ob, oc, gb, gc, zb0, zb1, zc0, zc1,
                  mb_ref, wa_ref, wb_ref, wc_ref, wo_ref, pn_ref, x_ref, out_ref, y_scr, *, tm):
    def branch(o, g, w_ref, z, idx):
        g = g.astype(F32)
        u = (o.astype(F32) * (g * _sigmoid(g))).astype(BF16)
        yb = jnp.dot(u, w_ref[...], preferred_element_type=F32)
        return _sigmoid(z.astype(F32) + mb_ref[idx:idx + 1, :]) * yb

    def logits(z_lo, z_hi, rows):
        return jnp.concatenate([z_lo[...].reshape(rows, WIDTH), z_hi[...].reshape(rows, WIDTH)],
                               axis=-1)

    seg = tm // 16
    ya = branch(oa[...].reshape(tm, WIDTH), ga[...].reshape(tm, WIDTH), wa_ref,
                logits(za0, za1, tm), 0)
    n_lane_tiles = D_MODEL // LANES
    for r in range(16):
        for c in range(n_lane_tiles):
            y_scr[c, pl.ds(r, seg, stride=16), :] = ya[r * seg:(r + 1) * seg,
                                                       c * LANES:(c + 1) * LANES]
    ya_nat = jnp.concatenate([y_scr[c] for c in range(n_lane_tiles)], axis=-1)

    y = (ya_nat + branch(ob[...], gb[...], wb_ref, logits(zb0, zb1, tm), 1)
         + branch(oc[...], gc[...], wc_ref, logits(zc0, zc1, tm), 2))
    z = jnp.dot(y.astype(BF16), wo_ref[...], preferred_element_type=F32)
    ms = jnp.mean(z * z, axis=-1, keepdims=True)
    out_ref[...] = x_ref[...] + z * lax.rsqrt(ms + EPS) * pn_ref[...]


def _merge(x2d, proj, out_a16, out_b, out_c, merge_bias, wa, wb, wc, wo, post_norm, *, tm):
    rows, dm = x2d.shape
    batch, _, seq, _ = proj.shape
    per_b = seq // tm
    seg = tm // 16
    proj16 = proj.reshape(batch, N_STEPS, 16, seq // 16, WIDTH)

    def p16_spec(tile):
        return pl.BlockSpec((None, None, 16, seg, WIDTH),
                            lambda i: (i // per_b, tile, 0, i % per_b, 0))

    def nat_spec(tile):
        return pl.BlockSpec((None, None, tm, WIDTH), lambda i: (i // per_b, tile, i % per_b, 0))

    def rows_spec(width):
        return pl.BlockSpec((tm, width), lambda i: (i, 0))

    def full(shape):
        return pl.BlockSpec(shape, lambda i: (0, 0))

    return pl.pallas_call(
        functools.partial(_merge_kernel, tm=tm),
        out_shape=jax.ShapeDtypeStruct((rows, dm), F32),
        grid=(rows // tm,),
        in_specs=[pl.BlockSpec((None, 16, seg, WIDTH), lambda i: (i // per_b, 0, i % per_b, 0)),
                  p16_spec(T_GATE_A), p16_spec(T_MERGE_A), p16_spec(T_MERGE_A + 1),
                  rows_spec(WIDTH), rows_spec(WIDTH),
                  nat_spec(T_GATE_B), nat_spec(T_GATE_B + 1),
                  nat_spec(T_MERGE_B), nat_spec(T_MERGE_B + 1),
                  nat_spec(T_MERGE_B + 2), nat_spec(T_MERGE_B + 3),
                  full(merge_bias.shape), full(wa.shape), full(wb.shape), full(wc.shape),
                  full(wo.shape), full(post_norm.shape), rows_spec(dm)],
        out_specs=pl.BlockSpec((tm, dm), lambda i: (i, 0)),
        scratch_shapes=[pltpu.VMEM((dm // LANES, tm, LANES), F32)],
        compiler_params=pltpu.CompilerParams(
            dimension_semantics=("arbitrary",), vmem_limit_bytes=VMEM_LIMIT),
        name="merge",
    )(out_a16, proj16, proj16, proj16, out_b, out_c, proj, proj, proj, proj, proj, proj,
      merge_bias, wa, wb, wc, wo, post_norm, x2d)


def kernel(x, mem, pre_norm, w_in, merge_bias, na_rpb, mem_norm, w_mem_kv,
           w_branch_a, w_branch_b, w_branch_c, w_out, post_norm):
    b, s, dm = x.shape
    depth = pre_norm.shape[0]
    reach = (DIL_CONFIGS[0][0] // 2) // DIL_CONFIGS[0][1]
    assert all((w // 2) // d == reach for w, d in DIL_CONFIGS) and 2 * reach == QBLK
    assert tuple(d for _, d in DIL_CONFIGS) == SLOT_DIL
    cos_t, sin_t = _rope_tables(s)
    scales = jnp.asarray([st[3] for st in STEPS], F32)
    for layer in range(depth):
        x2d = x.reshape(b * s, dm)
        proj = _proj(x2d, pre_norm[layer][None], _proj_weights(w_in[layer]), scales,
                     cos_t, sin_t, batch=b, seq=s)

        out_a16 = _dil_attn(proj, reach=reach)
        out_b = _na_attn(proj, _na_bias_tiles(na_rpb[layer], s // GRID_W))

        mem2d = mem.reshape(b * mem.shape[1], dm)
        kv_m = _kv_proj(mem2d, mem_norm[layer][None], w_mem_kv[layer].astype(BF16), tm=1024)
        out_c = _mem_attn(proj, kv_m.reshape(b, mem.shape[1], -1))

        y = _merge(x2d, proj, out_a16, out_b.reshape(b * s, WIDTH), out_c.reshape(b * s, WIDTH),
                   merge_bias[layer],
                   w_branch_a[layer].astype(BF16), w_branch_b[layer].astype(BF16),
                   w_branch_c[layer].astype(BF16), w_out[layer].astype(BF16),
                   post_norm[layer][None], tm=256)
        x = y.reshape(b, s, dm)
    return x
```

```python
import functools
import math

import jax
import jax.numpy as jnp
import numpy as np
from jax import lax
from jax.experimental import pallas as pl
from jax.experimental.pallas import tpu as pltpu

D_MODEL = 1024
HEAD_DIM = 64
DIL_CONFIGS = ((128, 1), (512, 4), (2048, 16))
WIDTH = 512
NA_ROWS = 8
NA_COLS = 16
GRID_W = 64
MEM_HEADS = 4
MEM_HEAD_DIM = 128
ROPE_THETA = 500000.0
ROPE_DIM = HEAD_DIM // 4
ROPE_HALF = ROPE_DIM // 2
EPS = 1e-6
NEG = -1e30
LOG2E = math.log2(math.e)

LANES = 128
QBLK = 128
CHUNK = 4
PROJ_ROWS = 512
VMEM_LIMIT = 56 * 1024 * 1024

_QS = HEAD_DIM ** -0.5 * LOG2E
_MS = MEM_HEAD_DIM ** -0.5 * LOG2E
STEPS = (
    (0, 0, True, _QS), (1, 0, True, 1.0), (2, 0, False, 1.0),
    (9, 0, False, _QS), (10, 0, False, 1.0), (11, 0, False, 1.0),
    (12, 0, False, _MS),
    (14, 0, False, 1.0), (15, 0, False, 1.0),
    (18, 0, False, 1.0), (19, 0, False, 1.0), (20, 0, False, 1.0), (21, 0, False, 1.0),
    (3, 1, True, _QS), (4, 1, True, 1.0), (5, 1, False, 1.0),
    (6, 2, True, _QS), (7, 2, True, 1.0), (8, 2, False, 1.0),
    (13, 2, False, 1.0), (16, 2, False, 1.0), (17, 2, False, 1.0),
)
N_STEPS = len(STEPS)
N_ROPE = sum(st[2] for st in STEPS)
T_G0, T_NA, T_MEMQ, T_GATE_B, T_MERGE_B = 0, 3, 6, 7, 9
T_G1, T_G2, T_GATE_A, T_MERGE_A = 13, 16, 19, 20
SLOT_DIL = (1, 4, 16)

BF16 = jnp.bfloat16
F32 = jnp.float32


def _pair_layout_columns():
    a, b = np.arange(HEAD_DIM), HEAD_DIM + np.arange(HEAD_DIM)
    return np.concatenate([a[:ROPE_HALF], b[:ROPE_HALF], a[ROPE_DIM:],
                           a[ROPE_HALF:ROPE_DIM], b[ROPE_HALF:ROPE_DIM], b[ROPE_DIM:]])


def _head_a_lanes():
    lane = lax.broadcasted_iota(jnp.int32, (1, LANES), 1)
    return jnp.logical_or(lane < ROPE_HALF,
                          jnp.logical_and(lane >= ROPE_DIM, lane < HEAD_DIM + ROPE_HALF))


def _proj_kernel(wtile_ref, rtile_ref, rope_ref, scale_ref, x_ref, g_ref, w_ref, wr_ref,
                 cos_ref, sin_ref, o_ref, h_ref, hf_ref, wb_ref, *, tm):
    j = pl.program_id(1)
    n_lane_tiles = x_ref.shape[1] // LANES

    @pl.when(j == 0)
    def _():
        rb = 256
        ssq = [jnp.sum(jnp.square(x_ref[r0:r0 + rb, :]), axis=-1, keepdims=True)
               for r0 in range(0, tm, rb)]
        rs = lax.rsqrt(jnp.concatenate(ssq, axis=0) * (1.0 / x_ref.shape[1]) + EPS)
        for c in range(n_lane_tiles):
            cs = slice(c * LANES, (c + 1) * LANES)
            hf = x_ref[:, cs] * rs * g_ref[:, cs]
            h_ref[0, :, cs] = hf.astype(BF16)
            hf_ref[...] = hf
            for slot, d in ((1, 4), (2, 16)):
                seg = tm // d
                for r in range(d):
                    h_ref[slot, r * seg:(r + 1) * seg, cs] = (
                        hf_ref[pl.ds(r, seg, stride=d), :].astype(BF16))

    slot = (j >= T_G1).astype(jnp.int32) + (j >= T_G2).astype(jnp.int32)
    q_scale = scale_ref[j]

    def step(w_src, rope):
        wb_ref[...] = w_src[...].astype(BF16)
        for r0 in range(0, tm, PROJ_ROWS):
            rows = pl.ds(r0, PROJ_ROWS)
            acc = jnp.dot(h_ref[slot, rows, :], wb_ref[...], preferred_element_type=F32)
            if rope:
                c, s = cos_ref[rows, :], sin_ref[rows, :]
            for t in range(WIDTH // LANES):
                a = acc[:, t * LANES:(t + 1) * LANES]
                if rope:
                    a = a * c + pltpu.roll(a, HEAD_DIM, 1) * s
                o_ref[rows, t * LANES:(t + 1) * LANES] = (a * q_scale).astype(BF16)

    is_rope = rope_ref[j] == 1
    pl.when(is_rope)(lambda: step(wr_ref, True))
    pl.when(jnp.logical_not(is_rope))(lambda: step(w_ref, False))


def _proj(x2d, gain, w, w_rope, cos_t, sin_t, *, batch, seq):
    rows, dm = x2d.shape
    tm = seq
    plain = [st[0] for st in STEPS if not st[2]]
    w_idx, r_idx, n_plain, n_rope = [], [], 0, 0
    for tile, _, rope, _ in STEPS:
        w_idx.append(plain[min(n_plain, len(plain) - 1)])
        r_idx.append(min(n_rope, N_ROPE - 1))
        n_plain += not rope
        n_rope += rope
    prefetch = (jnp.asarray(w_idx, jnp.int32), jnp.asarray(r_idx, jnp.int32),
                jnp.asarray([int(st[2]) for st in STEPS], jnp.int32),
                jnp.asarray([st[3] for st in STEPS], F32))

    def slot_of(j):
        return (j >= T_G1).astype(jnp.int32) + (j >= T_G2).astype(jnp.int32)

    tab_spec = pl.BlockSpec((None, tm, LANES), lambda i, j, *_: (slot_of(j), 0, 0))
    return pl.pallas_call(
        functools.partial(_proj_kernel, tm=tm),
        out_shape=jax.ShapeDtypeStruct((batch, N_STEPS, seq, WIDTH), BF16),
        grid_spec=pltpu.PrefetchScalarGridSpec(
            num_scalar_prefetch=len(prefetch),
            grid=(rows // tm, N_STEPS),
            in_specs=[
                pl.BlockSpec((tm, dm), lambda i, j, *_: (i, 0)),
                pl.BlockSpec((1, dm), lambda i, j, *_: (0, 0)),
                pl.BlockSpec((dm, WIDTH), lambda i, j, wt, rt, rp, sc: (0, wt[j])),
                pl.BlockSpec((dm, WIDTH), lambda i, j, wt, rt, rp, sc: (0, rt[j])),
                tab_spec, tab_spec,
            ],
            out_specs=pl.BlockSpec((None, None, tm, WIDTH), lambda i, j, *_: (i, j, 0, 0)),
            scratch_shapes=[pltpu.VMEM((3, tm, dm), BF16), pltpu.VMEM((tm, LANES), F32),
                            pltpu.VMEM((dm, WIDTH), BF16)]),
        compiler_params=pltpu.CompilerParams(
            dimension_semantics=("arbitrary", "arbitrary"),
            vmem_limit_bytes=VMEM_LIMIT),
        name="proj",
    )(*prefetch, x2d, gain, w, w_rope, cos_t, sin_t)


def _rope_tables(seq):
    pos = jnp.arange(seq, dtype=F32)
    inv = ROPE_THETA ** (-jnp.arange(ROPE_HALF, dtype=F32) * 2.0 / ROPE_DIM)
    ang = pos[:, None] * inv[None, :]
    cos, sin = jnp.cos(ang), jnp.sin(ang)
    rest = LANES // 2 - ROPE_DIM
    c = jnp.concatenate([cos, cos, jnp.ones((seq, rest), F32)] * 2, axis=-1)
    s = jnp.concatenate([-sin, -sin, jnp.zeros((seq, rest), F32),
                         sin, sin, jnp.zeros((seq, rest), F32)], axis=-1)

    def orders(t):
        out = [t]
        for d in SLOT_DIL[1:]:
            out.append(t.reshape(seq // d, d, LANES).transpose(1, 0, 2).reshape(seq, LANES))
        return jnp.stack(out)

    return orders(c), orders(s)


def _rope_weights(w):
    pair = _pair_layout_columns()
    cols = np.concatenate([tile * WIDTH + p * LANES + pair
                           for tile, _, rope, _ in STEPS if rope
                           for p in range(WIDTH // LANES)])
    pieces, start = [], 0
    for i in range(1, len(cols) + 1):
        if i == len(cols) or cols[i] != cols[i - 1] + 1:
            pieces.append(w[:, int(cols[start]):int(cols[i - 1]) + 1])
            start = i
    return jnp.concatenate(pieces, axis=1)


def _kv_kernel(x_ref, g_ref, w_ref, o_ref):
    xf = x_ref[...]
    ms = jnp.mean(xf * xf, axis=-1, keepdims=True)
    h = (xf * lax.rsqrt(ms + EPS) * g_ref[...]).astype(BF16)
    o_ref[...] = jnp.dot(h, w_ref[...].astype(BF16), preferred_element_type=F32).astype(BF16)


def _kv_proj(mem2d, gain, w, *, tm):
    rows, dm = mem2d.shape
    n_out = w.shape[1]
    return pl.pallas_call(
        _kv_kernel,
        out_shape=jax.ShapeDtypeStruct((rows, n_out), BF16),
        grid=(rows // tm,),
        in_specs=[pl.BlockSpec((tm, dm), lambda i: (i, 0)),
                  pl.BlockSpec((1, dm), lambda i: (0, 0)),
                  pl.BlockSpec((dm, n_out), lambda i: (0, 0))],
        out_specs=pl.BlockSpec((tm, n_out), lambda i: (i, 0)),
        compiler_params=pltpu.CompilerParams(
            dimension_semantics=("arbitrary",), vmem_limit_bytes=VMEM_LIMIT),
        name="kv_proj",
    )(mem2d, gain, w)


def _dil_kernel(q0, k0, v0, q1, k1, v1, q2, k2, v2, o_ref,
                og0, mg0, dg0, og1, mg1, dg1, bias_scr, s_scr, p_scr, *, seq, reach):
    left = lax.broadcasted_iota(jnp.int32, (1, LANES), 1) < HEAD_DIM
    head_a = _head_a_lanes()
    kw_max = 2 * QBLK

    @pl.when(jnp.logical_and(pl.program_id(0) == 0, pl.program_id(1) == 0))
    def _():
        rel = (lax.broadcasted_iota(jnp.int32, (2 * QBLK, kw_max), 0) % QBLK
               - lax.broadcasted_iota(jnp.int32, (2 * QBLK, kw_max), 1))
        for n in range(3):
            bias_scr[n] = jnp.where(jnp.abs(rel + n * reach) <= reach, 0.0, NEG)

    def pair_tile(col):
        return jnp.where(left, jnp.broadcast_to(col[:QBLK], (QBLK, LANES)),
                         jnp.broadcast_to(col[QBLK:], (QBLK, LANES)))

    def group(q_ref, k_ref, v_ref, dilation, finish):
        length = seq // dilation
        kw = min(kw_max, length)
        nqb = length // QBLK

        def chunk(cid, carry):
            where = []
            for t in range(CHUNK):
                bi = cid * CHUNK + t
                r = bi // nqb
                row0 = pl.multiple_of((bi % nqb) * QBLK, QBLK)
                ks = pl.multiple_of(jnp.clip(row0 - reach, 0, length - kw), reach)
                where.append((r, row0, ks))
                q2d = q_ref[r, pl.ds(row0, QBLK), :]
                zero = jnp.zeros_like(q2d)
                qq = jnp.concatenate([jnp.where(head_a, q2d, zero),
                                      jnp.where(head_a, zero, q2d)], axis=0)
                s_scr[t, :, :kw] = lax.dot_general(
                    qq, k_ref[r, pl.ds(ks, kw), :], (((1,), (1,)), ((), ())),
                    preferred_element_type=F32)
            for t, (r, row0, ks) in enumerate(where):
                s = s_scr[t, :, :kw] + bias_scr[(row0 - ks) // reach, :, :kw]
                m = jnp.max(s, axis=-1, keepdims=True)
                p = jnp.exp2(s - m)
                den = jnp.sum(p, axis=-1, keepdims=True)
                p_scr[t, :, :kw] = p.astype(BF16)
                pv = jnp.dot(p_scr[t, :, :kw], v_ref[r, pl.ds(ks, kw), :],
                             preferred_element_type=F32)
                finish(r, row0, jnp.where(left, pv[:QBLK], pv[QBLK:]), pair_tile(m), pair_tile(den))
            return carry

        lax.fori_loop(0, seq // (QBLK * CHUNK), chunk, 0)

    def keep(o_scr, m_scr, d_scr, length):
        def finish(r, row0, acc, m, den):
            rows = pl.ds(pl.multiple_of(r * length + row0, QBLK), QBLK)
            o_scr[rows, :] = acc
            m_scr[rows, :] = m
            d_scr[rows, :] = den
        return finish

    group(q0, k0, v0, 1, keep(og0, mg0, dg0, seq))
    group(q1, k1, v1, 4, keep(og1, mg1, dg1, seq // 4))

    def combine(r, row0, acc2, m2, den2):
        rows0 = pl.ds(r, QBLK, stride=16)
        rows1 = pl.ds((r % 4) * (seq // 4) + r // 4, QBLK, stride=4)
        m0, m1 = mg0[rows0, :], mg1[rows1, :]
        m = jnp.maximum(jnp.maximum(m0, m1), m2)
        w0, w1, w2 = jnp.exp2(m0 - m), jnp.exp2(m1 - m), jnp.exp2(m2 - m)
        num = w0 * og0[rows0, :] + w1 * og1[rows1, :] + w2 * acc2
        den = w0 * dg0[rows0, :] + w1 * dg1[rows1, :] + w2 * den2
        o_ref[r] = (num / den).astype(BF16)

    group(q2, k2, v2, 16, combine)


def _dil_attn(proj, *, reach):
    batch, _, seq, _ = proj.shape
    n_hp = WIDTH // LANES

    def spec(first_tile, d, kind):
        return pl.BlockSpec((None, None, d, seq // d, LANES),
                            lambda b, hp: (b, first_tile + kind, 0, 0, hp))

    views = {d: proj.reshape(batch, N_STEPS, d, seq // d, WIDTH) for d in SLOT_DIL}
    f32_rows = pltpu.VMEM((seq, LANES), F32)
    return pl.pallas_call(
        functools.partial(_dil_kernel, seq=seq, reach=reach),
        out_shape=jax.ShapeDtypeStruct((batch, 16, seq // 16, WIDTH), BF16),
        grid=(batch, n_hp),
        in_specs=[spec(t0, d, kind) for t0, d in zip((T_G0, T_G1, T_G2), SLOT_DIL)
                  for kind in range(3)],
        out_specs=pl.BlockSpec((None, 16, seq // 16, LANES), lambda b, hp: (b, 0, 0, hp)),
        scratch_shapes=[f32_rows] * 6 + [
            pltpu.VMEM((3, 2 * QBLK, 2 * QBLK), F32),
            pltpu.VMEM((CHUNK, 2 * QBLK, 2 * QBLK), F32),
            pltpu.VMEM((CHUNK, 2 * QBLK, 2 * QBLK), BF16)],
        compiler_params=pltpu.CompilerParams(
            dimension_semantics=("arbitrary", "arbitrary"),
            vmem_limit_bytes=VMEM_LIMIT),
        name="dil_attn",
    )(*[views[d] for d in SLOT_DIL for _ in range(3)])


def _na_kernel(rpb_ref, q_ref, k_ref, v_ref, o_ref, bias_scr, s_scr, p_scr, *, rows):
    kr = min(NA_ROWS, rows)
    nk = kr * GRID_W
    n_var = 2 * NA_ROWS - kr
    n_rel = 2 * NA_COLS - 1
    left = lax.broadcasted_iota(jnp.int32, (1, LANES), 1) < HEAD_DIM

    @pl.when(pl.program_id(1) == 0)
    def _():
        qc = lax.broadcasted_iota(jnp.int32, (GRID_W, LANES), 0)
        kc = lax.broadcasted_iota(jnp.int32, (GRID_W, LANES), 1) % GRID_W
        rel = jnp.clip(kc - qc, -(NA_COLS - 1), NA_COLS - 1) + NA_COLS - 1
        c_start = jnp.clip(qc - NA_COLS // 2, 0, GRID_W - NA_COLS)
        valid = jnp.logical_and(kc >= c_start, kc < c_start + NA_COLS)
        for hh in range(LANES // HEAD_DIM):
            head = pl.program_id(0) * (LANES // HEAD_DIM) + hh
            for a in range(2 * NA_ROWS - 1):
                base = (head * (2 * NA_ROWS - 1) + a) * n_rel
                blk = jnp.zeros((GRID_W, LANES), F32)
                for d in range(n_rel):
                    blk = jnp.where(rel == d, rpb_ref[base + d], blk)
                blk = jnp.where(valid, blk * LOG2E, NEG)
                for j in range(kr):
                    if 0 <= a - j < n_var:
                        half = slice((j % 2) * GRID_W, (j % 2 + 1) * GRID_W)
                        bias_scr[hh, a - j, :, j * GRID_W:(j + 1) * GRID_W] = blk[:, half]

    def chunk(cid, carry):
        where = []
        for t in range(CHUNK):
            r = cid * CHUNK + t
            r_start = jnp.clip(r - kr // 2, 0, rows - kr)
            row0 = pl.multiple_of(r * GRID_W, GRID_W)
            ks = pl.multiple_of(r_start * GRID_W, GRID_W)
            where.append((row0, ks, r_start - r + (NA_ROWS - 1)))
            q2d = q_ref[pl.ds(row0, GRID_W), :]
            zero = jnp.zeros_like(q2d)
            qq = jnp.concatenate([jnp.where(left, q2d, zero), jnp.where(left, zero, q2d)], axis=0)
            s_scr[t] = lax.dot_general(qq, k_ref[pl.ds(ks, nk), :], (((1,), (1,)), ((), ())),
                                       preferred_element_type=F32)
        for t, (row0, ks, variant) in enumerate(where):
            s = s_scr[t] + jnp.concatenate([bias_scr[0, variant], bias_scr[1, variant]], axis=0)
            m = jnp.max(s, axis=-1, keepdims=True)
            p = jnp.exp2(s - m)
            den = jnp.sum(p, axis=-1, keepdims=True)
            p_scr[t] = p.astype(BF16)
            pv = jnp.dot(p_scr[t], v_ref[pl.ds(ks, nk), :], preferred_element_type=F32)
            den_t = jnp.where(left, jnp.broadcast_to(den[:GRID_W], (GRID_W, LANES)),
                              jnp.broadcast_to(den[GRID_W:], (GRID_W, LANES)))
            o_ref[pl.ds(row0, GRID_W), :] = (
                jnp.where(left, pv[:GRID_W], pv[GRID_W:]) / den_t).astype(BF16)
        return carry

    lax.fori_loop(0, rows // CHUNK, chunk, 0)


def _na_attn(proj, rpb):
    batch, _, seq, _ = proj.shape
    rows = seq // GRID_W
    kr = min(NA_ROWS, rows)
    nk = kr * GRID_W
    n_hp = WIDTH // LANES
    assert rpb.shape == (WIDTH // HEAD_DIM, 2 * NA_ROWS - 1, 2 * NA_COLS - 1)

    def in_spec(kind):
        return pl.BlockSpec((None, None, seq, LANES), lambda hp, b, rp: (b, T_NA + kind, 0, hp))

    return pl.pallas_call(
        functools.partial(_na_kernel, rows=rows),
        out_shape=jax.ShapeDtypeStruct((batch, seq, WIDTH), BF16),
        grid_spec=pltpu.PrefetchScalarGridSpec(
            num_scalar_prefetch=1,
            grid=(n_hp, batch),
            in_specs=[in_spec(0), in_spec(1), in_spec(2)],
            out_specs=pl.BlockSpec((None, seq, LANES), lambda hp, b, rp: (b, 0, hp)),
            scratch_shapes=[pltpu.VMEM((LANES // HEAD_DIM, 2 * NA_ROWS - kr, GRID_W, nk), F32),
                            pltpu.VMEM((CHUNK, 2 * GRID_W, nk), F32),
                            pltpu.VMEM((CHUNK, 2 * GRID_W, nk), BF16)]),
        compiler_params=pltpu.CompilerParams(
            dimension_semantics=("arbitrary", "arbitrary"), vmem_limit_bytes=VMEM_LIMIT),
        name="na_attn",
    )(rpb.reshape(-1).astype(F32), proj, proj, proj)


def _mem_kernel(q_ref, kv_ref, o_ref, s_scr, p_scr, rden_scr, *, seq):
    mem_width = MEM_HEADS * MEM_HEAD_DIM
    rb = 512

    for h in range(MEM_HEADS):
        cs = slice(h * MEM_HEAD_DIM, (h + 1) * MEM_HEAD_DIM)
        vs = slice(mem_width + h * MEM_HEAD_DIM, mem_width + (h + 1) * MEM_HEAD_DIM)
        s_scr[...] = lax.dot_general(q_ref[:, cs], kv_ref[:, cs], (((1,), (1,)), ((), ())),
                                     preferred_element_type=F32)

        def softmax_rows(i, carry):
            for t in range(rb // QBLK):
                r0 = pl.multiple_of(i * rb + t * QBLK, QBLK)
                s = s_scr[pl.ds(r0, QBLK), :]
                m = jnp.max(s, axis=-1, keepdims=True)
                p = jnp.exp2(s - m)
                den = jnp.sum(p, axis=-1, keepdims=True)
                p_scr[pl.ds(r0, QBLK), :] = p.astype(BF16)
                rden_scr[pl.ds(r0, QBLK), :] = jnp.broadcast_to(1.0 / den, (QBLK, LANES))
            return carry

        lax.fori_loop(0, seq // rb, softmax_rows, 0)
        pv = jnp.dot(p_scr[...], kv_ref[:, vs], preferred_element_type=F32)
        o_ref[:, cs] = (pv * rden_scr[...]).astype(BF16)


def _mem_attn(proj, kv_m):
    batch, _, seq, _ = proj.shape
    mem_len, kv_w = kv_m.shape[1:]
    return pl.pallas_call(
        functools.partial(_mem_kernel, seq=seq),
        out_shape=jax.ShapeDtypeStruct((batch, seq, WIDTH), BF16),
        grid=(batch,),
        in_specs=[pl.BlockSpec((None, None, seq, WIDTH), lambda b: (b, T_MEMQ, 0, 0)),
                  pl.BlockSpec((None, mem_len, kv_w), lambda b: (b, 0, 0))],
        out_specs=pl.BlockSpec((None, seq, WIDTH), lambda b: (b, 0, 0)),
        scratch_shapes=[pltpu.VMEM((seq, mem_len), F32), pltpu.VMEM((seq, mem_len), BF16),
                        pltpu.VMEM((seq, LANES), F32)],
        compiler_params=pltpu.CompilerParams(
            dimension_semantics=("arbitrary",), vmem_limit_bytes=VMEM_LIMIT),
        name="mem_attn",
    )(proj, kv_m)


def _sigmoid(z):
    return 1.0 / (1.0 + jnp.exp(-z))


def _merge_kernel(oa, ga, za0, za1, ob, oc, gb, gc, zb0, zb1, zc0, zc1,
                  mb_ref, wa_f32, wb_f32, wc_f32, wo_f32, pn_ref, x_ref, out_ref,
                  y_scr, wa_ref, wb_ref, wc_ref, wo_ref, *, tm):
    @pl.when(pl.program_id(0) == 0)
    def _():
        for src, dst in ((wa_f32, wa_ref), (wb_f32, wb_ref), (wc_f32, wc_ref), (wo_f32, wo_ref)):
            dst[...] = src[...].astype(BF16)

    def branch(o, g, w_ref, z, idx):
        g = g.astype(F32)
        u = (o.astype(F32) * (g * _sigmoid(g))).astype(BF16)
        yb = jnp.dot(u, w_ref[...], preferred_element_type=F32)
        return _sigmoid(z.astype(F32) + mb_ref[idx:idx + 1, :]) * yb

    def logits(z_lo, z_hi, rows):
        return jnp.concatenate([z_lo[...].reshape(rows, WIDTH), z_hi[...].reshape(rows, WIDTH)],
                               axis=-1)

    seg = tm // 16
    ya = branch(oa[...].reshape(tm, WIDTH), ga[...].reshape(tm, WIDTH), wa_ref,
                logits(za0, za1, tm), 0)
    n_lane_tiles = D_MODEL // LANES
    for r in range(16):
        for c in range(n_lane_tiles):
            y_scr[c, pl.ds(r, seg, stride=16), :] = ya[r * seg:(r + 1) * seg,
                                                       c * LANES:(c + 1) * LANES]
    ya_nat = jnp.concatenate([y_scr[c] for c in range(n_lane_tiles)], axis=-1)

    y = (ya_nat + branch(ob[...], gb[...], wb_ref, logits(zb0, zb1, tm), 1)
         + branch(oc[...], gc[...], wc_ref, logits(zc0, zc1, tm), 2))
    z = jnp.dot(y.astype(BF16), wo_ref[...], preferred_element_type=F32)
    ms = jnp.mean(z * z, axis=-1, keepdims=True)
    out_ref[...] = x_ref[...] + z * lax.rsqrt(ms + EPS) * pn_ref[...]


def _merge(x2d, proj, out_a16, out_b, out_c, merge_bias, wa, wb, wc, wo, post_norm, *, tm):
    rows, dm = x2d.shape
    batch, _, seq, _ = proj.shape
    per_b = seq // tm
    seg = tm // 16
    proj16 = proj.reshape(batch, N_STEPS, 16, seq // 16, WIDTH)

    def p16_spec(tile):
        return pl.BlockSpec((None, None, 16, seg, WIDTH),
                            lambda i: (i // per_b, tile, 0, i % per_b, 0))

    def nat_spec(tile):
        return pl.BlockSpec((None, None, tm, WIDTH), lambda i: (i // per_b, tile, i % per_b, 0))

    def rows_spec(width):
        return pl.BlockSpec((tm, width), lambda i: (i, 0))

    def full(shape):
        return pl.BlockSpec(shape, lambda i: (0, 0), pipeline_mode=pl.Buffered(1))

    return pl.pallas_call(
        functools.partial(_merge_kernel, tm=tm),
        out_shape=jax.ShapeDtypeStruct((rows, dm), F32),
        grid=(rows // tm,),
        in_specs=[pl.BlockSpec((None, 16, seg, WIDTH), lambda i: (i // per_b, 0, i % per_b, 0)),
                  p16_spec(T_GATE_A), p16_spec(T_MERGE_A), p16_spec(T_MERGE_A + 1),
                  rows_spec(WIDTH), rows_spec(WIDTH),
                  nat_spec(T_GATE_B), nat_spec(T_GATE_B + 1),
                  nat_spec(T_MERGE_B), nat_spec(T_MERGE_B + 1),
                  nat_spec(T_MERGE_B + 2), nat_spec(T_MERGE_B + 3),
                  full(merge_bias.shape), full(wa.shape), full(wb.shape), full(wc.shape),
                  full(wo.shape), full(post_norm.shape), rows_spec(dm)],
        out_specs=pl.BlockSpec((tm, dm), lambda i: (i, 0)),
        scratch_shapes=[pltpu.VMEM((dm // LANES, tm, LANES), F32)]
        + [pltpu.VMEM(w.shape, BF16) for w in (wa, wb, wc, wo)],
        compiler_params=pltpu.CompilerParams(
            dimension_semantics=("arbitrary",), vmem_limit_bytes=VMEM_LIMIT),
        name="merge",
    )(out_a16, proj16, proj16, proj16, out_b, out_c, proj, proj, proj, proj, proj, proj,
      merge_bias, wa, wb, wc, wo, post_norm, x2d)


def kernel(x, mem, pre_norm, w_in, merge_bias, na_rpb, mem_norm, w_mem_kv,
           w_branch_a, w_branch_b, w_branch_c, w_out, post_norm):
    b, s, dm = x.shape
    depth = pre_norm.shape[0]
    reach = (DIL_CONFIGS[0][0] // 2) // DIL_CONFIGS[0][1]
    assert all((w // 2) // d == reach for w, d in DIL_CONFIGS) and 2 * reach == QBLK
    assert tuple(d for _, d in DIL_CONFIGS) == SLOT_DIL
    cos_t, sin_t = _rope_tables(s)
    for layer in range(depth):
        x2d = x.reshape(b * s, dm)
        proj = _proj(x2d, pre_norm[layer][None], w_in[layer], _rope_weights(w_in[layer]),
                     cos_t, sin_t, batch=b, seq=s)

        out_a16 = _dil_attn(proj, reach=reach)
        out_b = _na_attn(proj, na_rpb[layer])

        mem2d = mem.reshape(b * mem.shape[1], dm)
        kv_m = _kv_proj(mem2d, mem_norm[layer][None], w_mem_kv[layer], tm=1024)
        out_c = _mem_attn(proj, kv_m.reshape(b, mem.shape[1], -1))

        y = _merge(x2d, proj, out_a16, out_b.reshape(b * s, WIDTH), out_c.reshape(b * s, WIDTH),
                   merge_bias[layer], w_branch_a[layer], w_branch_b[layer], w_branch_c[layer],
                   w_out[layer], post_norm[layer][None], tm=256)
        x = y.reshape(b, s, dm)
    return x
```

```python
import functools
import math

import jax
import jax.numpy as jnp
import numpy as np
from jax import lax
from jax.experimental import pallas as pl
from jax.experimental.pallas import tpu as pltpu

D_MODEL = 1024
HEAD_DIM = 64
DIL_CONFIGS = ((128, 1), (512, 4), (2048, 16))
WIDTH = 512
NA_ROWS = 8
NA_COLS = 16
GRID_W = 64
MEM_HEADS = 4
MEM_HEAD_DIM = 128
ROPE_THETA = 500000.0
ROPE_DIM = HEAD_DIM // 4
ROPE_HALF = ROPE_DIM // 2
EPS = 1e-6
NEG = -1e30
LOG2E = math.log2(math.e)

LANES = 128
QBLK = 128
CHUNK = 4
PROJ_ROWS = 512
VMEM_LIMIT = 56 * 1024 * 1024

_QS = HEAD_DIM ** -0.5 * LOG2E
_MS = MEM_HEAD_DIM ** -0.5 * LOG2E
STEPS = (
    (0, 0, True, _QS), (1, 0, True, 1.0), (2, 0, False, 1.0),
    (9, 0, False, _QS), (10, 0, False, 1.0), (11, 0, False, 1.0),
    (12, 0, False, _MS),
    (14, 0, False, 1.0), (15, 0, False, 1.0),
    (18, 0, False, 1.0), (19, 0, False, 1.0), (20, 0, False, 1.0), (21, 0, False, 1.0),
    (3, 1, True, _QS), (4, 1, True, 1.0), (5, 1, False, 1.0),
    (6, 2, True, _QS), (7, 2, True, 1.0), (8, 2, False, 1.0),
    (13, 2, False, 1.0), (16, 2, False, 1.0), (17, 2, False, 1.0),
)
N_STEPS = len(STEPS)
T_G0, T_NA, T_MEMQ, T_GATE_B, T_MERGE_B = 0, 3, 6, 7, 9
T_G1, T_G2, T_GATE_A, T_MERGE_A = 13, 16, 19, 20
SLOT_DIL = (1, 4, 16)

BF16 = jnp.bfloat16
F32 = jnp.float32


def _proj_kernel(wtile_ref, rope_ref, scale_ref, x_ref, g_ref, w_ref,
                 cos_ref, sin_ref, o_ref, h_ref, hf_ref, wb_ref, *, tm):
    j = pl.program_id(1)
    n_lane_tiles = x_ref.shape[1] // LANES

    @pl.when(j == 0)
    def _():
        rb = 256
        ssq = [jnp.sum(jnp.square(x_ref[r0:r0 + rb, :]), axis=-1, keepdims=True)
               for r0 in range(0, tm, rb)]
        rs = lax.rsqrt(jnp.concatenate(ssq, axis=0) * (1.0 / x_ref.shape[1]) + EPS)
        for c in range(n_lane_tiles):
            cs = slice(c * LANES, (c + 1) * LANES)
            hf = x_ref[:, cs] * rs * g_ref[:, cs]
            h_ref[0, :, cs] = hf.astype(BF16)
            hf_ref[...] = hf
            for slot, d in ((1, 4), (2, 16)):
                seg = tm // d
                for r in range(d):
                    h_ref[slot, r * seg:(r + 1) * seg, cs] = (
                        hf_ref[pl.ds(r, seg, stride=d), :].astype(BF16))

    slot = (j >= T_G1).astype(jnp.int32) + (j >= T_G2).astype(jnp.int32)
    q_scale = scale_ref[j]

    wb_ref[...] = w_ref[...].astype(BF16)

    def step(rope):
        lane = lax.broadcasted_iota(jnp.int32, (1, LANES), 1)
        first_half = lane % HEAD_DIM < ROPE_HALF
        for r0 in range(0, tm, PROJ_ROWS):
            rows = pl.ds(r0, PROJ_ROWS)
            acc = jnp.dot(h_ref[slot, rows, :], wb_ref[...], preferred_element_type=F32)
            if rope:
                c, s = cos_ref[rows, :], sin_ref[rows, :]
            for t in range(WIDTH // LANES):
                a = acc[:, t * LANES:(t + 1) * LANES]
                if rope:
                    partner = jnp.where(first_half, pltpu.roll(a, LANES - ROPE_HALF, 1),
                                        pltpu.roll(a, ROPE_HALF, 1))
                    a = a * c + partner * s
                o_ref[rows, t * LANES:(t + 1) * LANES] = (a * q_scale).astype(BF16)

    is_rope = rope_ref[j] == 1
    pl.when(is_rope)(lambda: step(True))
    pl.when(jnp.logical_not(is_rope))(lambda: step(False))


def _proj(x2d, gain, w, cos_t, sin_t, *, batch, seq):
    rows, dm = x2d.shape
    tm = seq
    prefetch = (jnp.asarray([st[0] for st in STEPS], jnp.int32),
                jnp.asarray([int(st[2]) for st in STEPS], jnp.int32),
                jnp.asarray([st[3] for st in STEPS], F32))

    def slot_of(j):
        return (j >= T_G1).astype(jnp.int32) + (j >= T_G2).astype(jnp.int32)

    tab_spec = pl.BlockSpec((None, tm, LANES), lambda i, j, *_: (slot_of(j), 0, 0))
    return pl.pallas_call(
        functools.partial(_proj_kernel, tm=tm),
        out_shape=jax.ShapeDtypeStruct((batch, N_STEPS, seq, WIDTH), BF16),
        grid_spec=pltpu.PrefetchScalarGridSpec(
            num_scalar_prefetch=len(prefetch),
            grid=(rows // tm, N_STEPS),
            in_specs=[
                pl.BlockSpec((tm, dm), lambda i, j, *_: (i, 0)),
                pl.BlockSpec((1, dm), lambda i, j, *_: (0, 0)),
                pl.BlockSpec((dm, WIDTH), lambda i, j, wt, rp, sc: (0, wt[j])),
                tab_spec, tab_spec,
            ],
            out_specs=pl.BlockSpec((None, None, tm, WIDTH), lambda i, j, *_: (i, j, 0, 0)),
            scratch_shapes=[pltpu.VMEM((3, tm, dm), BF16), pltpu.VMEM((tm, LANES), F32),
                            pltpu.VMEM((dm, WIDTH), BF16)]),
        compiler_params=pltpu.CompilerParams(
            dimension_semantics=("arbitrary", "arbitrary"),
            vmem_limit_bytes=VMEM_LIMIT),
        name="proj",
    )(*prefetch, x2d, gain, w, cos_t, sin_t)


def _rope_tables(seq):
    pos = jnp.arange(seq, dtype=F32)
    inv = ROPE_THETA ** (-jnp.arange(ROPE_HALF, dtype=F32) * 2.0 / ROPE_DIM)
    ang = pos[:, None] * inv[None, :]
    cos, sin = jnp.cos(ang), jnp.sin(ang)
    rest = HEAD_DIM - ROPE_DIM
    c = jnp.concatenate([cos, cos, jnp.ones((seq, rest), F32)] * (LANES // HEAD_DIM), axis=-1)
    s = jnp.concatenate([-sin, sin, jnp.zeros((seq, rest), F32)] * (LANES // HEAD_DIM), axis=-1)

    def orders(t):
        out = [t]
        for d in SLOT_DIL[1:]:
            out.append(t.reshape(seq // d, d, LANES).transpose(1, 0, 2).reshape(seq, LANES))
        return jnp.stack(out)

    return orders(c), orders(s)


def _kv_kernel(x_ref, g_ref, w_ref, o_ref):
    xf = x_ref[...]
    ms = jnp.mean(xf * xf, axis=-1, keepdims=True)
    h = (xf * lax.rsqrt(ms + EPS) * g_ref[...]).astype(BF16)
    o_ref[...] = jnp.dot(h, w_ref[...].astype(BF16), preferred_element_type=F32).astype(BF16)


def _kv_proj(mem2d, gain, w, *, tm):
    rows, dm = mem2d.shape
    n_out = w.shape[1]
    return pl.pallas_call(
        _kv_kernel,
        out_shape=jax.ShapeDtypeStruct((rows, n_out), BF16),
        grid=(rows // tm,),
        in_specs=[pl.BlockSpec((tm, dm), lambda i: (i, 0)),
                  pl.BlockSpec((1, dm), lambda i: (0, 0)),
                  pl.BlockSpec((dm, n_out), lambda i: (0, 0))],
        out_specs=pl.BlockSpec((tm, n_out), lambda i: (i, 0)),
        compiler_params=pltpu.CompilerParams(
            dimension_semantics=("arbitrary",), vmem_limit_bytes=VMEM_LIMIT),
        name="kv_proj",
    )(mem2d, gain, w)


def _dil_kernel(q0, k0, v0, q1, k1, v1, q2, k2, v2, o_ref,
                og0, mg0, dg0, og1, mg1, dg1, bias_scr, s_scr, p_scr, *, seq, reach):
    left = lax.broadcasted_iota(jnp.int32, (1, LANES), 1) < HEAD_DIM
    kw_max = 2 * QBLK

    @pl.when(jnp.logical_and(pl.program_id(0) == 0, pl.program_id(1) == 0))
    def _():
        rel = (lax.broadcasted_iota(jnp.int32, (2 * QBLK, kw_max), 0) % QBLK
               - lax.broadcasted_iota(jnp.int32, (2 * QBLK, kw_max), 1))
        for n in range(3):
            bias_scr[n] = jnp.where(jnp.abs(rel + n * reach) <= reach, 0.0, NEG)

    def pair_tile(col):
        return jnp.where(left, jnp.broadcast_to(col[:QBLK], (QBLK, LANES)),
                         jnp.broadcast_to(col[QBLK:], (QBLK, LANES)))

    def keep(o_scr, m_scr, d_scr, length):
        def finish(r, row0, acc, m, den):
            rows = pl.ds(r * length + row0, QBLK)
            o_scr[rows, :] = acc
            m_scr[rows, :] = m
            d_scr[rows, :] = den
        return finish

    def combine(r, row0, acc2, m2, den2):
        rows0 = pl.ds(r, QBLK, stride=16)
        rows1 = pl.ds((r % 4) * (seq // 4) + r // 4, QBLK, stride=4)
        m0, m1 = mg0[rows0, :], mg1[rows1, :]
        m = jnp.maximum(jnp.maximum(m0, m1), m2)
        w0, w1, w2 = jnp.exp2(m0 - m), jnp.exp2(m1 - m), jnp.exp2(m2 - m)
        num = w0 * og0[rows0, :] + w1 * og1[rows1, :] + w2 * acc2
        den = w0 * dg0[rows0, :] + w1 * dg1[rows1, :] + w2 * den2
        o_ref[r] = (num / den).astype(BF16)

    tiles = []
    for (q_ref, k_ref, v_ref), d, finish in (
            ((q0, k0, v0), 1, keep(og0, mg0, dg0, seq)),
            ((q1, k1, v1), 4, keep(og1, mg1, dg1, seq // 4)),
            ((q2, k2, v2), 16, combine)):
        length = seq // d
        kw = min(kw_max, length)
        for bi in range(seq // QBLK):
            r, row0 = divmod(bi * QBLK, length)
            ks = min(max(row0 - reach, 0), length - kw)
            tiles.append((q_ref, k_ref, v_ref, r, row0, ks, kw, finish))
    chunks = [tiles[i:i + CHUNK] for i in range(0, len(tiles), CHUNK)]

    def scores(c):
        for t, (q_ref, k_ref, _, r, row0, ks, kw, _) in enumerate(chunks[c]):
            q2d = q_ref[r, row0:row0 + QBLK, :]
            zero = jnp.zeros_like(q2d)
            qq = jnp.concatenate([jnp.where(left, q2d, zero), jnp.where(left, zero, q2d)], axis=0)
            s_scr[c % 2, t, :, :kw] = lax.dot_general(
                qq, k_ref[r, ks:ks + kw, :], (((1,), (1,)), ((), ())),
                preferred_element_type=F32)

    def softmax_pv(c):
        for t, (_, _, v_ref, r, row0, ks, kw, finish) in enumerate(chunks[c]):
            s = s_scr[c % 2, t, :, :kw] + bias_scr[(row0 - ks) // reach, :, :kw]
            m = jnp.max(s, axis=-1, keepdims=True)
            p = jnp.exp2(s - m)
            den = jnp.sum(p, axis=-1, keepdims=True)
            p_scr[t, :, :kw] = p.astype(BF16)
            pv = jnp.dot(p_scr[t, :, :kw], v_ref[r, ks:ks + kw, :], preferred_element_type=F32)
            finish(r, row0, jnp.where(left, pv[:QBLK], pv[QBLK:]), pair_tile(m), pair_tile(den))

    scores(0)
    for c in range(len(chunks)):
        if c + 1 < len(chunks):
            scores(c + 1)
        softmax_pv(c)


def _dil_attn(proj, *, reach):
    batch, _, seq, _ = proj.shape
    n_hp = WIDTH // LANES

    def spec(first_tile, d, kind):
        return pl.BlockSpec((None, None, d, seq // d, LANES),
                            lambda b, hp: (b, first_tile + kind, 0, 0, hp))

    views = {d: proj.reshape(batch, N_STEPS, d, seq // d, WIDTH) for d in SLOT_DIL}
    f32_rows = pltpu.VMEM((seq, LANES), F32)
    return pl.pallas_call(
        functools.partial(_dil_kernel, seq=seq, reach=reach),
        out_shape=jax.ShapeDtypeStruct((batch, 16, seq // 16, WIDTH), BF16),
        grid=(batch, n_hp),
        in_specs=[spec(t0, d, kind) for t0, d in zip((T_G0, T_G1, T_G2), SLOT_DIL)
                  for kind in range(3)],
        out_specs=pl.BlockSpec((None, 16, seq // 16, LANES), lambda b, hp: (b, 0, 0, hp)),
        scratch_shapes=[f32_rows] * 6 + [
            pltpu.VMEM((3, 2 * QBLK, 2 * QBLK), F32),
            pltpu.VMEM((2, CHUNK, 2 * QBLK, 2 * QBLK), F32),
            pltpu.VMEM((CHUNK, 2 * QBLK, 2 * QBLK), BF16)],
        compiler_params=pltpu.CompilerParams(
            dimension_semantics=("arbitrary", "arbitrary"),
            vmem_limit_bytes=VMEM_LIMIT),
        name="dil_attn",
    )(*[views[d] for d in SLOT_DIL for _ in range(3)])


def _na_kernel(rpb_ref, q_ref, k_ref, v_ref, o_ref, bias_scr, s_scr, p_scr, *, rows):
    kr = min(NA_ROWS, rows)
    nk = kr * GRID_W
    n_var = 2 * NA_ROWS - kr
    n_rel = 2 * NA_COLS - 1
    left = lax.broadcasted_iota(jnp.int32, (1, LANES), 1) < HEAD_DIM

    @pl.when(pl.program_id(1) == 0)
    def _():
        qc = lax.broadcasted_iota(jnp.int32, (GRID_W, LANES), 0)
        kc = lax.broadcasted_iota(jnp.int32, (GRID_W, LANES), 1) % GRID_W
        rel = jnp.clip(kc - qc, -(NA_COLS - 1), NA_COLS - 1) + NA_COLS - 1
        c_start = jnp.clip(qc - NA_COLS // 2, 0, GRID_W - NA_COLS)
        valid = jnp.logical_and(kc >= c_start, kc < c_start + NA_COLS)
        for hh in range(LANES // HEAD_DIM):
            head = pl.program_id(0) * (LANES // HEAD_DIM) + hh
            for a in range(2 * NA_ROWS - 1):
                base = (head * (2 * NA_ROWS - 1) + a) * n_rel
                blk = jnp.zeros((GRID_W, LANES), F32)
                for d in range(n_rel):
                    blk = jnp.where(rel == d, rpb_ref[base + d], blk)
                blk = jnp.where(valid, blk * LOG2E, NEG)
                for j in range(kr):
                    if 0 <= a - j < n_var:
                        half = slice((j % 2) * GRID_W, (j % 2 + 1) * GRID_W)
                        bias_scr[hh, a - j, :, j * GRID_W:(j + 1) * GRID_W] = blk[:, half]

    def window(r):
        r_start = min(max(r - kr // 2, 0), rows - kr)
        return r * GRID_W, r_start * GRID_W, r_start - r + (NA_ROWS - 1)

    def scores(c):
        for t in range(CHUNK):
            row0, ks, _ = window(c * CHUNK + t)
            q2d = q_ref[row0:row0 + GRID_W, :]
            zero = jnp.zeros_like(q2d)
            qq = jnp.concatenate([jnp.where(left, q2d, zero), jnp.where(left, zero, q2d)], axis=0)
            s_scr[c % 2, t] = lax.dot_general(qq, k_ref[ks:ks + nk, :], (((1,), (1,)), ((), ())),
                                              preferred_element_type=F32)

    def softmax_pv(c):
        for t in range(CHUNK):
            row0, ks, variant = window(c * CHUNK + t)
            s = s_scr[c % 2, t] + jnp.concatenate([bias_scr[0, variant], bias_scr[1, variant]],
                                                  axis=0)
            m = jnp.max(s, axis=-1, keepdims=True)
            p = jnp.exp2(s - m)
            den = jnp.sum(p, axis=-1, keepdims=True)
            p_scr[t] = p.astype(BF16)
            pv = jnp.dot(p_scr[t], v_ref[ks:ks + nk, :], preferred_element_type=F32)
            den_t = jnp.where(left, jnp.broadcast_to(den[:GRID_W], (GRID_W, LANES)),
                              jnp.broadcast_to(den[GRID_W:], (GRID_W, LANES)))
            o_ref[row0:row0 + GRID_W, :] = (
                jnp.where(left, pv[:GRID_W], pv[GRID_W:]) / den_t).astype(BF16)

    n_chunks = rows // CHUNK
    scores(0)
    for c in range(n_chunks):
        if c + 1 < n_chunks:
            scores(c + 1)
        softmax_pv(c)


def _na_attn(proj, rpb):
    batch, _, seq, _ = proj.shape
    rows = seq // GRID_W
    kr = min(NA_ROWS, rows)
    nk = kr * GRID_W
    n_hp = WIDTH // LANES
    assert rpb.shape == (WIDTH // HEAD_DIM, 2 * NA_ROWS - 1, 2 * NA_COLS - 1)

    def in_spec(kind):
        return pl.BlockSpec((None, None, seq, LANES), lambda hp, b, rp: (b, T_NA + kind, 0, hp))

    return pl.pallas_call(
        functools.partial(_na_kernel, rows=rows),
        out_shape=jax.ShapeDtypeStruct((batch, seq, WIDTH), BF16),
        grid_spec=pltpu.PrefetchScalarGridSpec(
            num_scalar_prefetch=1,
            grid=(n_hp, batch),
            in_specs=[in_spec(0), in_spec(1), in_spec(2)],
            out_specs=pl.BlockSpec((None, seq, LANES), lambda hp, b, rp: (b, 0, hp)),
            scratch_shapes=[pltpu.VMEM((LANES // HEAD_DIM, 2 * NA_ROWS - kr, GRID_W, nk), F32),
                            pltpu.VMEM((2, CHUNK, 2 * GRID_W, nk), F32),
                            pltpu.VMEM((CHUNK, 2 * GRID_W, nk), BF16)]),
        compiler_params=pltpu.CompilerParams(
            dimension_semantics=("arbitrary", "arbitrary"), vmem_limit_bytes=VMEM_LIMIT),
        name="na_attn",
    )(rpb.reshape(-1).astype(F32), proj, proj, proj)


def _mem_kernel(q_ref, kv_ref, o_ref, s_scr, p_scr, rden_scr, *, seq):
    mem_width = MEM_HEADS * MEM_HEAD_DIM
    rb = 512

    for h in range(MEM_HEADS):
        cs = slice(h * MEM_HEAD_DIM, (h + 1) * MEM_HEAD_DIM)
        vs = slice(mem_width + h * MEM_HEAD_DIM, mem_width + (h + 1) * MEM_HEAD_DIM)
        s_scr[...] = lax.dot_general(q_ref[:, cs], kv_ref[:, cs], (((1,), (1,)), ((), ())),
                                     preferred_element_type=F32)

        def softmax_rows(i, carry):
            for t in range(rb // QBLK):
                r0 = pl.multiple_of(i * rb + t * QBLK, QBLK)
                s = s_scr[pl.ds(r0, QBLK), :]
                m = jnp.max(s, axis=-1, keepdims=True)
                p = jnp.exp2(s - m)
                den = jnp.sum(p, axis=-1, keepdims=True)
                p_scr[pl.ds(r0, QBLK), :] = p.astype(BF16)
                rden_scr[pl.ds(r0, QBLK), :] = jnp.broadcast_to(1.0 / den, (QBLK, LANES))
            return carry

        lax.fori_loop(0, seq // rb, softmax_rows, 0)
        pv = jnp.dot(p_scr[...], kv_ref[:, vs], preferred_element_type=F32)
        o_ref[:, cs] = (pv * rden_scr[...]).astype(BF16)


def _mem_attn(proj, kv_m):
    batch, _, seq, _ = proj.shape
    mem_len, kv_w = kv_m.shape[1:]
    return pl.pallas_call(
        functools.partial(_mem_kernel, seq=seq),
        out_shape=jax.ShapeDtypeStruct((batch, seq, WIDTH), BF16),
        grid=(batch,),
        in_specs=[pl.BlockSpec((None, None, seq, WIDTH), lambda b: (b, T_MEMQ, 0, 0)),
                  pl.BlockSpec((None, mem_len, kv_w), lambda b: (b, 0, 0))],
        out_specs=pl.BlockSpec((None, seq, WIDTH), lambda b: (b, 0, 0)),
        scratch_shapes=[pltpu.VMEM((seq, mem_len), F32), pltpu.VMEM((seq, mem_len), BF16),
                        pltpu.VMEM((seq, LANES), F32)],
        compiler_params=pltpu.CompilerParams(
            dimension_semantics=("arbitrary",), vmem_limit_bytes=VMEM_LIMIT),
        name="mem_attn",
    )(proj, kv_m)


def _sigmoid(z):
    return 1.0 / (1.0 + jnp.exp(-z))


def _merge_kernel(oa, ga, za0, za1, ob, oc, gb, gc, zb0, zb1, zc0, zc1,
                  mb_ref, wa_f32, wb_f32, wc_f32, wo_f32, pn_ref, x_ref, out_ref,
                  y_scr, wa_ref, wb_ref, wc_ref, wo_ref, *, tm):
    @pl.when(pl.program_id(0) == 0)
    def _():
        for src, dst in ((wa_f32, wa_ref), (wb_f32, wb_ref), (wc_f32, wc_ref), (wo_f32, wo_ref)):
            dst[...] = src[...].astype(BF16)

    def branch(o, g, w_ref, z, idx):
        g = g.astype(F32)
        u = (o.astype(F32) * (g * _sigmoid(g))).astype(BF16)
        yb = jnp.dot(u, w_ref[...], preferred_element_type=F32)
        return _sigmoid(z.astype(F32) + mb_ref[idx:idx + 1, :]) * yb

    def logits(z_lo, z_hi, rows):
        return jnp.concatenate([z_lo[...].reshape(rows, WIDTH), z_hi[...].reshape(rows, WIDTH)],
                               axis=-1)

    seg = tm // 16
    ya = branch(oa[...].reshape(tm, WIDTH), ga[...].reshape(tm, WIDTH), wa_ref,
                logits(za0, za1, tm), 0)
    n_lane_tiles = D_MODEL // LANES
    for r in range(16):
        for c in range(n_lane_tiles):
            y_scr[c, pl.ds(r, seg, stride=16), :] = ya[r * seg:(r + 1) * seg,
                                                       c * LANES:(c + 1) * LANES]
    ya_nat = jnp.concatenate([y_scr[c] for c in range(n_lane_tiles)], axis=-1)

    y = (ya_nat + branch(ob[...], gb[...], wb_ref, logits(zb0, zb1, tm), 1)
         + branch(oc[...], gc[...], wc_ref, logits(zc0, zc1, tm), 2))
    z = jnp.dot(y.astype(BF16), wo_ref[...], preferred_element_type=F32)
    ms = jnp.mean(z * z, axis=-1, keepdims=True)
    out_ref[...] = x_ref[...] + z * lax.rsqrt(ms + EPS) * pn_ref[...]


def _merge(x2d, proj, out_a16, out_b, out_c, merge_bias, wa, wb, wc, wo, post_norm, *, tm):
    rows, dm = x2d.shape
    batch, _, seq, _ = proj.shape
    per_b = seq // tm
    seg = tm // 16
    proj16 = proj.reshape(batch, N_STEPS, 16, seq // 16, WIDTH)

    def p16_spec(tile):
        return pl.BlockSpec((None, None, 16, seg, WIDTH),
                            lambda i: (i // per_b, tile, 0, i % per_b, 0))

    def nat_spec(tile):
        return pl.BlockSpec((None, None, tm, WIDTH), lambda i: (i // per_b, tile, i % per_b, 0))

    def rows_spec(width):
        return pl.BlockSpec((tm, width), lambda i: (i, 0))

    def full(shape):
        return pl.BlockSpec(shape, lambda i: (0, 0), pipeline_mode=pl.Buffered(1))

    return pl.pallas_call(
        functools.partial(_merge_kernel, tm=tm),
        out_shape=jax.ShapeDtypeStruct((rows, dm), F32),
        grid=(rows // tm,),
        in_specs=[pl.BlockSpec((None, 16, seg, WIDTH), lambda i: (i // per_b, 0, i % per_b, 0)),
                  p16_spec(T_GATE_A), p16_spec(T_MERGE_A), p16_spec(T_MERGE_A + 1),
                  rows_spec(WIDTH), rows_spec(WIDTH),
                  nat_spec(T_GATE_B), nat_spec(T_GATE_B + 1),
                  nat_spec(T_MERGE_B), nat_spec(T_MERGE_B + 1),
                  nat_spec(T_MERGE_B + 2), nat_spec(T_MERGE_B + 3),
                  full(merge_bias.shape), full(wa.shape), full(wb.shape), full(wc.shape),
                  full(wo.shape), full(post_norm.shape), rows_spec(dm)],
        out_specs=pl.BlockSpec((tm, dm), lambda i: (i, 0)),
        scratch_shapes=[pltpu.VMEM((dm // LANES, tm, LANES), F32)]
        + [pltpu.VMEM(w.shape, BF16) for w in (wa, wb, wc, wo)],
        compiler_params=pltpu.CompilerParams(
            dimension_semantics=("arbitrary",), vmem_limit_bytes=VMEM_LIMIT),
        name="merge",
    )(out_a16, proj16, proj16, proj16, out_b, out_c, proj, proj, proj, proj, proj, proj,
      merge_bias, wa, wb, wc, wo, post_norm, x2d)


def kernel(x, mem, pre_norm, w_in, merge_bias, na_rpb, mem_norm, w_mem_kv,
           w_branch_a, w_branch_b, w_branch_c, w_out, post_norm):
    b, s, dm = x.shape
    depth = pre_norm.shape[0]
    reach = (DIL_CONFIGS[0][0] // 2) // DIL_CONFIGS[0][1]
    assert all((w // 2) // d == reach for w, d in DIL_CONFIGS) and 2 * reach == QBLK
    assert tuple(d for _, d in DIL_CONFIGS) == SLOT_DIL
    cos_t, sin_t = _rope_tables(s)
    for layer in range(depth):
        x2d = x.reshape(b * s, dm)
        proj = _proj(x2d, pre_norm[layer][None], w_in[layer], cos_t, sin_t, batch=b, seq=s)

        out_a16 = _dil_attn(proj, reach=reach)
        out_b = _na_attn(proj, na_rpb[layer])

        mem2d = mem.reshape(b * mem.shape[1], dm)
        kv_m = _kv_proj(mem2d, mem_norm[layer][None], w_mem_kv[layer], tm=1024)
        out_c = _mem_attn(proj, kv_m.reshape(b, mem.shape[1], -1))

        y = _merge(x2d, proj, out_a16, out_b.reshape(b * s, WIDTH), out_c.reshape(b * s, WIDTH),
                   merge_bias[layer], w_branch_a[layer], w_branch_b[layer], w_branch_c[layer],
                   w_out[layer], post_norm[layer][None], tm=256)
        x = y.reshape(b, s, dm)
    return x
```

```python
import functools
import math

import jax
import jax.numpy as jnp
import numpy as np
from jax import lax
from jax.experimental import pallas as pl
from jax.experimental.pallas import tpu as pltpu

D_MODEL = 1024
HEAD_DIM = 64
DIL_CONFIGS = ((128, 1), (512, 4), (2048, 16))
WIDTH = 512
NA_ROWS = 8
NA_COLS = 16
GRID_W = 64
MEM_HEADS = 4
MEM_HEAD_DIM = 128
ROPE_THETA = 500000.0
ROPE_DIM = HEAD_DIM // 4
ROPE_HALF = ROPE_DIM // 2
EPS = 1e-6
NEG = -1e30
LOG2E = math.log2(math.e)

LANES = 128
QBLK = 128
CHUNK = 4
PROJ_ROWS = 256
MERGE_ROWS = 256
VMEM_LIMIT = 56 * 1024 * 1024

_QS = HEAD_DIM ** -0.5 * LOG2E
_MS = MEM_HEAD_DIM ** -0.5 * LOG2E
EPI_ROPE, EPI_SCALE, EPI_SIGMOID, EPI_SILU = range(4)
STEPS = (
    (0, 0, EPI_ROPE, _QS), (1, 0, EPI_ROPE, 1.0), (2, 0, EPI_SCALE, 1.0),
    (9, 0, EPI_SCALE, _QS), (10, 0, EPI_SCALE, 1.0), (11, 0, EPI_SCALE, 1.0),
    (12, 0, EPI_SCALE, _MS),
    (14, 0, EPI_SILU, 1.0), (15, 0, EPI_SILU, 1.0),
    (18, 0, EPI_SIGMOID, 1.0), (19, 0, EPI_SIGMOID, 1.0),
    (20, 0, EPI_SIGMOID, 1.0), (21, 0, EPI_SIGMOID, 1.0),
    (16, 0, EPI_SIGMOID, 1.0), (17, 0, EPI_SIGMOID, 1.0),
    (3, 1, EPI_ROPE, _QS), (4, 1, EPI_ROPE, 1.0), (5, 1, EPI_SCALE, 1.0),
    (6, 2, EPI_ROPE, _QS), (7, 2, EPI_ROPE, 1.0), (8, 2, EPI_SCALE, 1.0),
    (13, 2, EPI_SILU, 1.0),
)
FIRST_MERGE_TILE = 16
N_STEPS = len(STEPS)
T_G0, T_NA, T_MEMQ, T_GATE_B, T_MERGE_B, T_MERGE_A = 0, 3, 6, 7, 9, 13
T_G1, T_G2, T_GATE_A = 15, 18, 21
SLOT_DIL = (1, 4, 16)

BF16 = jnp.bfloat16
F32 = jnp.float32


def _sigmoid(z):
    return 1.0 / (1.0 + jnp.exp(-z))


def _proj_kernel(wtile_ref, epi_ref, scale_ref, x_ref, g_ref, w_ref, mb_ref,
                 cos_ref, sin_ref, o_ref, h_ref, hf_ref, wb_ref, *, tm):
    j = pl.program_id(1)
    n_lane_tiles = x_ref.shape[1] // LANES

    @pl.when(j == 0)
    def _():
        rb = 256
        ssq = [jnp.sum(jnp.square(x_ref[r0:r0 + rb, :]), axis=-1, keepdims=True)
               for r0 in range(0, tm, rb)]
        rs = lax.rsqrt(jnp.concatenate(ssq, axis=0) * (1.0 / x_ref.shape[1]) + EPS)
        for c in range(n_lane_tiles):
            cs = slice(c * LANES, (c + 1) * LANES)
            hf = x_ref[:, cs] * rs * g_ref[:, cs]
            h_ref[0, :, cs] = hf.astype(BF16)
            hf_ref[...] = hf
            for slot, d in ((1, 4), (2, 16)):
                seg = tm // d
                for r in range(d):
                    h_ref[slot, r * seg:(r + 1) * seg, cs] = (
                        hf_ref[pl.ds(r, seg, stride=d), :].astype(BF16))

    slot = (j >= T_G1).astype(jnp.int32) + (j >= T_G2).astype(jnp.int32)
    q_scale = scale_ref[j]

    wb_ref[...] = w_ref[...].astype(BF16)

    def step(epi):
        lane = lax.broadcasted_iota(jnp.int32, (1, LANES), 1)
        first_half = lane % HEAD_DIM < ROPE_HALF
        if epi == EPI_SIGMOID:
            bias = mb_ref[pl.ds(jnp.maximum(wtile_ref[j] - FIRST_MERGE_TILE, 0), 1), :]
        for r0 in range(0, tm, PROJ_ROWS):
            rows = pl.ds(r0, PROJ_ROWS)
            acc = jnp.dot(h_ref[slot, rows, :], wb_ref[...], preferred_element_type=F32)
            if epi == EPI_ROPE:
                c, s = cos_ref[rows, :], sin_ref[rows, :]
            for t in range(WIDTH // LANES):
                lanes = slice(t * LANES, (t + 1) * LANES)
                a = acc[:, lanes]
                if epi == EPI_ROPE:
                    partner = jnp.where(first_half, pltpu.roll(a, LANES - ROPE_HALF, 1),
                                        pltpu.roll(a, ROPE_HALF, 1))
                    a = (a * c + partner * s) * q_scale
                elif epi == EPI_SCALE:
                    a = a * q_scale
                elif epi == EPI_SIGMOID:
                    a = _sigmoid(a + bias[:, lanes])
                else:
                    a = a * _sigmoid(a)
                o_ref[rows, lanes] = a.astype(BF16)

    for epi in (EPI_ROPE, EPI_SCALE, EPI_SIGMOID, EPI_SILU):
        pl.when(epi_ref[j] == epi)(functools.partial(step, epi))


def _proj(x2d, gain, w, merge_bias, cos_t, sin_t, *, batch, seq):
    rows, dm = x2d.shape
    tm = seq
    prefetch = (jnp.asarray([st[0] for st in STEPS], jnp.int32),
                jnp.asarray([st[2] for st in STEPS], jnp.int32),
                jnp.asarray([st[3] for st in STEPS], F32))
    mb_rows = merge_bias.reshape(-1, WIDTH)

    def slot_of(j):
        return (j >= T_G1).astype(jnp.int32) + (j >= T_G2).astype(jnp.int32)

    tab_spec = pl.BlockSpec((None, tm, LANES), lambda i, j, *_: (slot_of(j), 0, 0))
    return pl.pallas_call(
        functools.partial(_proj_kernel, tm=tm),
        out_shape=jax.ShapeDtypeStruct((batch, N_STEPS, seq, WIDTH), BF16),
        grid_spec=pltpu.PrefetchScalarGridSpec(
            num_scalar_prefetch=len(prefetch),
            grid=(rows // tm, N_STEPS),
            in_specs=[
                pl.BlockSpec((tm, dm), lambda i, j, *_: (i, 0)),
                pl.BlockSpec((1, dm), lambda i, j, *_: (0, 0)),
                pl.BlockSpec((dm, WIDTH), lambda i, j, wt, ep, sc: (0, wt[j])),
                pl.BlockSpec(mb_rows.shape, lambda i, j, *_: (0, 0)),
                tab_spec, tab_spec,
            ],
            out_specs=pl.BlockSpec((None, None, tm, WIDTH), lambda i, j, *_: (i, j, 0, 0)),
            scratch_shapes=[pltpu.VMEM((3, tm, dm), BF16), pltpu.VMEM((tm, LANES), F32),
                            pltpu.VMEM((dm, WIDTH), BF16)]),
        compiler_params=pltpu.CompilerParams(
            dimension_semantics=("arbitrary", "arbitrary"),
            vmem_limit_bytes=VMEM_LIMIT),
        name="proj",
    )(*prefetch, x2d, gain, w, mb_rows, cos_t, sin_t)


def _rope_tables(seq):
    pos = np.arange(seq, dtype=np.float64)
    inv = ROPE_THETA ** (-np.arange(ROPE_HALF, dtype=np.float64) * 2.0 / ROPE_DIM)
    ang = pos[:, None] * inv[None, :]
    cos, sin = np.cos(ang), np.sin(ang)
    rest = HEAD_DIM - ROPE_DIM
    c = np.concatenate([cos, cos, np.ones((seq, rest))] * (LANES // HEAD_DIM), axis=-1)
    s = np.concatenate([-sin, sin, np.zeros((seq, rest))] * (LANES // HEAD_DIM), axis=-1)

    def orders(t):
        out = [t]
        for d in SLOT_DIL[1:]:
            out.append(t.reshape(seq // d, d, LANES).transpose(1, 0, 2).reshape(seq, LANES))
        return jnp.asarray(np.stack(out), F32)

    return orders(c), orders(s)


def _kv_kernel(x_ref, g_ref, w_ref, o_ref):
    xf = x_ref[...]
    ms = jnp.mean(xf * xf, axis=-1, keepdims=True)
    h = (xf * lax.rsqrt(ms + EPS) * g_ref[...]).astype(BF16)
    o_ref[...] = jnp.dot(h, w_ref[...].astype(BF16), preferred_element_type=F32).astype(BF16)


def _kv_proj(mem2d, gain, w, *, tm):
    rows, dm = mem2d.shape
    n_out = w.shape[1]
    return pl.pallas_call(
        _kv_kernel,
        out_shape=jax.ShapeDtypeStruct((rows, n_out), BF16),
        grid=(rows // tm,),
        in_specs=[pl.BlockSpec((tm, dm), lambda i: (i, 0)),
                  pl.BlockSpec((1, dm), lambda i: (0, 0)),
                  pl.BlockSpec((dm, n_out), lambda i: (0, 0))],
        out_specs=pl.BlockSpec((tm, n_out), lambda i: (i, 0)),
        compiler_params=pltpu.CompilerParams(
            dimension_semantics=("arbitrary",), vmem_limit_bytes=VMEM_LIMIT),
        name="kv_proj",
    )(mem2d, gain, w)


def _dil_kernel(q0, k0, v0, q1, k1, v1, q2, k2, v2, o_ref,
                og0, lg0, og1, lg1, bias_scr, s_scr, p_scr, *, seq, reach):
    left = lax.broadcasted_iota(jnp.int32, (1, LANES), 1) < HEAD_DIM
    keep_a = jnp.where(left, 1.0, 0.0).astype(BF16)
    keep_b = jnp.where(left, 0.0, 1.0).astype(BF16)
    kw_max = 2 * QBLK

    @pl.when(jnp.logical_and(pl.program_id(0) == 0, pl.program_id(1) == 0))
    def _():
        rel = (lax.broadcasted_iota(jnp.int32, (QBLK, kw_max), 0)
               - lax.broadcasted_iota(jnp.int32, (QBLK, kw_max), 1))
        for n in range(3):
            bias_scr[n] = jnp.where(jnp.abs(rel + n * reach) <= reach, 0.0, NEG)

    def pair_tile(col_a, col_b):
        return jnp.where(left, jnp.broadcast_to(col_a, (QBLK, LANES)),
                         jnp.broadcast_to(col_b, (QBLK, LANES)))

    def keep(o_scr, l_scr, length):
        def finish(r, row0, acc, m, den):
            rows = pl.ds(r * length + row0, QBLK)
            o_scr[rows, :] = acc / den
            l_scr[rows, :] = m + jnp.log2(den)
        return finish

    def combine(r, row0, acc2, m2, den2):
        rows0 = pl.ds(r, QBLK, stride=16)
        rows1 = pl.ds((r % 4) * (seq // 4) + r // 4, QBLK, stride=4)
        l0, l1, l2 = lg0[rows0, :], lg1[rows1, :], m2 + jnp.log2(den2)
        m = jnp.maximum(jnp.maximum(l0, l1), l2)
        w0, w1, w2 = jnp.exp2(l0 - m), jnp.exp2(l1 - m), jnp.exp2(l2 - m)
        num = w0 * og0[rows0, :] + w1 * og1[rows1, :] + w2 * (acc2 / den2)
        o_ref[r] = (num / (w0 + w1 + w2)).astype(BF16)

    tiles = []
    for (q_ref, k_ref, v_ref), d, finish in (
            ((q0, k0, v0), 1, keep(og0, lg0, seq)),
            ((q1, k1, v1), 4, keep(og1, lg1, seq // 4)),
            ((q2, k2, v2), 16, combine)):
        length = seq // d
        kw = min(kw_max, length)
        for bi in range(seq // QBLK):
            r, row0 = divmod(bi * QBLK, length)
            ks = min(max(row0 - reach, 0), length - kw)
            tiles.append((q_ref, k_ref, v_ref, r, row0, ks, kw, finish))
    chunks = [tiles[i:i + CHUNK] for i in range(0, len(tiles), CHUNK)]

    def scores(c):
        for t, (q_ref, k_ref, _, r, row0, ks, kw, _) in enumerate(chunks[c]):
            q2d = q_ref[r, row0:row0 + QBLK, :]
            qq = jnp.concatenate([q2d * keep_a, q2d * keep_b], axis=0)
            s_scr[c % 2, t, :, :kw] = lax.dot_general(
                qq, k_ref[r, ks:ks + kw, :], (((1,), (1,)), ((), ())),
                preferred_element_type=F32)

    def softmax_pv(c):
        for t, (_, _, v_ref, r, row0, ks, kw, finish) in enumerate(chunks[c]):
            bias = bias_scr[(row0 - ks) // reach, :, :kw]
            s = s_scr[c % 2, t, :, :kw] + jnp.concatenate([bias, bias], axis=0)
            m = jnp.max(s, axis=-1, keepdims=True)
            p = jnp.exp2(s - m)
            den = jnp.sum(p, axis=-1, keepdims=True)
            p_scr[c % 2, t, :, :kw] = p.astype(BF16)
            pv = jnp.dot(p_scr[c % 2, t, :, :kw], v_ref[r, ks:ks + kw, :],
                         preferred_element_type=F32)
            finish(r, row0, jnp.where(left, pv[:QBLK], pv[QBLK:]),
                   pair_tile(m[:QBLK], m[QBLK:]), pair_tile(den[:QBLK], den[QBLK:]))

    scores(0)
    for c in range(len(chunks)):
        if c + 1 < len(chunks):
            scores(c + 1)
        softmax_pv(c)


def _dil_attn(proj, *, reach):
    batch, _, seq, _ = proj.shape
    n_hp = WIDTH // LANES

    def spec(first_tile, d, kind):
        return pl.BlockSpec((None, None, d, seq // d, LANES),
                            lambda b, hp: (b, first_tile + kind, 0, 0, hp))

    views = {d: proj.reshape(batch, N_STEPS, d, seq // d, WIDTH) for d in SLOT_DIL}
    f32_rows = pltpu.VMEM((seq, LANES), F32)
    return pl.pallas_call(
        functools.partial(_dil_kernel, seq=seq, reach=reach),
        out_shape=jax.ShapeDtypeStruct((batch, 16, seq // 16, WIDTH), BF16),
        grid=(batch, n_hp),
        in_specs=[spec(t0, d, kind) for t0, d in zip((T_G0, T_G1, T_G2), SLOT_DIL)
                  for kind in range(3)],
        out_specs=pl.BlockSpec((None, 16, seq // 16, LANES), lambda b, hp: (b, 0, 0, hp)),
        scratch_shapes=[f32_rows] * 4 + [
            pltpu.VMEM((3, QBLK, 2 * QBLK), F32),
            pltpu.VMEM((2, CHUNK, 2 * QBLK, 2 * QBLK), F32),
            pltpu.VMEM((2, CHUNK, 2 * QBLK, 2 * QBLK), BF16)],
        compiler_params=pltpu.CompilerParams(
            dimension_semantics=("arbitrary", "arbitrary"),
            vmem_limit_bytes=VMEM_LIMIT),
        name="dil_attn",
    )(*[views[d] for d in SLOT_DIL for _ in range(3)])


def _na_kernel(rpb_ref, q_ref, k_ref, v_ref, o_ref, bias_scr, s_scr, p_scr, *, rows):
    kr = min(NA_ROWS, rows)
    nk = kr * GRID_W
    n_var = 2 * NA_ROWS - kr
    n_rel = 2 * NA_COLS - 1
    left = lax.broadcasted_iota(jnp.int32, (1, LANES), 1) < HEAD_DIM
    keep_a = jnp.where(left, 1.0, 0.0).astype(BF16)
    keep_b = jnp.where(left, 0.0, 1.0).astype(BF16)

    @pl.when(pl.program_id(1) == 0)
    def _():
        qc = lax.broadcasted_iota(jnp.int32, (GRID_W, LANES), 0)
        kc = lax.broadcasted_iota(jnp.int32, (GRID_W, LANES), 1) % GRID_W
        rel = jnp.clip(kc - qc, -(NA_COLS - 1), NA_COLS - 1) + NA_COLS - 1
        c_start = jnp.clip(qc - NA_COLS // 2, 0, GRID_W - NA_COLS)
        valid = jnp.logical_and(kc >= c_start, kc < c_start + NA_COLS)
        for hh in range(LANES // HEAD_DIM):
            head = pl.program_id(0) * (LANES // HEAD_DIM) + hh
            for a in range(2 * NA_ROWS - 1):
                base = (head * (2 * NA_ROWS - 1) + a) * n_rel
                blk = jnp.zeros((GRID_W, LANES), F32)
                for d in range(n_rel):
                    blk = jnp.where(rel == d, rpb_ref[base + d], blk)
                blk = jnp.where(valid, blk * LOG2E, NEG)
                for j in range(kr):
                    if 0 <= a - j < n_var:
                        half = slice((j % 2) * GRID_W, (j % 2 + 1) * GRID_W)
                        bias_scr[hh, a - j, :, j * GRID_W:(j + 1) * GRID_W] = blk[:, half]

    def window(r):
        r_start = min(max(r - kr // 2, 0), rows - kr)
        return r * GRID_W, r_start * GRID_W, r_start - r + (NA_ROWS - 1)

    def scores(c):
        for t in range(CHUNK):
            row0, ks, _ = window(c * CHUNK + t)
            q2d = q_ref[row0:row0 + GRID_W, :]
            qq = jnp.concatenate([q2d * keep_a, q2d * keep_b], axis=0)
            s_scr[c % 2, t] = lax.dot_general(qq, k_ref[ks:ks + nk, :], (((1,), (1,)), ((), ())),
                                              preferred_element_type=F32)

    def softmax_pv(c):
        for t in range(CHUNK):
            row0, ks, variant = window(c * CHUNK + t)
            dens = []
            for hh in range(LANES // HEAD_DIM):
                rows_h = slice(hh * GRID_W, (hh + 1) * GRID_W)
                s = s_scr[c % 2, t, rows_h, :] + bias_scr[hh, variant]
                m = jnp.max(s, axis=-1, keepdims=True)
                p = jnp.exp2(s - m)
                dens.append(jnp.sum(p, axis=-1, keepdims=True))
                p_scr[t, rows_h, :] = p.astype(BF16)
            pv = jnp.dot(p_scr[t], v_ref[ks:ks + nk, :], preferred_element_type=F32)
            den_t = jnp.where(left, jnp.broadcast_to(dens[0], (GRID_W, LANES)),
                              jnp.broadcast_to(dens[1], (GRID_W, LANES)))
            o_ref[row0:row0 + GRID_W, :] = (
                jnp.where(left, pv[:GRID_W], pv[GRID_W:]) / den_t).astype(BF16)

    n_chunks = rows // CHUNK
    scores(0)
    for c in range(n_chunks):
        if c + 1 < n_chunks:
            scores(c + 1)
        softmax_pv(c)


def _na_attn(proj, rpb):
    batch, _, seq, _ = proj.shape
    rows = seq // GRID_W
    kr = min(NA_ROWS, rows)
    nk = kr * GRID_W
    n_hp = WIDTH // LANES
    assert rpb.shape == (WIDTH // HEAD_DIM, 2 * NA_ROWS - 1, 2 * NA_COLS - 1)

    def in_spec(kind):
        return pl.BlockSpec((None, None, seq, LANES), lambda hp, b, rp: (b, T_NA + kind, 0, hp))

    return pl.pallas_call(
        functools.partial(_na_kernel, rows=rows),
        out_shape=jax.ShapeDtypeStruct((batch, seq, WIDTH), BF16),
        grid_spec=pltpu.PrefetchScalarGridSpec(
            num_scalar_prefetch=1,
            grid=(n_hp, batch),
            in_specs=[in_spec(0), in_spec(1), in_spec(2)],
            out_specs=pl.BlockSpec((None, seq, LANES), lambda hp, b, rp: (b, 0, hp)),
            scratch_shapes=[pltpu.VMEM((LANES // HEAD_DIM, 2 * NA_ROWS - kr, GRID_W, nk), F32),
                            pltpu.VMEM((2, CHUNK, 2 * GRID_W, nk), F32),
                            pltpu.VMEM((CHUNK, 2 * GRID_W, nk), BF16)]),
        compiler_params=pltpu.CompilerParams(
            dimension_semantics=("arbitrary", "arbitrary"), vmem_limit_bytes=VMEM_LIMIT),
        name="na_attn",
    )(rpb.reshape(-1).astype(F32), proj, proj, proj)


def _mem_kernel(q_ref, kv_ref, o_ref, s_scr, p_scr, rden_scr, *, seq):
    mem_width = MEM_HEADS * MEM_HEAD_DIM
    rb = 512

    for h in range(MEM_HEADS):
        cs = slice(h * MEM_HEAD_DIM, (h + 1) * MEM_HEAD_DIM)
        vs = slice(mem_width + h * MEM_HEAD_DIM, mem_width + (h + 1) * MEM_HEAD_DIM)
        s_scr[...] = lax.dot_general(q_ref[:, cs], kv_ref[:, cs], (((1,), (1,)), ((), ())),
                                     preferred_element_type=F32)

        def softmax_rows(i, carry):
            for t in range(rb // QBLK):
                r0 = pl.multiple_of(i * rb + t * QBLK, QBLK)
                s = s_scr[pl.ds(r0, QBLK), :]
                m = jnp.max(s, axis=-1, keepdims=True)
                p = jnp.exp2(s - m)
                den = jnp.sum(p, axis=-1, keepdims=True)
                p_scr[pl.ds(r0, QBLK), :] = p.astype(BF16)
                rden_scr[pl.ds(r0, QBLK), :] = jnp.broadcast_to(1.0 / den, (QBLK, LANES))
            return carry

        lax.fori_loop(0, seq // rb, softmax_rows, 0)
        pv = jnp.dot(p_scr[...], kv_ref[:, vs], preferred_element_type=F32)
        o_ref[:, cs] = (pv * rden_scr[...]).astype(BF16)


def _mem_attn(proj, kv_m):
    batch, _, seq, _ = proj.shape
    mem_len, kv_w = kv_m.shape[1:]
    return pl.pallas_call(
        functools.partial(_mem_kernel, seq=seq),
        out_shape=jax.ShapeDtypeStruct((batch, seq, WIDTH), BF16),
        grid=(batch,),
        in_specs=[pl.BlockSpec((None, None, seq, WIDTH), lambda b: (b, T_MEMQ, 0, 0)),
                  pl.BlockSpec((None, mem_len, kv_w), lambda b: (b, 0, 0))],
        out_specs=pl.BlockSpec((None, seq, WIDTH), lambda b: (b, 0, 0)),
        scratch_shapes=[pltpu.VMEM((seq, mem_len), F32), pltpu.VMEM((seq, mem_len), BF16),
                        pltpu.VMEM((seq, LANES), F32)],
        compiler_params=pltpu.CompilerParams(
            dimension_semantics=("arbitrary",), vmem_limit_bytes=VMEM_LIMIT),
        name="mem_attn",
    )(proj, kv_m)


def _merge_kernel(oa, ga, ob, oc, gb, gc, za0, za1, zb0, zb1, zc0, zc1,
                  wa_f32, wb_f32, wc_f32, wo_f32, pn_ref, x_ref, out_ref,
                  u_scr, wa_ref, wb_ref, wc_ref, wo_ref, *, tm):
    @pl.when(pl.program_id(0) == 0)
    def _():
        for src, dst in ((wa_f32, wa_ref), (wb_f32, wb_ref), (wc_f32, wc_ref), (wo_f32, wo_ref)):
            dst[...] = src[...].astype(BF16)

    def silu_gated(o, silu_g):
        return o.astype(F32) * silu_g.astype(F32)

    seg = MERGE_ROWS // 16
    for blk in range(tm // MERGE_ROWS):
        rows = slice(blk * MERGE_ROWS, (blk + 1) * MERGE_ROWS)

        def project(u, w_ref, gate_lo, gate_hi):
            yb = jnp.dot(u.astype(BF16), w_ref[...], preferred_element_type=F32)
            gate = jnp.concatenate([gate_lo[rows, :], gate_hi[rows, :]], axis=-1)
            return gate.astype(F32) * yb

        p16_rows = slice(blk * seg, (blk + 1) * seg)
        ua = silu_gated(oa[:, p16_rows, :].reshape(MERGE_ROWS, WIDTH),
                        ga[:, p16_rows, :].reshape(MERGE_ROWS, WIDTH))
        for r in range(16):
            for c in range(WIDTH // LANES):
                u_scr[blk, c, pl.ds(r, seg, stride=16), :] = ua[r * seg:(r + 1) * seg,
                                                                c * LANES:(c + 1) * LANES]
        ua = jnp.concatenate([u_scr[blk, c] for c in range(WIDTH // LANES)], axis=-1)

        y = (project(ua, wa_ref, za0, za1)
             + project(silu_gated(ob[rows, :], gb[rows, :]), wb_ref, zb0, zb1)
             + project(silu_gated(oc[rows, :], gc[rows, :]), wc_ref, zc0, zc1))
        z = jnp.dot(y.astype(BF16), wo_ref[...], preferred_element_type=F32)
        ms = jnp.mean(z * z, axis=-1, keepdims=True)
        out_ref[rows, :] = x_ref[rows, :] + z * lax.rsqrt(ms + EPS) * pn_ref[...]


def _merge(x2d, proj, out_a16, out_b, out_c, wa, wb, wc, wo, post_norm, *, tm):
    rows, dm = x2d.shape
    batch, _, seq, _ = proj.shape
    per_b = seq // tm
    seg = tm // 16
    proj16 = proj.reshape(batch, N_STEPS, 16, seq // 16, WIDTH)

    def p16_spec(tile):
        return pl.BlockSpec((None, None, 16, seg, WIDTH),
                            lambda i: (i // per_b, tile, 0, i % per_b, 0))

    def nat_spec(tile):
        return pl.BlockSpec((None, None, tm, WIDTH), lambda i: (i // per_b, tile, i % per_b, 0))

    def rows_spec(width):
        return pl.BlockSpec((tm, width), lambda i: (i, 0))

    def full(shape):
        return pl.BlockSpec(shape, lambda i: (0, 0), pipeline_mode=pl.Buffered(1))

    return pl.pallas_call(
        functools.partial(_merge_kernel, tm=tm),
        out_shape=jax.ShapeDtypeStruct((rows, dm), F32),
        grid=(rows // tm,),
        in_specs=[pl.BlockSpec((None, 16, seg, WIDTH), lambda i: (i // per_b, 0, i % per_b, 0)),
                  p16_spec(T_GATE_A), rows_spec(WIDTH), rows_spec(WIDTH),
                  nat_spec(T_GATE_B), nat_spec(T_GATE_B + 1),
                  nat_spec(T_MERGE_A), nat_spec(T_MERGE_A + 1),
                  nat_spec(T_MERGE_B), nat_spec(T_MERGE_B + 1),
                  nat_spec(T_MERGE_B + 2), nat_spec(T_MERGE_B + 3),
                  full(wa.shape), full(wb.shape), full(wc.shape),
                  full(wo.shape), full(post_norm.shape), rows_spec(dm)],
        out_specs=pl.BlockSpec((tm, dm), lambda i: (i, 0)),
        scratch_shapes=[pltpu.VMEM((tm // MERGE_ROWS, WIDTH // LANES, MERGE_ROWS, LANES), F32)]
        + [pltpu.VMEM(w.shape, BF16) for w in (wa, wb, wc, wo)],
        compiler_params=pltpu.CompilerParams(
            dimension_semantics=("arbitrary",), vmem_limit_bytes=VMEM_LIMIT),
        name="merge",
    )(out_a16, proj16, out_b, out_c, proj, proj, proj, proj, proj, proj, proj, proj,
      wa, wb, wc, wo, post_norm, x2d)


def kernel(x, mem, pre_norm, w_in, merge_bias, na_rpb, mem_norm, w_mem_kv,
           w_branch_a, w_branch_b, w_branch_c, w_out, post_norm):
    b, s, dm = x.shape
    depth = pre_norm.shape[0]
    reach = (DIL_CONFIGS[0][0] // 2) // DIL_CONFIGS[0][1]
    assert all((w // 2) // d == reach for w, d in DIL_CONFIGS) and 2 * reach == QBLK
    assert tuple(d for _, d in DIL_CONFIGS) == SLOT_DIL
    cos_t, sin_t = _rope_tables(s)
    for layer in range(depth):
        x2d = x.reshape(b * s, dm)
        proj = _proj(x2d, pre_norm[layer][None], w_in[layer], merge_bias[layer], cos_t, sin_t,
                     batch=b, seq=s)

        out_a16 = _dil_attn(proj, reach=reach)
        out_b = _na_attn(proj, na_rpb[layer])

        mem2d = mem.reshape(b * mem.shape[1], dm)
        kv_m = _kv_proj(mem2d, mem_norm[layer][None], w_mem_kv[layer], tm=1024)
        out_c = _mem_attn(proj, kv_m.reshape(b, mem.shape[1], -1))

        y = _merge(x2d, proj, out_a16, out_b.reshape(b * s, WIDTH), out_c.reshape(b * s, WIDTH),
                   w_branch_a[layer], w_branch_b[layer], w_branch_c[layer],
                   w_out[layer], post_norm[layer][None], tm=2 * MERGE_ROWS)
        x = y.reshape(b, s, dm)
    return x
```

```python
import functools
import math

import jax
import jax.numpy as jnp
import numpy as np
from jax import lax
from jax.experimental import pallas as pl
from jax.experimental.pallas import tpu as pltpu

D_MODEL = 1024
HEAD_DIM = 64
DIL_CONFIGS = ((128, 1), (512, 4), (2048, 16))
WIDTH = 512
NA_ROWS = 8
NA_COLS = 16
GRID_W = 64
MEM_HEADS = 4
MEM_HEAD_DIM = 128
ROPE_THETA = 500000.0
ROPE_DIM = HEAD_DIM // 4
ROPE_HALF = ROPE_DIM // 2
EPS = 1e-6
NEG = -1e30
LOG2E = math.log2(math.e)

LANES = 128
QBLK = 128
CHUNK = 4
PROJ_ROWS = 256
MERGE_ROWS = 256
VMEM_LIMIT = 56 * 1024 * 1024

_QS = HEAD_DIM ** -0.5 * LOG2E
_MS = MEM_HEAD_DIM ** -0.5 * LOG2E
EPI_ROPE, EPI_SCALE, EPI_SIGMOID, EPI_SILU = range(4)
STEPS = (
    (0, 0, EPI_ROPE, _QS), (1, 0, EPI_ROPE, 1.0), (2, 0, EPI_SCALE, 1.0),
    (9, 0, EPI_SCALE, _QS), (10, 0, EPI_SCALE, 1.0), (11, 0, EPI_SCALE, 1.0),
    (12, 0, EPI_SCALE, _MS),
    (14, 0, EPI_SILU, 1.0), (15, 0, EPI_SILU, 1.0),
    (18, 0, EPI_SIGMOID, 1.0), (19, 0, EPI_SIGMOID, 1.0),
    (20, 0, EPI_SIGMOID, 1.0), (21, 0, EPI_SIGMOID, 1.0),
    (16, 0, EPI_SIGMOID, 1.0), (17, 0, EPI_SIGMOID, 1.0),
    (3, 1, EPI_ROPE, _QS), (4, 1, EPI_ROPE, 1.0), (5, 1, EPI_SCALE, 1.0),
    (6, 2, EPI_ROPE, _QS), (7, 2, EPI_ROPE, 1.0), (8, 2, EPI_SCALE, 1.0),
    (13, 2, EPI_SILU, 1.0),
)
FIRST_MERGE_TILE = 16
N_STEPS = len(STEPS)
T_G0, T_NA, T_MEMQ, T_GATE_B, T_MERGE_B, T_MERGE_A = 0, 3, 6, 7, 9, 13
T_G1, T_G2, T_GATE_A = 15, 18, 21
SLOT_DIL = (1, 4, 16)

BF16 = jnp.bfloat16
F32 = jnp.float32


def _sigmoid(z):
    return 0.5 * jnp.tanh(0.5 * z) + 0.5


def _proj_kernel(wtile_ref, epi_ref, scale_ref, x_ref, g_ref, w_ref, mb_ref,
                 cos_ref, sin_ref, o_ref, h_ref, hf_ref, wb_ref, *, tm):
    j = pl.program_id(1)
    n_lane_tiles = x_ref.shape[1] // LANES

    @pl.when(j == 0)
    def _():
        rb = 256
        ssq = [jnp.sum(jnp.square(x_ref[r0:r0 + rb, :]), axis=-1, keepdims=True)
               for r0 in range(0, tm, rb)]
        rs = lax.rsqrt(jnp.concatenate(ssq, axis=0) * (1.0 / x_ref.shape[1]) + EPS)
        for c in range(n_lane_tiles):
            cs = slice(c * LANES, (c + 1) * LANES)
            hf = x_ref[:, cs] * rs * g_ref[:, cs]
            h_ref[0, :, cs] = hf.astype(BF16)
            hf_ref[...] = hf
            for slot, d in ((1, 4), (2, 16)):
                seg = tm // d
                for r in range(d):
                    h_ref[slot, r * seg:(r + 1) * seg, cs] = (
                        hf_ref[pl.ds(r, seg, stride=d), :].astype(BF16))

    slot = (j >= T_G1).astype(jnp.int32) + (j >= T_G2).astype(jnp.int32)
    q_scale = scale_ref[j]

    def step(epi):
        wb_ref[...] = w_ref[...].astype(BF16)
        lane = lax.broadcasted_iota(jnp.int32, (1, LANES), 1)
        first_half = lane % HEAD_DIM < ROPE_HALF
        if epi == EPI_SIGMOID:
            bias = mb_ref[pl.ds(jnp.maximum(wtile_ref[j] - FIRST_MERGE_TILE, 0), 1), :]
        for r0 in range(0, tm, PROJ_ROWS):
            rows = pl.ds(r0, PROJ_ROWS)
            acc = jnp.dot(h_ref[slot, rows, :], wb_ref[...], preferred_element_type=F32)
            if epi == EPI_ROPE:
                c, s = cos_ref[rows, :], sin_ref[rows, :]
            for t in range(WIDTH // LANES):
                lanes = slice(t * LANES, (t + 1) * LANES)
                a = acc[:, lanes]
                if epi == EPI_ROPE:
                    partner = jnp.where(first_half, pltpu.roll(a, LANES - ROPE_HALF, 1),
                                        pltpu.roll(a, ROPE_HALF, 1))
                    a = (a * c + partner * s) * q_scale
                elif epi == EPI_SCALE:
                    a = a * q_scale
                elif epi == EPI_SIGMOID:
                    a = _sigmoid(a + bias[:, lanes])
                else:
                    a = a * _sigmoid(a)
                o_ref[rows, lanes] = a.astype(BF16)

    for epi in (EPI_ROPE, EPI_SCALE, EPI_SIGMOID, EPI_SILU):
        pl.when(epi_ref[j] == epi)(functools.partial(step, epi))


def _proj(x2d, gain, w, merge_bias, cos_t, sin_t, *, batch, seq):
    rows, dm = x2d.shape
    tm = seq
    prefetch = (jnp.asarray([st[0] for st in STEPS], jnp.int32),
                jnp.asarray([st[2] for st in STEPS], jnp.int32),
                jnp.asarray([st[3] for st in STEPS], F32))
    mb_rows = merge_bias.reshape(-1, WIDTH)

    def slot_of(j):
        return (j >= T_G1).astype(jnp.int32) + (j >= T_G2).astype(jnp.int32)

    tab_spec = pl.BlockSpec((None, tm, LANES), lambda i, j, *_: (slot_of(j), 0, 0))
    return pl.pallas_call(
        functools.partial(_proj_kernel, tm=tm),
        out_shape=jax.ShapeDtypeStruct((batch, N_STEPS, seq, WIDTH), BF16),
        grid_spec=pltpu.PrefetchScalarGridSpec(
            num_scalar_prefetch=len(prefetch),
            grid=(rows // tm, N_STEPS),
            in_specs=[
                pl.BlockSpec((tm, dm), lambda i, j, *_: (i, 0)),
                pl.BlockSpec((1, dm), lambda i, j, *_: (0, 0)),
                pl.BlockSpec((dm, WIDTH), lambda i, j, wt, ep, sc: (0, wt[j])),
                pl.BlockSpec(mb_rows.shape, lambda i, j, *_: (0, 0)),
                tab_spec, tab_spec,
            ],
            out_specs=pl.BlockSpec((None, None, tm, WIDTH), lambda i, j, *_: (i, j, 0, 0)),
            scratch_shapes=[pltpu.VMEM((3, tm, dm), BF16), pltpu.VMEM((tm, LANES), F32),
                            pltpu.VMEM((dm, WIDTH), BF16)]),
        compiler_params=pltpu.CompilerParams(
            dimension_semantics=("arbitrary", "arbitrary"),
            vmem_limit_bytes=VMEM_LIMIT),
        name="proj",
    )(*prefetch, x2d, gain, w, mb_rows, cos_t, sin_t)


def _rope_tables(seq):
    pos = np.arange(seq, dtype=np.float64)
    inv = ROPE_THETA ** (-np.arange(ROPE_HALF, dtype=np.float64) * 2.0 / ROPE_DIM)
    ang = pos[:, None] * inv[None, :]
    cos, sin = np.cos(ang), np.sin(ang)
    rest = HEAD_DIM - ROPE_DIM
    c = np.concatenate([cos, cos, np.ones((seq, rest))] * (LANES // HEAD_DIM), axis=-1)
    s = np.concatenate([-sin, sin, np.zeros((seq, rest))] * (LANES // HEAD_DIM), axis=-1)

    def orders(t):
        out = [t]
        for d in SLOT_DIL[1:]:
            out.append(t.reshape(seq // d, d, LANES).transpose(1, 0, 2).reshape(seq, LANES))
        return jnp.asarray(np.stack(out), F32)

    return orders(c), orders(s)


def _kv_kernel(x_ref, g_ref, w_ref, o_ref):
    xf = x_ref[...]
    ms = jnp.mean(xf * xf, axis=-1, keepdims=True)
    h = (xf * lax.rsqrt(ms + EPS) * g_ref[...]).astype(BF16)
    o_ref[...] = jnp.dot(h, w_ref[...].astype(BF16), preferred_element_type=F32).astype(BF16)


def _kv_proj(mem2d, gain, w, *, tm):
    rows, dm = mem2d.shape
    n_out = w.shape[1]
    return pl.pallas_call(
        _kv_kernel,
        out_shape=jax.ShapeDtypeStruct((rows, n_out), BF16),
        grid=(rows // tm,),
        in_specs=[pl.BlockSpec((tm, dm), lambda i: (i, 0)),
                  pl.BlockSpec((1, dm), lambda i: (0, 0)),
                  pl.BlockSpec((dm, n_out), lambda i: (0, 0))],
        out_specs=pl.BlockSpec((tm, n_out), lambda i: (i, 0)),
        compiler_params=pltpu.CompilerParams(
            dimension_semantics=("arbitrary",), vmem_limit_bytes=VMEM_LIMIT),
        name="kv_proj",
    )(mem2d, gain, w)


def _dil_kernel(q0, k0, v0, q1, k1, v1, q2, k2, v2, o_ref,
                og0, lg0, og1, lg1, bias_scr, s_scr, p_scr, *, seq, reach):
    left = lax.broadcasted_iota(jnp.int32, (1, LANES), 1) < HEAD_DIM
    keep_a = jnp.where(left, 1.0, 0.0).astype(BF16)
    keep_b = jnp.where(left, 0.0, 1.0).astype(BF16)
    kw_max = 2 * QBLK

    @pl.when(jnp.logical_and(pl.program_id(0) == 0, pl.program_id(1) == 0))
    def _():
        rel = (lax.broadcasted_iota(jnp.int32, (QBLK, kw_max), 0)
               - lax.broadcasted_iota(jnp.int32, (QBLK, kw_max), 1))
        for n in range(3):
            bias_scr[n] = jnp.where(jnp.abs(rel + n * reach) <= reach, 0.0, NEG)

    def pair_tile(col_a, col_b):
        return jnp.where(left, jnp.broadcast_to(col_a, (QBLK, LANES)),
                         jnp.broadcast_to(col_b, (QBLK, LANES)))

    def keep(o_scr, l_scr, length):
        def finish(r, row0, acc, m, den):
            rows = pl.ds(r * length + row0, QBLK)
            o_scr[rows, :] = acc / den
            l_scr[rows, :] = m + jnp.log2(den)
        return finish

    def combine(r, row0, acc2, m2, den2):
        rows0 = pl.ds(r, QBLK, stride=16)
        rows1 = pl.ds((r % 4) * (seq // 4) + r // 4, QBLK, stride=4)
        l0, l1, l2 = lg0[rows0, :], lg1[rows1, :], m2 + jnp.log2(den2)
        m = jnp.maximum(jnp.maximum(l0, l1), l2)
        w0, w1, w2 = jnp.exp2(l0 - m), jnp.exp2(l1 - m), jnp.exp2(l2 - m)
        num = w0 * og0[rows0, :] + w1 * og1[rows1, :] + w2 * (acc2 / den2)
        o_ref[r] = (num / (w0 + w1 + w2)).astype(BF16)

    tiles = []
    for (q_ref, k_ref, v_ref), d, finish in (
            ((q0, k0, v0), 1, keep(og0, lg0, seq)),
            ((q1, k1, v1), 4, keep(og1, lg1, seq // 4)),
            ((q2, k2, v2), 16, combine)):
        length = seq // d
        kw = min(kw_max, length)
        for bi in range(seq // QBLK):
            r, row0 = divmod(bi * QBLK, length)
            ks = min(max(row0 - reach, 0), length - kw)
            tiles.append((q_ref, k_ref, v_ref, r, row0, ks, kw, finish))
    chunks = [tiles[i:i + CHUNK] for i in range(0, len(tiles), CHUNK)]

    def scores(c):
        for t, (q_ref, k_ref, _, r, row0, ks, kw, _) in enumerate(chunks[c]):
            q2d = q_ref[r, row0:row0 + QBLK, :]
            qq = jnp.concatenate([q2d * keep_a, q2d * keep_b], axis=0)
            s_scr[c % 2, t, :, :kw] = lax.dot_general(
                qq, k_ref[r, ks:ks + kw, :], (((1,), (1,)), ((), ())),
                preferred_element_type=F32)

    def softmax_pv(c):
        for t, (_, _, v_ref, r, row0, ks, kw, finish) in enumerate(chunks[c]):
            bias = bias_scr[(row0 - ks) // reach, :, :kw]
            s = s_scr[c % 2, t, :, :kw] + jnp.concatenate([bias, bias], axis=0)
            m = jnp.max(s, axis=-1, keepdims=True)
            p = jnp.exp2(s - m)
            den = jnp.sum(p, axis=-1, keepdims=True)
            p_scr[c % 2, t, :, :kw] = p.astype(BF16)
            pv = jnp.dot(p_scr[c % 2, t, :, :kw], v_ref[r, ks:ks + kw, :],
                         preferred_element_type=F32)
            finish(r, row0, jnp.where(left, pv[:QBLK], pv[QBLK:]),
                   pair_tile(m[:QBLK], m[QBLK:]), pair_tile(den[:QBLK], den[QBLK:]))

    scores(0)
    for c in range(len(chunks)):
        if c + 1 < len(chunks):
            scores(c + 1)
        softmax_pv(c)


def _dil_attn(proj, *, reach):
    batch, _, seq, _ = proj.shape
    n_hp = WIDTH // LANES

    def spec(first_tile, d, kind):
        return pl.BlockSpec((None, None, d, seq // d, LANES),
                            lambda b, hp: (b, first_tile + kind, 0, 0, hp))

    views = {d: proj.reshape(batch, N_STEPS, d, seq // d, WIDTH) for d in SLOT_DIL}
    f32_rows = pltpu.VMEM((seq, LANES), F32)
    return pl.pallas_call(
        functools.partial(_dil_kernel, seq=seq, reach=reach),
        out_shape=jax.ShapeDtypeStruct((batch, 16, seq // 16, WIDTH), BF16),
        grid=(batch, n_hp),
        in_specs=[spec(t0, d, kind) for t0, d in zip((T_G0, T_G1, T_G2), SLOT_DIL)
                  for kind in range(3)],
        out_specs=pl.BlockSpec((None, 16, seq // 16, LANES), lambda b, hp: (b, 0, 0, hp)),
        scratch_shapes=[f32_rows] * 4 + [
            pltpu.VMEM((3, QBLK, 2 * QBLK), F32),
            pltpu.VMEM((2, CHUNK, 2 * QBLK, 2 * QBLK), F32),
            pltpu.VMEM((2, CHUNK, 2 * QBLK, 2 * QBLK), BF16)],
        compiler_params=pltpu.CompilerParams(
            dimension_semantics=("arbitrary", "arbitrary"),
            vmem_limit_bytes=VMEM_LIMIT),
        name="dil_attn",
    )(*[views[d] for d in SLOT_DIL for _ in range(3)])


def _na_kernel(q_ref, k_ref, v_ref, rpb_ref, o_ref, bias_scr, s_scr, p_scr, *, rows):
    kr = min(NA_ROWS, rows)
    nk = kr * GRID_W
    n_var = 2 * NA_ROWS - kr
    left = lax.broadcasted_iota(jnp.int32, (1, LANES), 1) < HEAD_DIM
    keep_a = jnp.where(left, 1.0, 0.0).astype(BF16)
    keep_b = jnp.where(left, 0.0, 1.0).astype(BF16)

    @pl.when(pl.program_id(1) == 0)
    def _():
        qc = lax.broadcasted_iota(jnp.int32, (GRID_W, LANES), 0)
        kc = lax.broadcasted_iota(jnp.int32, (GRID_W, LANES), 1) % GRID_W
        rel = jnp.clip(kc - qc, -(NA_COLS - 1), NA_COLS - 1) + NA_COLS - 1
        c_start = jnp.clip(qc - NA_COLS // 2, 0, GRID_W - NA_COLS)
        valid = jnp.logical_and(kc >= c_start, kc < c_start + NA_COLS)
        for hh in range(LANES // HEAD_DIM):
            head = pl.program_id(0) * (LANES // HEAD_DIM) + hh
            for a in range(2 * NA_ROWS - 1):
                row = rpb_ref[pl.ds(head * (2 * NA_ROWS - 1) + a, 1), :]
                blk = jnp.take_along_axis(jnp.broadcast_to(row, (GRID_W, LANES)), rel, axis=1)
                blk = jnp.where(valid, blk * LOG2E, NEG)
                for j in range(kr):
                    if 0 <= a - j < n_var:
                        half = slice((j % 2) * GRID_W, (j % 2 + 1) * GRID_W)
                        bias_scr[hh, a - j, :, j * GRID_W:(j + 1) * GRID_W] = blk[:, half]

    def window(r):
        r_start = min(max(r - kr // 2, 0), rows - kr)
        return r * GRID_W, r_start * GRID_W, r_start - r + (NA_ROWS - 1)

    def scores(c):
        for t in range(CHUNK):
            row0, ks, _ = window(c * CHUNK + t)
            q2d = q_ref[row0:row0 + GRID_W, :]
            qq = jnp.concatenate([q2d * keep_a, q2d * keep_b], axis=0)
            s_scr[c % 2, t] = lax.dot_general(qq, k_ref[ks:ks + nk, :], (((1,), (1,)), ((), ())),
                                              preferred_element_type=F32)

    def softmax_pv(c):
        for t in range(CHUNK):
            row0, ks, variant = window(c * CHUNK + t)
            dens = []
            for hh in range(LANES // HEAD_DIM):
                rows_h = slice(hh * GRID_W, (hh + 1) * GRID_W)
                s = s_scr[c % 2, t, rows_h, :] + bias_scr[hh, variant]
                m = jnp.max(s, axis=-1, keepdims=True)
                p = jnp.exp2(s - m)
                dens.append(jnp.sum(p, axis=-1, keepdims=True))
                p_scr[t, rows_h, :] = p.astype(BF16)
            pv = jnp.dot(p_scr[t], v_ref[ks:ks + nk, :], preferred_element_type=F32)
            den_t = jnp.where(left, jnp.broadcast_to(dens[0], (GRID_W, LANES)),
                              jnp.broadcast_to(dens[1], (GRID_W, LANES)))
            o_ref[row0:row0 + GRID_W, :] = (
                jnp.where(left, pv[:GRID_W], pv[GRID_W:]) / den_t).astype(BF16)

    n_chunks = rows // CHUNK
    scores(0)
    for c in range(n_chunks):
        if c + 1 < n_chunks:
            scores(c + 1)
        softmax_pv(c)


def _na_attn(proj, rpb):
    batch, _, seq, _ = proj.shape
    rows = seq // GRID_W
    kr = min(NA_ROWS, rows)
    nk = kr * GRID_W
    n_hp = WIDTH // LANES
    assert rpb.shape == (WIDTH // HEAD_DIM, 2 * NA_ROWS - 1, 2 * NA_COLS - 1)

    def in_spec(kind):
        return pl.BlockSpec((None, None, seq, LANES), lambda hp, b: (b, T_NA + kind, 0, hp))

    rpb_rows = jnp.pad(rpb.reshape(-1, rpb.shape[-1]).astype(F32),
                       ((0, 0), (0, LANES - rpb.shape[-1])))
    return pl.pallas_call(
        functools.partial(_na_kernel, rows=rows),
        out_shape=jax.ShapeDtypeStruct((batch, seq, WIDTH), BF16),
        grid=(n_hp, batch),
        in_specs=[in_spec(0), in_spec(1), in_spec(2),
                  pl.BlockSpec(rpb_rows.shape, lambda hp, b: (0, 0))],
        out_specs=pl.BlockSpec((None, seq, LANES), lambda hp, b: (b, 0, hp)),
        scratch_shapes=[pltpu.VMEM((LANES // HEAD_DIM, 2 * NA_ROWS - kr, GRID_W, nk), F32),
                        pltpu.VMEM((2, CHUNK, 2 * GRID_W, nk), F32),
                        pltpu.VMEM((CHUNK, 2 * GRID_W, nk), BF16)],
        compiler_params=pltpu.CompilerParams(
            dimension_semantics=("arbitrary", "arbitrary"), vmem_limit_bytes=VMEM_LIMIT),
        name="na_attn",
    )(proj, proj, proj, rpb_rows)


def _mem_kernel(q_ref, kv_ref, o_ref, s_scr, p_scr, rden_scr, *, seq):
    mem_width = MEM_HEADS * MEM_HEAD_DIM
    rb = 512

    for h in range(MEM_HEADS):
        cs = slice(h * MEM_HEAD_DIM, (h + 1) * MEM_HEAD_DIM)
        vs = slice(mem_width + h * MEM_HEAD_DIM, mem_width + (h + 1) * MEM_HEAD_DIM)
        s_scr[...] = lax.dot_general(q_ref[:, cs], kv_ref[:, cs], (((1,), (1,)), ((), ())),
                                     preferred_element_type=F32)

        def softmax_rows(i, carry):
            for t in range(rb // QBLK):
                r0 = pl.multiple_of(i * rb + t * QBLK, QBLK)
                s = s_scr[pl.ds(r0, QBLK), :]
                m = jnp.max(s, axis=-1, keepdims=True)
                p = jnp.exp2(s - m)
                den = jnp.sum(p, axis=-1, keepdims=True)
                p_scr[pl.ds(r0, QBLK), :] = p.astype(BF16)
                rden_scr[pl.ds(r0, QBLK), :] = jnp.broadcast_to(1.0 / den, (QBLK, LANES))
            return carry

        lax.fori_loop(0, seq // rb, softmax_rows, 0)
        pv = jnp.dot(p_scr[...], kv_ref[:, vs], preferred_element_type=F32)
        o_ref[:, cs] = (pv * rden_scr[...]).astype(BF16)


def _mem_attn(proj, kv_m):
    batch, _, seq, _ = proj.shape
    mem_len, kv_w = kv_m.shape[1:]
    return pl.pallas_call(
        functools.partial(_mem_kernel, seq=seq),
        out_shape=jax.ShapeDtypeStruct((batch, seq, WIDTH), BF16),
        grid=(batch,),
        in_specs=[pl.BlockSpec((None, None, seq, WIDTH), lambda b: (b, T_MEMQ, 0, 0)),
                  pl.BlockSpec((None, mem_len, kv_w), lambda b: (b, 0, 0))],
        out_specs=pl.BlockSpec((None, seq, WIDTH), lambda b: (b, 0, 0)),
        scratch_shapes=[pltpu.VMEM((seq, mem_len), F32), pltpu.VMEM((seq, mem_len), BF16),
                        pltpu.VMEM((seq, LANES), F32)],
        compiler_params=pltpu.CompilerParams(
            dimension_semantics=("arbitrary",), vmem_limit_bytes=VMEM_LIMIT),
        name="mem_attn",
    )(proj, kv_m)


def _merge_kernel(oa, ga, ob, oc, gb, gc, za0, za1, zb0, zb1, zc0, zc1,
                  wa_f32, wb_f32, wc_f32, wo_f32, pn_ref, x_ref, out_ref,
                  u_scr, wa_ref, wb_ref, wc_ref, wo_ref, *, tm):
    @pl.when(pl.program_id(0) == 0)
    def _():
        for src, dst in ((wa_f32, wa_ref), (wb_f32, wb_ref), (wc_f32, wc_ref), (wo_f32, wo_ref)):
            dst[...] = src[...].astype(BF16)

    def silu_gated(o, silu_g):
        return o.astype(F32) * silu_g.astype(F32)

    seg = MERGE_ROWS // 16
    for blk in range(tm // MERGE_ROWS):
        rows = slice(blk * MERGE_ROWS, (blk + 1) * MERGE_ROWS)

        def project(u, w_ref, gate_lo, gate_hi):
            yb = jnp.dot(u.astype(BF16), w_ref[...], preferred_element_type=F32)
            gate = jnp.concatenate([gate_lo[rows, :], gate_hi[rows, :]], axis=-1)
            return gate.astype(F32) * yb

        p16_rows = slice(blk * seg, (blk + 1) * seg)
        ua = silu_gated(oa[:, p16_rows, :].reshape(MERGE_ROWS, WIDTH),
                        ga[:, p16_rows, :].reshape(MERGE_ROWS, WIDTH))
        for r in range(16):
            for c in range(WIDTH // LANES):
                u_scr[blk, c, pl.ds(r, seg, stride=16), :] = ua[r * seg:(r + 1) * seg,
                                                                c * LANES:(c + 1) * LANES]
        ua = jnp.concatenate([u_scr[blk, c] for c in range(WIDTH // LANES)], axis=-1)

        y = (project(ua, wa_ref, za0, za1)
             + project(silu_gated(ob[rows, :], gb[rows, :]), wb_ref, zb0, zb1)
             + project(silu_gated(oc[rows, :], gc[rows, :]), wc_ref, zc0, zc1))
        z = jnp.dot(y.astype(BF16), wo_ref[...], preferred_element_type=F32)
        ms = jnp.mean(z * z, axis=-1, keepdims=True)
        out_ref[rows, :] = x_ref[rows, :] + z * lax.rsqrt(ms + EPS) * pn_ref[...]


def _merge(x2d, proj, out_a16, out_b, out_c, wa, wb, wc, wo, post_norm, *, tm):
    rows, dm = x2d.shape
    batch, _, seq, _ = proj.shape
    per_b = seq // tm
    seg = tm // 16
    proj16 = proj.reshape(batch, N_STEPS, 16, seq // 16, WIDTH)

    def p16_spec(tile):
        return pl.BlockSpec((None, None, 16, seg, WIDTH),
                            lambda i: (i // per_b, tile, 0, i % per_b, 0))

    def nat_spec(tile):
        return pl.BlockSpec((None, None, tm, WIDTH), lambda i: (i // per_b, tile, i % per_b, 0))

    def rows_spec(width):
        return pl.BlockSpec((tm, width), lambda i: (i, 0))

    def full(shape):
        return pl.BlockSpec(shape, lambda i: (0, 0), pipeline_mode=pl.Buffered(1))

    return pl.pallas_call(
        functools.partial(_merge_kernel, tm=tm),
        out_shape=jax.ShapeDtypeStruct((rows, dm), F32),
        grid=(rows // tm,),
        in_specs=[pl.BlockSpec((None, 16, seg, WIDTH), lambda i: (i // per_b, 0, i % per_b, 0)),
                  p16_spec(T_GATE_A), rows_spec(WIDTH), rows_spec(WIDTH),
                  nat_spec(T_GATE_B), nat_spec(T_GATE_B + 1),
                  nat_spec(T_MERGE_A), nat_spec(T_MERGE_A + 1),
                  nat_spec(T_MERGE_B), nat_spec(T_MERGE_B + 1),
                  nat_spec(T_MERGE_B + 2), nat_spec(T_MERGE_B + 3),
                  full(wa.shape), full(wb.shape), full(wc.shape),
                  full(wo.shape), full(post_norm.shape), rows_spec(dm)],
        out_specs=pl.BlockSpec((tm, dm), lambda i: (i, 0)),
        scratch_shapes=[pltpu.VMEM((tm // MERGE_ROWS, WIDTH // LANES, MERGE_ROWS, LANES), F32)]
        + [pltpu.VMEM(w.shape, BF16) for w in (wa, wb, wc, wo)],
        compiler_params=pltpu.CompilerParams(
            dimension_semantics=("arbitrary",), vmem_limit_bytes=VMEM_LIMIT),
        name="merge",
    )(out_a16, proj16, out_b, out_c, proj, proj, proj, proj, proj, proj, proj, proj,
      wa, wb, wc, wo, post_norm, x2d)


def kernel(x, mem, pre_norm, w_in, merge_bias, na_rpb, mem_norm, w_mem_kv,
           w_branch_a, w_branch_b, w_branch_c, w_out, post_norm):
    b, s, dm = x.shape
    depth = pre_norm.shape[0]
    reach = (DIL_CONFIGS[0][0] // 2) // DIL_CONFIGS[0][1]
    assert all((w // 2) // d == reach for w, d in DIL_CONFIGS) and 2 * reach == QBLK
    assert tuple(d for _, d in DIL_CONFIGS) == SLOT_DIL
    cos_t, sin_t = _rope_tables(s)
    for layer in range(depth):
        x2d = x.reshape(b * s, dm)
        proj = _proj(x2d, pre_norm[layer][None], w_in[layer], merge_bias[layer], cos_t, sin_t,
                     batch=b, seq=s)

        out_a16 = _dil_attn(proj, reach=reach)
        out_b = _na_attn(proj, na_rpb[layer])

        mem2d = mem.reshape(b * mem.shape[1], dm)
        kv_m = _kv_proj(mem2d, mem_norm[layer][None], w_mem_kv[layer], tm=1024)
        out_c = _mem_attn(proj, kv_m.reshape(b, mem.shape[1], -1))

        y = _merge(x2d, proj, out_a16, out_b.reshape(b * s, WIDTH), out_c.reshape(b * s, WIDTH),
                   w_branch_a[layer], w_branch_b[layer], w_branch_c[layer],
                   w_out[layer], post_norm[layer][None], tm=2 * MERGE_ROWS)
        x = y.reshape(b, s, dm)
    return x
```

```python
import functools
import math

import jax
import jax.numpy as jnp
import numpy as np
from jax import lax
from jax.experimental import pallas as pl
from jax.experimental.pallas import tpu as pltpu

D_MODEL = 1024
HEAD_DIM = 64
DIL_CONFIGS = ((128, 1), (512, 4), (2048, 16))
WIDTH = 512
NA_ROWS = 8
NA_COLS = 16
GRID_W = 64
MEM_HEADS = 4
MEM_HEAD_DIM = 128
ROPE_THETA = 500000.0
ROPE_DIM = HEAD_DIM // 4
ROPE_HALF = ROPE_DIM // 2
EPS = 1e-6
NEG = -1e30
LOG2E = math.log2(math.e)

LANES = 128
QBLK = 128
CHUNK = 4
PROJ_ROWS = 256
MERGE_ROWS = 256
VMEM_LIMIT = 56 * 1024 * 1024

_QS = HEAD_DIM ** -0.5 * LOG2E
_MS = MEM_HEAD_DIM ** -0.5 * LOG2E
EPI_ROPE, EPI_SCALE, EPI_SIGMOID, EPI_SILU = range(4)
STEPS = (
    (0, 0, EPI_ROPE, _QS), (1, 0, EPI_ROPE, 1.0), (2, 0, EPI_SCALE, 1.0),
    (9, 0, EPI_SCALE, _QS), (10, 0, EPI_SCALE, 1.0), (11, 0, EPI_SCALE, 1.0),
    (12, 0, EPI_SCALE, _MS),
    (14, 0, EPI_SILU, 1.0), (15, 0, EPI_SILU, 1.0),
    (18, 0, EPI_SIGMOID, 1.0), (19, 0, EPI_SIGMOID, 1.0),
    (20, 0, EPI_SIGMOID, 1.0), (21, 0, EPI_SIGMOID, 1.0),
    (16, 0, EPI_SIGMOID, 1.0), (17, 0, EPI_SIGMOID, 1.0),
    (3, 1, EPI_ROPE, _QS), (4, 1, EPI_ROPE, 1.0), (5, 1, EPI_SCALE, 1.0),
    (6, 2, EPI_ROPE, _QS), (7, 2, EPI_ROPE, 1.0), (8, 2, EPI_SCALE, 1.0),
    (13, 2, EPI_SILU, 1.0),
)
FIRST_MERGE_TILE = 16
N_STEPS = len(STEPS)
T_G0, T_NA, T_MEMQ, T_GATE_B, T_MERGE_B, T_MERGE_A = 0, 3, 6, 7, 9, 13
T_G1, T_G2, T_GATE_A = 15, 18, 21
SLOT_DIL = (1, 4, 16)

BF16 = jnp.bfloat16
F32 = jnp.float32


def _sigmoid(z):
    return 0.5 * jnp.tanh(0.5 * z) + 0.5


def _proj_kernel(wtile_ref, epi_ref, scale_ref, x_ref, g_ref, w_ref, mb_ref,
                 cos_ref, sin_ref, o_ref, h_ref, wb_ref, *, tm):
    j = pl.program_id(1)
    n_lane_tiles = x_ref.shape[1] // LANES

    @pl.when(j == 0)
    def _():
        rb = 256
        ssq = [jnp.sum(jnp.square(x_ref[r0:r0 + rb, :]), axis=-1, keepdims=True)
               for r0 in range(0, tm, rb)]
        rs = lax.rsqrt(jnp.concatenate(ssq, axis=0) * (1.0 / x_ref.shape[1]) + EPS)
        for c in range(n_lane_tiles):
            cs = slice(c * LANES, (c + 1) * LANES)
            hf = x_ref[:, cs] * rs * g_ref[:, cs]
            h_ref[0, :, cs] = hf.astype(BF16)
            for slot, d in ((1, 4), (2, 16)):
                by_class = jnp.swapaxes(hf.reshape(tm // d, d, LANES), 0, 1)
                h_ref[slot, :, cs] = by_class.reshape(tm, LANES).astype(BF16)

    slot = (j >= T_G1).astype(jnp.int32) + (j >= T_G2).astype(jnp.int32)
    q_scale = scale_ref[j]

    def step(epi):
        wb_ref[...] = w_ref[...].astype(BF16)
        lane = lax.broadcasted_iota(jnp.int32, (1, LANES), 1)
        first_half = lane % HEAD_DIM < ROPE_HALF
        if epi == EPI_SIGMOID:
            bias = mb_ref[pl.ds(jnp.maximum(wtile_ref[j] - FIRST_MERGE_TILE, 0), 1), :]
        for r0 in range(0, tm, PROJ_ROWS):
            rows = pl.ds(r0, PROJ_ROWS)
            acc = jnp.dot(h_ref[slot, rows, :], wb_ref[...], preferred_element_type=F32)
            if epi == EPI_ROPE:
                c, s = cos_ref[rows, :], sin_ref[rows, :]
            for t in range(WIDTH // LANES):
                lanes = slice(t * LANES, (t + 1) * LANES)
                a = acc[:, lanes]
                if epi == EPI_ROPE:
                    partner = jnp.where(first_half, pltpu.roll(a, LANES - ROPE_HALF, 1),
                                        pltpu.roll(a, ROPE_HALF, 1))
                    a = (a * c + partner * s) * q_scale
                elif epi == EPI_SCALE:
                    a = a * q_scale
                elif epi == EPI_SIGMOID:
                    a = _sigmoid(a + bias[:, lanes])
                else:
                    a = a * _sigmoid(a)
                o_ref[rows, lanes] = a.astype(BF16)

    for epi in (EPI_ROPE, EPI_SCALE, EPI_SIGMOID, EPI_SILU):
        pl.when(epi_ref[j] == epi)(functools.partial(step, epi))


def _proj(x2d, gain, w, merge_bias, cos_t, sin_t, *, batch, seq):
    rows, dm = x2d.shape
    tm = seq
    prefetch = (jnp.asarray([st[0] for st in STEPS], jnp.int32),
                jnp.asarray([st[2] for st in STEPS], jnp.int32),
                jnp.asarray([st[3] for st in STEPS], F32))
    mb_rows = merge_bias.reshape(-1, WIDTH)

    def slot_of(j):
        return (j >= T_G1).astype(jnp.int32) + (j >= T_G2).astype(jnp.int32)

    tab_spec = pl.BlockSpec((None, tm, LANES), lambda i, j, *_: (slot_of(j), 0, 0))
    return pl.pallas_call(
        functools.partial(_proj_kernel, tm=tm),
        out_shape=jax.ShapeDtypeStruct((batch, N_STEPS, seq, WIDTH), BF16),
        grid_spec=pltpu.PrefetchScalarGridSpec(
            num_scalar_prefetch=len(prefetch),
            grid=(rows // tm, N_STEPS),
            in_specs=[
                pl.BlockSpec((tm, dm), lambda i, j, *_: (i, 0)),
                pl.BlockSpec((1, dm), lambda i, j, *_: (0, 0)),
                pl.BlockSpec((dm, WIDTH), lambda i, j, wt, ep, sc: (0, wt[j])),
                pl.BlockSpec(mb_rows.shape, lambda i, j, *_: (0, 0)),
                tab_spec, tab_spec,
            ],
            out_specs=pl.BlockSpec((None, None, tm, WIDTH), lambda i, j, *_: (i, j, 0, 0)),
            scratch_shapes=[pltpu.VMEM((3, tm, dm), BF16), pltpu.VMEM((dm, WIDTH), BF16)]),
        compiler_params=pltpu.CompilerParams(
            dimension_semantics=("arbitrary", "arbitrary"),
            vmem_limit_bytes=VMEM_LIMIT),
        name="proj",
    )(*prefetch, x2d, gain, w, mb_rows, cos_t, sin_t)


def _rope_tables(seq):
    pos = np.arange(seq, dtype=np.float64)
    inv = ROPE_THETA ** (-np.arange(ROPE_HALF, dtype=np.float64) * 2.0 / ROPE_DIM)
    ang = pos[:, None] * inv[None, :]
    cos, sin = np.cos(ang), np.sin(ang)
    rest = HEAD_DIM - ROPE_DIM
    c = np.concatenate([cos, cos, np.ones((seq, rest))] * (LANES // HEAD_DIM), axis=-1)
    s = np.concatenate([-sin, sin, np.zeros((seq, rest))] * (LANES // HEAD_DIM), axis=-1)

    def orders(t):
        out = [t]
        for d in SLOT_DIL[1:]:
            out.append(t.reshape(seq // d, d, LANES).transpose(1, 0, 2).reshape(seq, LANES))
        return jnp.asarray(np.stack(out), F32)

    return orders(c), orders(s)


def _kv_kernel(x_ref, g_ref, w_ref, o_ref):
    xf = x_ref[...]
    ms = jnp.mean(xf * xf, axis=-1, keepdims=True)
    h = (xf * lax.rsqrt(ms + EPS) * g_ref[...]).astype(BF16)
    o_ref[...] = jnp.dot(h, w_ref[...].astype(BF16), preferred_element_type=F32).astype(BF16)


def _kv_proj(mem2d, gain, w, *, tm):
    rows, dm = mem2d.shape
    n_out = w.shape[1]
    return pl.pallas_call(
        _kv_kernel,
        out_shape=jax.ShapeDtypeStruct((rows, n_out), BF16),
        grid=(rows // tm,),
        in_specs=[pl.BlockSpec((tm, dm), lambda i: (i, 0)),
                  pl.BlockSpec((1, dm), lambda i: (0, 0)),
                  pl.BlockSpec((dm, n_out), lambda i: (0, 0))],
        out_specs=pl.BlockSpec((tm, n_out), lambda i: (i, 0)),
        compiler_params=pltpu.CompilerParams(
            dimension_semantics=("arbitrary",), vmem_limit_bytes=VMEM_LIMIT),
        name="kv_proj",
    )(mem2d, gain, w)


def _dil_kernel(q0, k0, v0, q1, k1, v1, q2, k2, v2, o_ref,
                og0, lg0, og1, lg1, bias_scr, s_scr, p_scr, *, seq, reach):
    left = lax.broadcasted_iota(jnp.int32, (1, LANES), 1) < HEAD_DIM
    keep_a = jnp.where(left, 1.0, 0.0).astype(BF16)
    keep_b = jnp.where(left, 0.0, 1.0).astype(BF16)
    kw_max = 2 * QBLK

    @pl.when(jnp.logical_and(pl.program_id(0) == 0, pl.program_id(1) == 0))
    def _():
        rel = (lax.broadcasted_iota(jnp.int32, (QBLK, kw_max), 0)
               - lax.broadcasted_iota(jnp.int32, (QBLK, kw_max), 1))
        for n in range(3):
            bias_scr[n] = jnp.where(jnp.abs(rel + n * reach) <= reach, 0.0, NEG)

    def pair_tile(col_a, col_b):
        return jnp.where(left, jnp.broadcast_to(col_a, (QBLK, LANES)),
                         jnp.broadcast_to(col_b, (QBLK, LANES)))

    def keep(o_scr, l_scr, length):
        def finish(r, row0, acc, m, den):
            rows = pl.ds(r * length + row0, QBLK)
            o_scr[rows, :] = acc / den
            l_scr[rows, :] = m + jnp.log2(den)
        return finish

    def combine(r, row0, acc2, m2, den2):
        rows0 = pl.ds(r, QBLK, stride=16)
        rows1 = pl.ds((r % 4) * (seq // 4) + r // 4, QBLK, stride=4)
        l0, l1, l2 = lg0[rows0, :], lg1[rows1, :], m2 + jnp.log2(den2)
        m = jnp.maximum(jnp.maximum(l0, l1), l2)
        w0, w1, w2 = jnp.exp2(l0 - m), jnp.exp2(l1 - m), jnp.exp2(l2 - m)
        num = w0 * og0[rows0, :] + w1 * og1[rows1, :] + w2 * (acc2 / den2)
        o_ref[r] = (num / (w0 + w1 + w2)).astype(BF16)

    tiles = []
    for (q_ref, k_ref, v_ref), d, finish in (
            ((q0, k0, v0), 1, keep(og0, lg0, seq)),
            ((q1, k1, v1), 4, keep(og1, lg1, seq // 4)),
            ((q2, k2, v2), 16, combine)):
        length = seq // d
        kw = min(kw_max, length)
        for bi in range(seq // QBLK):
            r, row0 = divmod(bi * QBLK, length)
            ks = min(max(row0 - reach, 0), length - kw)
            tiles.append((q_ref, k_ref, v_ref, r, row0, ks, kw, finish))
    chunks = [tiles[i:i + CHUNK] for i in range(0, len(tiles), CHUNK)]

    def scores(c):
        for t, (q_ref, k_ref, _, r, row0, ks, kw, _) in enumerate(chunks[c]):
            q2d = q_ref[r, row0:row0 + QBLK, :]
            qq = jnp.concatenate([q2d * keep_a, q2d * keep_b], axis=0)
            s_scr[c % 2, t, :, :kw] = lax.dot_general(
                qq, k_ref[r, ks:ks + kw, :], (((1,), (1,)), ((), ())),
                preferred_element_type=F32)

    def softmax_pv(c):
        for t, (_, _, v_ref, r, row0, ks, kw, finish) in enumerate(chunks[c]):
            bias = bias_scr[(row0 - ks) // reach, :, :kw]
            s = s_scr[c % 2, t, :, :kw] + jnp.concatenate([bias, bias], axis=0)
            m = jnp.max(s, axis=-1, keepdims=True)
            p = jnp.exp2(s - m)
            den = jnp.sum(p, axis=-1, keepdims=True)
            p_scr[c % 2, t, :, :kw] = p.astype(BF16)
            pv = jnp.dot(p_scr[c % 2, t, :, :kw], v_ref[r, ks:ks + kw, :],
                         preferred_element_type=F32)
            finish(r, row0, jnp.where(left, pv[:QBLK], pv[QBLK:]),
                   pair_tile(m[:QBLK], m[QBLK:]), pair_tile(den[:QBLK], den[QBLK:]))

    scores(0)
    for c in range(len(chunks)):
        if c + 1 < len(chunks):
            scores(c + 1)
        softmax_pv(c)


def _dil_attn(proj, *, reach):
    batch, _, seq, _ = proj.shape
    n_hp = WIDTH // LANES

    def spec(first_tile, d, kind):
        return pl.BlockSpec((None, None, d, seq // d, LANES),
                            lambda b, hp: (b, first_tile + kind, 0, 0, hp))

    views = {d: proj.reshape(batch, N_STEPS, d, seq // d, WIDTH) for d in SLOT_DIL}
    f32_rows = pltpu.VMEM((seq, LANES), F32)
    return pl.pallas_call(
        functools.partial(_dil_kernel, seq=seq, reach=reach),
        out_shape=jax.ShapeDtypeStruct((batch, 16, seq // 16, WIDTH), BF16),
        grid=(batch, n_hp),
        in_specs=[spec(t0, d, kind) for t0, d in zip((T_G0, T_G1, T_G2), SLOT_DIL)
                  for kind in range(3)],
        out_specs=pl.BlockSpec((None, 16, seq // 16, LANES), lambda b, hp: (b, 0, 0, hp)),
        scratch_shapes=[f32_rows] * 4 + [
            pltpu.VMEM((3, QBLK, 2 * QBLK), F32),
            pltpu.VMEM((2, CHUNK, 2 * QBLK, 2 * QBLK), F32),
            pltpu.VMEM((2, CHUNK, 2 * QBLK, 2 * QBLK), BF16)],
        compiler_params=pltpu.CompilerParams(
            dimension_semantics=("arbitrary", "arbitrary"),
            vmem_limit_bytes=VMEM_LIMIT),
        name="dil_attn",
    )(*[views[d] for d in SLOT_DIL for _ in range(3)])


def _na_kernel(q_ref, k_ref, v_ref, rpb_ref, qm_ref, km_ref, vm_ref,
               o_ref, om_ref, bias_scr, s_scr, p_scr, sm_scr, pm_scr, rdm_scr, *, rows):
    kr = min(NA_ROWS, rows)
    nk = kr * GRID_W
    n_var = 2 * NA_ROWS - kr
    left = lax.broadcasted_iota(jnp.int32, (1, LANES), 1) < HEAD_DIM
    keep_a = jnp.where(left, 1.0, 0.0).astype(BF16)
    keep_b = jnp.where(left, 0.0, 1.0).astype(BF16)

    @pl.when(pl.program_id(1) == 0)
    def _():
        qc = lax.broadcasted_iota(jnp.int32, (GRID_W, LANES), 0)
        kc = lax.broadcasted_iota(jnp.int32, (GRID_W, LANES), 1) % GRID_W
        rel = jnp.clip(kc - qc, -(NA_COLS - 1), NA_COLS - 1) + NA_COLS - 1
        c_start = jnp.clip(qc - NA_COLS // 2, 0, GRID_W - NA_COLS)
        valid = jnp.logical_and(kc >= c_start, kc < c_start + NA_COLS)
        for hh in range(LANES // HEAD_DIM):
            head = pl.program_id(0) * (LANES // HEAD_DIM) + hh
            for a in range(2 * NA_ROWS - 1):
                row = rpb_ref[pl.ds(head * (2 * NA_ROWS - 1) + a, 1), :]
                blk = jnp.take_along_axis(jnp.broadcast_to(row, (GRID_W, LANES)), rel, axis=1)
                blk = jnp.where(valid, blk * LOG2E, NEG)
                for j in range(kr):
                    if 0 <= a - j < n_var:
                        half = slice((j % 2) * GRID_W, (j % 2 + 1) * GRID_W)
                        bias_scr[hh, a - j, :, j * GRID_W:(j + 1) * GRID_W] = blk[:, half]

    def window(r):
        r_start = min(max(r - kr // 2, 0), rows - kr)
        return r * GRID_W, r_start * GRID_W, r_start - r + (NA_ROWS - 1)

    def scores(c):
        for t in range(CHUNK):
            row0, ks, _ = window(c * CHUNK + t)
            q2d = q_ref[row0:row0 + GRID_W, :]
            qq = jnp.concatenate([q2d * keep_a, q2d * keep_b], axis=0)
            s_scr[c % 2, t] = lax.dot_general(qq, k_ref[ks:ks + nk, :], (((1,), (1,)), ((), ())),
                                              preferred_element_type=F32)

    def softmax_pv(c):
        for t in range(CHUNK):
            row0, ks, variant = window(c * CHUNK + t)
            dens = []
            for hh in range(LANES // HEAD_DIM):
                rows_h = slice(hh * GRID_W, (hh + 1) * GRID_W)
                s = s_scr[c % 2, t, rows_h, :] + bias_scr[hh, variant]
                m = jnp.max(s, axis=-1, keepdims=True)
                p = jnp.exp2(s - m)
                dens.append(jnp.sum(p, axis=-1, keepdims=True))
                p_scr[t, rows_h, :] = p.astype(BF16)
            pv = jnp.dot(p_scr[t], v_ref[ks:ks + nk, :], preferred_element_type=F32)
            den_t = jnp.where(left, jnp.broadcast_to(dens[0], (GRID_W, LANES)),
                              jnp.broadcast_to(dens[1], (GRID_W, LANES)))
            o_ref[row0:row0 + GRID_W, :] = (
                jnp.where(left, pv[:GRID_W], pv[GRID_W:]) / den_t).astype(BF16)

    sm_scr[...] = lax.dot_general(qm_ref[...], km_ref[...], (((1,), (1,)), ((), ())),
                                  preferred_element_type=F32)

    def mem_softmax(i):
        rows_i = slice(i * QBLK, (i + 1) * QBLK)
        s = sm_scr[rows_i, :]
        m = jnp.max(s, axis=-1, keepdims=True)
        p = jnp.exp2(s - m)
        den = jnp.sum(p, axis=-1, keepdims=True)
        pm_scr[rows_i, :] = p.astype(BF16)
        rdm_scr[rows_i, :] = jnp.broadcast_to(1.0 / den, (QBLK, LANES))

    n_chunks = rows // CHUNK
    mem_tiles = qm_ref.shape[0] // QBLK
    scores(0)
    for c in range(n_chunks):
        if c + 1 < n_chunks:
            scores(c + 1)
        softmax_pv(c)
        for i in range(c * mem_tiles // n_chunks, (c + 1) * mem_tiles // n_chunks):
            mem_softmax(i)
    pv = jnp.dot(pm_scr[...], vm_ref[...], preferred_element_type=F32)
    om_ref[...] = (pv * rdm_scr[...]).astype(BF16)


def _na_mem_attn(proj, rpb, kv_m):
    batch, _, seq, _ = proj.shape
    rows = seq // GRID_W
    kr = min(NA_ROWS, rows)
    nk = kr * GRID_W
    n_hp = WIDTH // LANES
    assert rpb.shape == (WIDTH // HEAD_DIM, 2 * NA_ROWS - 1, 2 * NA_COLS - 1)

    def in_spec(tile):
        return pl.BlockSpec((None, None, seq, LANES), lambda hp, b: (b, tile, 0, hp))

    rpb_rows = jnp.pad(rpb.reshape(-1, rpb.shape[-1]).astype(F32),
                       ((0, 0), (0, LANES - rpb.shape[-1])))
    mem_len = kv_m.shape[1]
    assert MEM_HEADS == n_hp and MEM_HEAD_DIM == LANES
    out_spec = pl.BlockSpec((None, seq, LANES), lambda hp, b: (b, 0, hp))
    return pl.pallas_call(
        functools.partial(_na_kernel, rows=rows),
        out_shape=(jax.ShapeDtypeStruct((batch, seq, WIDTH), BF16),
                   jax.ShapeDtypeStruct((batch, seq, WIDTH), BF16)),
        grid=(n_hp, batch),
        in_specs=[in_spec(T_NA), in_spec(T_NA + 1), in_spec(T_NA + 2),
                  pl.BlockSpec(rpb_rows.shape, lambda hp, b: (0, 0)),
                  in_spec(T_MEMQ),
                  pl.BlockSpec((None, mem_len, LANES), lambda hp, b: (b, 0, hp)),
                  pl.BlockSpec((None, mem_len, LANES), lambda hp, b: (b, 0, MEM_HEADS + hp))],
        out_specs=(out_spec, out_spec),
        scratch_shapes=[pltpu.VMEM((LANES // HEAD_DIM, 2 * NA_ROWS - kr, GRID_W, nk), F32),
                        pltpu.VMEM((2, CHUNK, 2 * GRID_W, nk), F32),
                        pltpu.VMEM((CHUNK, 2 * GRID_W, nk), BF16),
                        pltpu.VMEM((seq, mem_len), F32), pltpu.VMEM((seq, mem_len), BF16),
                        pltpu.VMEM((seq, LANES), F32)],
        compiler_params=pltpu.CompilerParams(
            dimension_semantics=("arbitrary", "arbitrary"), vmem_limit_bytes=VMEM_LIMIT),
        name="na_mem_attn",
    )(proj, proj, proj, rpb_rows, proj, kv_m, kv_m)


def _merge_kernel(oa, ga, ob, oc, gb, gc, za0, za1, zb0, zb1, zc0, zc1,
                  wa_f32, wb_f32, wc_f32, wo_f32, pn_ref, x_ref, out_ref,
                  u_scr, wa_ref, wb_ref, wc_ref, wo_ref, *, tm):
    @pl.when(pl.program_id(0) == 0)
    def _():
        for src, dst in ((wa_f32, wa_ref), (wb_f32, wb_ref), (wc_f32, wc_ref), (wo_f32, wo_ref)):
            dst[...] = src[...].astype(BF16)

    def silu_gated(o, silu_g):
        return o.astype(F32) * silu_g.astype(F32)

    seg = MERGE_ROWS // 16
    for blk in range(tm // MERGE_ROWS):
        rows = slice(blk * MERGE_ROWS, (blk + 1) * MERGE_ROWS)

        def project(u, w_ref, gate_lo, gate_hi):
            yb = jnp.dot(u.astype(BF16), w_ref[...], preferred_element_type=F32)
            gate = jnp.concatenate([gate_lo[rows, :], gate_hi[rows, :]], axis=-1)
            return gate.astype(F32) * yb

        p16_rows = slice(blk * seg, (blk + 1) * seg)
        ua = silu_gated(oa[:, p16_rows, :].reshape(MERGE_ROWS, WIDTH),
                        ga[:, p16_rows, :].reshape(MERGE_ROWS, WIDTH))
        for r in range(16):
            for c in range(WIDTH // LANES):
                u_scr[blk, c, pl.ds(r, seg, stride=16), :] = ua[r * seg:(r + 1) * seg,
                                                                c * LANES:(c + 1) * LANES]
        ua = jnp.concatenate([u_scr[blk, c] for c in range(WIDTH // LANES)], axis=-1)

        y = (project(ua, wa_ref, za0, za1)
             + project(silu_gated(ob[rows, :], gb[rows, :]), wb_ref, zb0, zb1)
             + project(silu_gated(oc[rows, :], gc[rows, :]), wc_ref, zc0, zc1))
        z = jnp.dot(y.astype(BF16), wo_ref[...], preferred_element_type=F32)
        ms = jnp.mean(z * z, axis=-1, keepdims=True)
        out_ref[rows, :] = x_ref[rows, :] + z * lax.rsqrt(ms + EPS) * pn_ref[...]


def _merge(x2d, proj, out_a16, out_b, out_c, wa, wb, wc, wo, post_norm, *, tm):
    rows, dm = x2d.shape
    batch, _, seq, _ = proj.shape
    per_b = seq // tm
    seg = tm // 16
    proj16 = proj.reshape(batch, N_STEPS, 16, seq // 16, WIDTH)

    def p16_spec(tile):
        return pl.BlockSpec((None, None, 16, seg, WIDTH),
                            lambda i: (i // per_b, tile, 0, i % per_b, 0))

    def nat_spec(tile):
        return pl.BlockSpec((None, None, tm, WIDTH), lambda i: (i // per_b, tile, i % per_b, 0))

    def rows_spec(width):
        return pl.BlockSpec((tm, width), lambda i: (i, 0))

    def full(shape):
        return pl.BlockSpec(shape, lambda i: (0, 0), pipeline_mode=pl.Buffered(1))

    return pl.pallas_call(
        functools.partial(_merge_kernel, tm=tm),
        out_shape=jax.ShapeDtypeStruct((rows, dm), F32),
        grid=(rows // tm,),
        in_specs=[pl.BlockSpec((None, 16, seg, WIDTH), lambda i: (i // per_b, 0, i % per_b, 0)),
                  p16_spec(T_GATE_A), rows_spec(WIDTH), rows_spec(WIDTH),
                  nat_spec(T_GATE_B), nat_spec(T_GATE_B + 1),
                  nat_spec(T_MERGE_A), nat_spec(T_MERGE_A + 1),
                  nat_spec(T_MERGE_B), nat_spec(T_MERGE_B + 1),
                  nat_spec(T_MERGE_B + 2), nat_spec(T_MERGE_B + 3),
                  full(wa.shape), full(wb.shape), full(wc.shape),
                  full(wo.shape), full(post_norm.shape), rows_spec(dm)],
        out_specs=pl.BlockSpec((tm, dm), lambda i: (i, 0)),
        scratch_shapes=[pltpu.VMEM((tm // MERGE_ROWS, WIDTH // LANES, MERGE_ROWS, LANES), F32)]
        + [pltpu.VMEM(w.shape, BF16) for w in (wa, wb, wc, wo)],
        compiler_params=pltpu.CompilerParams(
            dimension_semantics=("arbitrary",), vmem_limit_bytes=VMEM_LIMIT),
        name="merge",
    )(out_a16, proj16, out_b, out_c, proj, proj, proj, proj, proj, proj, proj, proj,
      wa, wb, wc, wo, post_norm, x2d)


def kernel(x, mem, pre_norm, w_in, merge_bias, na_rpb, mem_norm, w_mem_kv,
           w_branch_a, w_branch_b, w_branch_c, w_out, post_norm):
    b, s, dm = x.shape
    depth = pre_norm.shape[0]
    reach = (DIL_CONFIGS[0][0] // 2) // DIL_CONFIGS[0][1]
    assert all((w // 2) // d == reach for w, d in DIL_CONFIGS) and 2 * reach == QBLK
    assert tuple(d for _, d in DIL_CONFIGS) == SLOT_DIL
    cos_t, sin_t = _rope_tables(s)
    for layer in range(depth):
        x2d = x.reshape(b * s, dm)
        proj = _proj(x2d, pre_norm[layer][None], w_in[layer], merge_bias[layer], cos_t, sin_t,
                     batch=b, seq=s)

        out_a16 = _dil_attn(proj, reach=reach)
        mem2d = mem.reshape(b * mem.shape[1], dm)
        kv_m = _kv_proj(mem2d, mem_norm[layer][None], w_mem_kv[layer], tm=1024)
        out_b, out_c = _na_mem_attn(proj, na_rpb[layer], kv_m.reshape(b, mem.shape[1], -1))

        y = _merge(x2d, proj, out_a16, out_b.reshape(b * s, WIDTH), out_c.reshape(b * s, WIDTH),
                   w_branch_a[layer], w_branch_b[layer], w_branch_c[layer],
                   w_out[layer], post_norm[layer][None], tm=2 * MERGE_ROWS)
        x = y.reshape(b, s, dm)
    return x
```

```python
import functools
import math

import jax
import jax.numpy as jnp
import numpy as np
from jax import lax
from jax.experimental import pallas as pl
from jax.experimental.pallas import tpu as pltpu

D_MODEL = 1024
HEAD_DIM = 64
DIL_CONFIGS = ((128, 1), (512, 4), (2048, 16))
WIDTH = 512
NA_ROWS = 8
NA_COLS = 16
GRID_W = 64
MEM_HEADS = 4
MEM_HEAD_DIM = 128
ROPE_THETA = 500000.0
ROPE_DIM = HEAD_DIM // 4
ROPE_HALF = ROPE_DIM // 2
EPS = 1e-6
NEG = -1e30
LOG2E = math.log2(math.e)

LANES = 128
QBLK = 128
DIL_CHUNK = 1
NA_CHUNK = 8
PROJ_ROWS = 256
MERGE_ROWS = 256
VMEM_LIMIT = 56 * 1024 * 1024

_QS = HEAD_DIM ** -0.5 * LOG2E
_MS = MEM_HEAD_DIM ** -0.5 * LOG2E
EPI_ROPE, EPI_SCALE, EPI_SIGMOID, EPI_SILU = range(4)
STEPS = (
    (0, 0, EPI_ROPE, _QS), (1, 0, EPI_ROPE, 1.0), (2, 0, EPI_SCALE, 1.0),
    (9, 0, EPI_SCALE, _QS), (10, 0, EPI_SCALE, 1.0), (11, 0, EPI_SCALE, 1.0),
    (12, 0, EPI_SCALE, _MS),
    (14, 0, EPI_SILU, 1.0), (15, 0, EPI_SILU, 1.0),
    (18, 0, EPI_SIGMOID, 1.0), (19, 0, EPI_SIGMOID, 1.0),
    (20, 0, EPI_SIGMOID, 1.0), (21, 0, EPI_SIGMOID, 1.0),
    (16, 0, EPI_SIGMOID, 1.0), (17, 0, EPI_SIGMOID, 1.0),
    (3, 1, EPI_ROPE, _QS), (4, 1, EPI_ROPE, 1.0), (5, 1, EPI_SCALE, 1.0),
    (6, 2, EPI_ROPE, _QS), (7, 2, EPI_ROPE, 1.0), (8, 2, EPI_SCALE, 1.0),
    (13, 2, EPI_SILU, 1.0),
)
FIRST_MERGE_TILE = 16
N_STEPS = len(STEPS)
T_G0, T_NA, T_MEMQ, T_GATE_B, T_MERGE_B, T_MERGE_A = 0, 3, 6, 7, 9, 13
T_G1, T_G2, T_GATE_A = 15, 18, 21
SLOT_DIL = (1, 4, 16)

BF16 = jnp.bfloat16
F32 = jnp.float32


def _sigmoid(z):
    return 0.5 * jnp.tanh(0.5 * z) + 0.5


def _proj_kernel(wtile_ref, epi_ref, scale_ref, x_ref, g_ref, w_ref, mb_ref,
                 cos_ref, sin_ref, o_ref, h_ref, wb_ref, *, tm):
    j = pl.program_id(1)
    n_lane_tiles = x_ref.shape[1] // LANES

    @pl.when(j == 0)
    def _():
        rb = 256
        ssq = [jnp.sum(jnp.square(x_ref[r0:r0 + rb, :]), axis=-1, keepdims=True)
               for r0 in range(0, tm, rb)]
        rs = lax.rsqrt(jnp.concatenate(ssq, axis=0) * (1.0 / x_ref.shape[1]) + EPS)
        for c in range(n_lane_tiles):
            cs = slice(c * LANES, (c + 1) * LANES)
            hf = x_ref[:, cs] * rs * g_ref[:, cs]
            h_ref[0, :, cs] = hf.astype(BF16)
            for slot, d in ((1, 4), (2, 16)):
                by_class = jnp.swapaxes(hf.reshape(tm // d, d, LANES), 0, 1)
                h_ref[slot, :, cs] = by_class.reshape(tm, LANES).astype(BF16)

    slot = (j >= T_G1).astype(jnp.int32) + (j >= T_G2).astype(jnp.int32)
    q_scale = scale_ref[j]

    def step(epi):
        wb_ref[...] = w_ref[...].astype(BF16)
        lane = lax.broadcasted_iota(jnp.int32, (1, LANES), 1)
        first_half = lane % HEAD_DIM < ROPE_HALF
        if epi == EPI_SIGMOID:
            bias = mb_ref[pl.ds(jnp.maximum(wtile_ref[j] - FIRST_MERGE_TILE, 0), 1), :]
        for r0 in range(0, tm, PROJ_ROWS):
            rows = pl.ds(r0, PROJ_ROWS)
            acc = jnp.dot(h_ref[slot, rows, :], wb_ref[...], preferred_element_type=F32)
            if epi == EPI_ROPE:
                c, s = cos_ref[rows, :], sin_ref[rows, :]
            for t in range(WIDTH // LANES):
                lanes = slice(t * LANES, (t + 1) * LANES)
                a = acc[:, lanes]
                if epi == EPI_ROPE:
                    partner = jnp.where(first_half, pltpu.roll(a, LANES - ROPE_HALF, 1),
                                        pltpu.roll(a, ROPE_HALF, 1))
                    a = (a * c + partner * s) * q_scale
                elif epi == EPI_SCALE:
                    a = a * q_scale
                elif epi == EPI_SIGMOID:
                    a = _sigmoid(a + bias[:, lanes])
                else:
                    a = a * _sigmoid(a)
                o_ref[rows, lanes] = a.astype(BF16)

    for epi in (EPI_ROPE, EPI_SCALE, EPI_SIGMOID, EPI_SILU):
        pl.when(epi_ref[j] == epi)(functools.partial(step, epi))


def _proj(x2d, gain, w, merge_bias, cos_t, sin_t, *, batch, seq):
    rows, dm = x2d.shape
    tm = seq
    prefetch = (jnp.asarray([st[0] for st in STEPS], jnp.int32),
                jnp.asarray([st[2] for st in STEPS], jnp.int32),
                jnp.asarray([st[3] for st in STEPS], F32))
    mb_rows = merge_bias.reshape(-1, WIDTH)

    def slot_of(j):
        return (j >= T_G1).astype(jnp.int32) + (j >= T_G2).astype(jnp.int32)

    tab_spec = pl.BlockSpec((None, tm, LANES), lambda i, j, *_: (slot_of(j), 0, 0))
    return pl.pallas_call(
        functools.partial(_proj_kernel, tm=tm),
        out_shape=jax.ShapeDtypeStruct((batch, N_STEPS, seq, WIDTH), BF16),
        grid_spec=pltpu.PrefetchScalarGridSpec(
            num_scalar_prefetch=len(prefetch),
            grid=(rows // tm, N_STEPS),
            in_specs=[
                pl.BlockSpec((tm, dm), lambda i, j, *_: (i, 0)),
                pl.BlockSpec((1, dm), lambda i, j, *_: (0, 0)),
                pl.BlockSpec((dm, WIDTH), lambda i, j, wt, ep, sc: (0, wt[j])),
                pl.BlockSpec(mb_rows.shape, lambda i, j, *_: (0, 0)),
                tab_spec, tab_spec,
            ],
            out_specs=pl.BlockSpec((None, None, tm, WIDTH), lambda i, j, *_: (i, j, 0, 0)),
            scratch_shapes=[pltpu.VMEM((3, tm, dm), BF16), pltpu.VMEM((dm, WIDTH), BF16)]),
        compiler_params=pltpu.CompilerParams(
            dimension_semantics=("arbitrary", "arbitrary"),
            vmem_limit_bytes=VMEM_LIMIT),
        name="proj",
    )(*prefetch, x2d, gain, w, mb_rows, cos_t, sin_t)


def _rope_tables(seq):
    pos = np.arange(seq, dtype=np.float64)
    inv = ROPE_THETA ** (-np.arange(ROPE_HALF, dtype=np.float64) * 2.0 / ROPE_DIM)
    ang = pos[:, None] * inv[None, :]
    cos, sin = np.cos(ang), np.sin(ang)
    rest = HEAD_DIM - ROPE_DIM
    c = np.concatenate([cos, cos, np.ones((seq, rest))] * (LANES // HEAD_DIM), axis=-1)
    s = np.concatenate([-sin, sin, np.zeros((seq, rest))] * (LANES // HEAD_DIM), axis=-1)

    def orders(t):
        out = [t]
        for d in SLOT_DIL[1:]:
            out.append(t.reshape(seq // d, d, LANES).transpose(1, 0, 2).reshape(seq, LANES))
        return jnp.asarray(np.stack(out), F32)

    return orders(c), orders(s)


def _kv_kernel(x_ref, g_ref, w_ref, o_ref):
    xf = x_ref[...]
    ms = jnp.mean(xf * xf, axis=-1, keepdims=True)
    h = (xf * lax.rsqrt(ms + EPS) * g_ref[...]).astype(BF16)
    o_ref[...] = jnp.dot(h, w_ref[...].astype(BF16), preferred_element_type=F32).astype(BF16)


def _kv_proj(mem2d, gain, w, *, tm):
    rows, dm = mem2d.shape
    n_out = w.shape[1]
    return pl.pallas_call(
        _kv_kernel,
        out_shape=jax.ShapeDtypeStruct((rows, n_out), BF16),
        grid=(rows // tm,),
        in_specs=[pl.BlockSpec((tm, dm), lambda i: (i, 0)),
                  pl.BlockSpec((1, dm), lambda i: (0, 0)),
                  pl.BlockSpec((dm, n_out), lambda i: (0, 0))],
        out_specs=pl.BlockSpec((tm, n_out), lambda i: (i, 0)),
        compiler_params=pltpu.CompilerParams(
            dimension_semantics=("arbitrary",), vmem_limit_bytes=VMEM_LIMIT),
        name="kv_proj",
    )(mem2d, gain, w)


def _dil_kernel(q0, k0, v0, q1, k1, v1, q2, k2, v2, o_ref,
                og0, lg0, og1, lg1, bias_scr, s_scr, p_scr, *, seq, reach):
    left = lax.broadcasted_iota(jnp.int32, (1, LANES), 1) < HEAD_DIM
    keep_a = jnp.where(left, 1.0, 0.0).astype(BF16)
    keep_b = jnp.where(left, 0.0, 1.0).astype(BF16)
    kw_max = 2 * QBLK

    @pl.when(jnp.logical_and(pl.program_id(0) == 0, pl.program_id(1) == 0))
    def _():
        rel = (lax.broadcasted_iota(jnp.int32, (QBLK, kw_max), 0)
               - lax.broadcasted_iota(jnp.int32, (QBLK, kw_max), 1))
        for n in range(3):
            bias_scr[n] = jnp.where(jnp.abs(rel + n * reach) <= reach, 0.0, NEG)

    def pair_tile(col_a, col_b):
        return jnp.where(left, jnp.broadcast_to(col_a, (QBLK, LANES)),
                         jnp.broadcast_to(col_b, (QBLK, LANES)))

    def keep(o_scr, l_scr, length):
        def finish(r, row0, acc, m, den):
            rows = pl.ds(r * length + row0, QBLK)
            o_scr[rows, :] = acc / den
            l_scr[rows, :] = m + jnp.log2(den)
        return finish

    def combine(r, row0, acc2, m2, den2):
        rows0 = pl.ds(r, QBLK, stride=16)
        rows1 = pl.ds((r % 4) * (seq // 4) + r // 4, QBLK, stride=4)
        l0, l1, l2 = lg0[rows0, :], lg1[rows1, :], m2 + jnp.log2(den2)
        m = jnp.maximum(jnp.maximum(l0, l1), l2)
        w0, w1, w2 = jnp.exp2(l0 - m), jnp.exp2(l1 - m), jnp.exp2(l2 - m)
        num = w0 * og0[rows0, :] + w1 * og1[rows1, :] + w2 * (acc2 / den2)
        o_ref[r] = (num / (w0 + w1 + w2)).astype(BF16)

    tiles = []
    for (q_ref, k_ref, v_ref), d, finish in (
            ((q0, k0, v0), 1, keep(og0, lg0, seq)),
            ((q1, k1, v1), 4, keep(og1, lg1, seq // 4)),
            ((q2, k2, v2), 16, combine)):
        length = seq // d
        kw = min(kw_max, length)
        for bi in range(seq // QBLK):
            r, row0 = divmod(bi * QBLK, length)
            ks = min(max(row0 - reach, 0), length - kw)
            tiles.append((q_ref, k_ref, v_ref, r, row0, ks, kw, finish))
    chunks = [tiles[i:i + DIL_CHUNK] for i in range(0, len(tiles), DIL_CHUNK)]

    def scores(c):
        for t, (q_ref, k_ref, _, r, row0, ks, kw, _) in enumerate(chunks[c]):
            q2d = q_ref[r, row0:row0 + QBLK, :]
            qq = jnp.concatenate([q2d * keep_a, q2d * keep_b], axis=0)
            s_scr[c % 2, t, :, :kw] = lax.dot_general(
                qq, k_ref[r, ks:ks + kw, :], (((1,), (1,)), ((), ())),
                preferred_element_type=F32)

    def softmax_pv(c):
        for t, (_, _, v_ref, r, row0, ks, kw, finish) in enumerate(chunks[c]):
            bias = bias_scr[(row0 - ks) // reach, :, :kw]
            s = s_scr[c % 2, t, :, :kw] + jnp.concatenate([bias, bias], axis=0)
            m = jnp.max(s, axis=-1, keepdims=True)
            p = jnp.exp2(s - m)
            den = jnp.sum(p, axis=-1, keepdims=True)
            p_scr[c % 2, t, :, :kw] = p.astype(BF16)
            pv = jnp.dot(p_scr[c % 2, t, :, :kw], v_ref[r, ks:ks + kw, :],
                         preferred_element_type=F32)
            finish(r, row0, jnp.where(left, pv[:QBLK], pv[QBLK:]),
                   pair_tile(m[:QBLK], m[QBLK:]), pair_tile(den[:QBLK], den[QBLK:]))

    scores(0)
    for c in range(len(chunks)):
        if c + 1 < len(chunks):
            scores(c + 1)
        softmax_pv(c)


def _dil_attn(proj, *, reach):
    batch, _, seq, _ = proj.shape
    n_hp = WIDTH // LANES

    def spec(first_tile, d, kind):
        return pl.BlockSpec((None, None, d, seq // d, LANES),
                            lambda b, hp: (b, first_tile + kind, 0, 0, hp))

    views = {d: proj.reshape(batch, N_STEPS, d, seq // d, WIDTH) for d in SLOT_DIL}
    f32_rows = pltpu.VMEM((seq, LANES), F32)
    return pl.pallas_call(
        functools.partial(_dil_kernel, seq=seq, reach=reach),
        out_shape=jax.ShapeDtypeStruct((batch, 16, seq // 16, WIDTH), BF16),
        grid=(batch, n_hp),
        in_specs=[spec(t0, d, kind) for t0, d in zip((T_G0, T_G1, T_G2), SLOT_DIL)
                  for kind in range(3)],
        out_specs=pl.BlockSpec((None, 16, seq // 16, LANES), lambda b, hp: (b, 0, 0, hp)),
        scratch_shapes=[f32_rows] * 4 + [
            pltpu.VMEM((3, QBLK, 2 * QBLK), F32),
            pltpu.VMEM((2, DIL_CHUNK, 2 * QBLK, 2 * QBLK), F32),
            pltpu.VMEM((2, DIL_CHUNK, 2 * QBLK, 2 * QBLK), BF16)],
        compiler_params=pltpu.CompilerParams(
            dimension_semantics=("arbitrary", "arbitrary"),
            vmem_limit_bytes=VMEM_LIMIT),
        name="dil_attn",
    )(*[views[d] for d in SLOT_DIL for _ in range(3)])


def _na_kernel(q_ref, k_ref, v_ref, rpb_ref, qm_ref, km_ref, vm_ref,
               o_ref, om_ref, bias_scr, s_scr, p_scr, sm_scr, pm_scr, rdm_scr, *, rows):
    kr = min(NA_ROWS, rows)
    nk = kr * GRID_W
    n_var = 2 * NA_ROWS - kr
    left = lax.broadcasted_iota(jnp.int32, (1, LANES), 1) < HEAD_DIM
    keep_a = jnp.where(left, 1.0, 0.0).astype(BF16)
    keep_b = jnp.where(left, 0.0, 1.0).astype(BF16)

    @pl.when(pl.program_id(1) == 0)
    def _():
        qc = lax.broadcasted_iota(jnp.int32, (GRID_W, LANES), 0)
        kc = lax.broadcasted_iota(jnp.int32, (GRID_W, LANES), 1) % GRID_W
        rel = jnp.clip(kc - qc, -(NA_COLS - 1), NA_COLS - 1) + NA_COLS - 1
        c_start = jnp.clip(qc - NA_COLS // 2, 0, GRID_W - NA_COLS)
        valid = jnp.logical_and(kc >= c_start, kc < c_start + NA_COLS)
        for hh in range(LANES // HEAD_DIM):
            head = pl.program_id(0) * (LANES // HEAD_DIM) + hh
            for a in range(2 * NA_ROWS - 1):
                row = rpb_ref[pl.ds(head * (2 * NA_ROWS - 1) + a, 1), :]
                blk = jnp.take_along_axis(jnp.broadcast_to(row, (GRID_W, LANES)), rel, axis=1)
                blk = jnp.where(valid, blk * LOG2E, NEG)
                for j in range(kr):
                    if 0 <= a - j < n_var:
                        half = slice((j % 2) * GRID_W, (j % 2 + 1) * GRID_W)
                        bias_scr[hh, a - j, :, j * GRID_W:(j + 1) * GRID_W] = blk[:, half]

    def window(r):
        r_start = min(max(r - kr // 2, 0), rows - kr)
        return r * GRID_W, r_start * GRID_W, r_start - r + (NA_ROWS - 1)

    def scores(c):
        for t in range(NA_CHUNK):
            row0, ks, _ = window(c * NA_CHUNK + t)
            q2d = q_ref[row0:row0 + GRID_W, :]
            qq = jnp.concatenate([q2d * keep_a, q2d * keep_b], axis=0)
            s_scr[c % 2, t] = lax.dot_general(qq, k_ref[ks:ks + nk, :], (((1,), (1,)), ((), ())),
                                              preferred_element_type=F32)

    def softmax_pv(c):
        for t in range(NA_CHUNK):
            row0, ks, variant = window(c * NA_CHUNK + t)
            dens = []
            for hh in range(LANES // HEAD_DIM):
                rows_h = slice(hh * GRID_W, (hh + 1) * GRID_W)
                s = s_scr[c % 2, t, rows_h, :] + bias_scr[hh, variant]
                m = jnp.max(s, axis=-1, keepdims=True)
                p = jnp.exp2(s - m)
                dens.append(jnp.sum(p, axis=-1, keepdims=True))
                p_scr[t, rows_h, :] = p.astype(BF16)
            pv = jnp.dot(p_scr[t], v_ref[ks:ks + nk, :], preferred_element_type=F32)
            den_t = jnp.where(left, jnp.broadcast_to(dens[0], (GRID_W, LANES)),
                              jnp.broadcast_to(dens[1], (GRID_W, LANES)))
            o_ref[row0:row0 + GRID_W, :] = (
                jnp.where(left, pv[:GRID_W], pv[GRID_W:]) / den_t).astype(BF16)

    sm_scr[...] = lax.dot_general(qm_ref[...], km_ref[...], (((1,), (1,)), ((), ())),
                                  preferred_element_type=F32)

    def mem_softmax(i):
        rows_i = slice(i * QBLK, (i + 1) * QBLK)
        s = sm_scr[rows_i, :]
        m = jnp.max(s, axis=-1, keepdims=True)
        p = jnp.exp2(s - m)
        den = jnp.sum(p, axis=-1, keepdims=True)
        pm_scr[rows_i, :] = p.astype(BF16)
        rdm_scr[rows_i, :] = jnp.broadcast_to(1.0 / den, (QBLK, LANES))

    n_chunks = rows // NA_CHUNK
    mem_tiles = qm_ref.shape[0] // QBLK
    scores(0)
    for c in range(n_chunks):
        if c + 1 < n_chunks:
            scores(c + 1)
        softmax_pv(c)
        for i in range(c * mem_tiles // n_chunks, (c + 1) * mem_tiles // n_chunks):
            mem_softmax(i)
    pv = jnp.dot(pm_scr[...], vm_ref[...], preferred_element_type=F32)
    om_ref[...] = (pv * rdm_scr[...]).astype(BF16)


def _na_mem_attn(proj, rpb, kv_m):
    batch, _, seq, _ = proj.shape
    rows = seq // GRID_W
    kr = min(NA_ROWS, rows)
    nk = kr * GRID_W
    n_hp = WIDTH // LANES
    assert rpb.shape == (WIDTH // HEAD_DIM, 2 * NA_ROWS - 1, 2 * NA_COLS - 1)

    def in_spec(tile):
        return pl.BlockSpec((None, None, seq, LANES), lambda hp, b: (b, tile, 0, hp))

    rpb_rows = jnp.pad(rpb.reshape(-1, rpb.shape[-1]).astype(F32),
                       ((0, 0), (0, LANES - rpb.shape[-1])))
    mem_len = kv_m.shape[1]
    assert MEM_HEADS == n_hp and MEM_HEAD_DIM == LANES
    out_spec = pl.BlockSpec((None, seq, LANES), lambda hp, b: (b, 0, hp))
    return pl.pallas_call(
        functools.partial(_na_kernel, rows=rows),
        out_shape=(jax.ShapeDtypeStruct((batch, seq, WIDTH), BF16),
                   jax.ShapeDtypeStruct((batch, seq, WIDTH), BF16)),
        grid=(n_hp, batch),
        in_specs=[in_spec(T_NA), in_spec(T_NA + 1), in_spec(T_NA + 2),
                  pl.BlockSpec(rpb_rows.shape, lambda hp, b: (0, 0)),
                  in_spec(T_MEMQ),
                  pl.BlockSpec((None, mem_len, LANES), lambda hp, b: (b, 0, hp)),
                  pl.BlockSpec((None, mem_len, LANES), lambda hp, b: (b, 0, MEM_HEADS + hp))],
        out_specs=(out_spec, out_spec),
        scratch_shapes=[pltpu.VMEM((LANES // HEAD_DIM, 2 * NA_ROWS - kr, GRID_W, nk), F32),
                        pltpu.VMEM((2, NA_CHUNK, 2 * GRID_W, nk), F32),
                        pltpu.VMEM((NA_CHUNK, 2 * GRID_W, nk), BF16),
                        pltpu.VMEM((seq, mem_len), F32), pltpu.VMEM((seq, mem_len), BF16),
                        pltpu.VMEM((seq, LANES), F32)],
        compiler_params=pltpu.CompilerParams(
            dimension_semantics=("arbitrary", "arbitrary"), vmem_limit_bytes=VMEM_LIMIT),
        name="na_mem_attn",
    )(proj, proj, proj, rpb_rows, proj, kv_m, kv_m)


def _merge_kernel(oa, ga, ob, oc, gb, gc, za0, za1, zb0, zb1, zc0, zc1,
                  wa_f32, wb_f32, wc_f32, wo_f32, pn_ref, x_ref, out_ref,
                  u_scr, wa_ref, wb_ref, wc_ref, wo_ref, *, tm):
    @pl.when(pl.program_id(0) == 0)
    def _():
        for src, dst in ((wa_f32, wa_ref), (wb_f32, wb_ref), (wc_f32, wc_ref), (wo_f32, wo_ref)):
            dst[...] = src[...].astype(BF16)

    def silu_gated(o, silu_g):
        return o.astype(F32) * silu_g.astype(F32)

    seg = MERGE_ROWS // 16
    for blk in range(tm // MERGE_ROWS):
        rows = slice(blk * MERGE_ROWS, (blk + 1) * MERGE_ROWS)

        def project(u, w_ref, gate_lo, gate_hi):
            yb = jnp.dot(u.astype(BF16), w_ref[...], preferred_element_type=F32)
            gate = jnp.concatenate([gate_lo[rows, :], gate_hi[rows, :]], axis=-1)
            return gate.astype(F32) * yb

        p16_rows = slice(blk * seg, (blk + 1) * seg)
        ua = silu_gated(oa[:, p16_rows, :].reshape(MERGE_ROWS, WIDTH),
                        ga[:, p16_rows, :].reshape(MERGE_ROWS, WIDTH))
        for r in range(16):
            for c in range(WIDTH // LANES):
                u_scr[blk, c, pl.ds(r, seg, stride=16), :] = ua[r * seg:(r + 1) * seg,
                                                                c * LANES:(c + 1) * LANES]
        ua = jnp.concatenate([u_scr[blk, c] for c in range(WIDTH // LANES)], axis=-1)

        y = (project(ua, wa_ref, za0, za1)
             + project(silu_gated(ob[rows, :], gb[rows, :]), wb_ref, zb0, zb1)
             + project(silu_gated(oc[rows, :], gc[rows, :]), wc_ref, zc0, zc1))
        z = jnp.dot(y.astype(BF16), wo_ref[...], preferred_element_type=F32)
        ms = jnp.mean(z * z, axis=-1, keepdims=True)
        out_ref[rows, :] = x_ref[rows, :] + z * lax.rsqrt(ms + EPS) * pn_ref[...]


def _merge(x2d, proj, out_a16, out_b, out_c, wa, wb, wc, wo, post_norm, *, tm):
    rows, dm = x2d.shape
    batch, _, seq, _ = proj.shape
    per_b = seq // tm
    seg = tm // 16
    proj16 = proj.reshape(batch, N_STEPS, 16, seq // 16, WIDTH)

    def p16_spec(tile):
        return pl.BlockSpec((None, None, 16, seg, WIDTH),
                            lambda i: (i // per_b, tile, 0, i % per_b, 0))

    def nat_spec(tile):
        return pl.BlockSpec((None, None, tm, WIDTH), lambda i: (i // per_b, tile, i % per_b, 0))

    def rows_spec(width):
        return pl.BlockSpec((tm, width), lambda i: (i, 0))

    def full(shape):
        return pl.BlockSpec(shape, lambda i: (0, 0), pipeline_mode=pl.Buffered(1))

    return pl.pallas_call(
        functools.partial(_merge_kernel, tm=tm),
        out_shape=jax.ShapeDtypeStruct((rows, dm), F32),
        grid=(rows // tm,),
        in_specs=[pl.BlockSpec((None, 16, seg, WIDTH), lambda i: (i // per_b, 0, i % per_b, 0)),
                  p16_spec(T_GATE_A), rows_spec(WIDTH), rows_spec(WIDTH),
                  nat_spec(T_GATE_B), nat_spec(T_GATE_B + 1),
                  nat_spec(T_MERGE_A), nat_spec(T_MERGE_A + 1),
                  nat_spec(T_MERGE_B), nat_spec(T_MERGE_B + 1),
                  nat_spec(T_MERGE_B + 2), nat_spec(T_MERGE_B + 3),
                  full(wa.shape), full(wb.shape), full(wc.shape),
                  full(wo.shape), full(post_norm.shape), rows_spec(dm)],
        out_specs=pl.BlockSpec((tm, dm), lambda i: (i, 0)),
        scratch_shapes=[pltpu.VMEM((tm // MERGE_ROWS, WIDTH // LANES, MERGE_ROWS, LANES), F32)]
        + [pltpu.VMEM(w.shape, BF16) for w in (wa, wb, wc, wo)],
        compiler_params=pltpu.CompilerParams(
            dimension_semantics=("arbitrary",), vmem_limit_bytes=VMEM_LIMIT),
        name="merge",
    )(out_a16, proj16, out_b, out_c, proj, proj, proj, proj, proj, proj, proj, proj,
      wa, wb, wc, wo, post_norm, x2d)


def kernel(x, mem, pre_norm, w_in, merge_bias, na_rpb, mem_norm, w_mem_kv,
           w_branch_a, w_branch_b, w_branch_c, w_out, post_norm):
    b, s, dm = x.shape
    depth = pre_norm.shape[0]
    reach = (DIL_CONFIGS[0][0] // 2) // DIL_CONFIGS[0][1]
    assert all((w // 2) // d == reach for w, d in DIL_CONFIGS) and 2 * reach == QBLK
    assert tuple(d for _, d in DIL_CONFIGS) == SLOT_DIL
    cos_t, sin_t = _rope_tables(s)
    for layer in range(depth):
        x2d = x.reshape(b * s, dm)
        proj = _proj(x2d, pre_norm[layer][None], w_in[layer], merge_bias[layer], cos_t, sin_t,
                     batch=b, seq=s)

        out_a16 = _dil_attn(proj, reach=reach)
        mem2d = mem.reshape(b * mem.shape[1], dm)
        kv_m = _kv_proj(mem2d, mem_norm[layer][None], w_mem_kv[layer], tm=1024)
        out_b, out_c = _na_mem_attn(proj, na_rpb[layer], kv_m.reshape(b, mem.shape[1], -1))

        y = _merge(x2d, proj, out_a16, out_b.reshape(b * s, WIDTH), out_c.reshape(b * s, WIDTH),
                   w_branch_a[layer], w_branch_b[layer], w_branch_c[layer],
                   w_out[layer], post_norm[layer][None], tm=2 * MERGE_ROWS)
        x = y.reshape(b, s, dm)
    return x
```

```python
import functools
import math

import jax
import jax.numpy as jnp
import numpy as np
from jax import lax
from jax.experimental import pallas as pl
from jax.experimental.pallas import tpu as pltpu

D_MODEL = 1024
HEAD_DIM = 64
DIL_CONFIGS = ((128, 1), (512, 4), (2048, 16))
WIDTH = 512
NA_ROWS = 8
NA_COLS = 16
GRID_W = 64
MEM_HEADS = 4
MEM_HEAD_DIM = 128
ROPE_THETA = 500000.0
ROPE_DIM = HEAD_DIM // 4
ROPE_HALF = ROPE_DIM // 2
EPS = 1e-6
NEG = -1e30
LOG2E = math.log2(math.e)

LANES = 128
QBLK = 128
DIL_CHUNK = 1
DIL_AHEAD = 1
N_SBUF = DIL_AHEAD + 1
NA_CHUNK = 8
PROJ_ROWS = 256
MERGE_ROWS = 256
VMEM_LIMIT = 56 * 1024 * 1024

_QS = HEAD_DIM ** -0.5 * LOG2E
_MS = MEM_HEAD_DIM ** -0.5 * LOG2E
EPI_ROPE, EPI_SCALE, EPI_SIGMOID, EPI_SILU = range(4)
STEPS = (
    (0, 0, EPI_ROPE, _QS), (1, 0, EPI_ROPE, 1.0), (2, 0, EPI_SCALE, 1.0),
    (9, 0, EPI_SCALE, _QS), (10, 0, EPI_SCALE, 1.0), (11, 0, EPI_SCALE, 1.0),
    (12, 0, EPI_SCALE, _MS),
    (14, 0, EPI_SILU, 1.0), (15, 0, EPI_SILU, 1.0),
    (18, 0, EPI_SIGMOID, 1.0), (19, 0, EPI_SIGMOID, 1.0),
    (20, 0, EPI_SIGMOID, 1.0), (21, 0, EPI_SIGMOID, 1.0),
    (16, 0, EPI_SIGMOID, 1.0), (17, 0, EPI_SIGMOID, 1.0),
    (3, 1, EPI_ROPE, _QS), (4, 1, EPI_ROPE, 1.0), (5, 1, EPI_SCALE, 1.0),
    (6, 2, EPI_ROPE, _QS), (7, 2, EPI_ROPE, 1.0), (8, 2, EPI_SCALE, 1.0),
    (13, 2, EPI_SILU, 1.0),
)
FIRST_MERGE_TILE = 16
N_STEPS = len(STEPS)
T_G0, T_NA, T_MEMQ, T_GATE_B, T_MERGE_B, T_MERGE_A = 0, 3, 6, 7, 9, 13
T_G1, T_G2, T_GATE_A = 15, 18, 21
SLOT_DIL = (1, 4, 16)

BF16 = jnp.bfloat16
F32 = jnp.float32


def _sigmoid(z):
    return 0.5 * jnp.tanh(0.5 * z) + 0.5


def _proj_kernel(wtile_ref, epi_ref, scale_ref, x_ref, g_ref, w_ref, mb_ref,
                 cos_ref, sin_ref, o_ref, h_ref, wb_ref, *, tm):
    j = pl.program_id(1)
    n_lane_tiles = x_ref.shape[1] // LANES

    @pl.when(j == 0)
    def _():
        rb = 256
        ssq = [jnp.sum(jnp.square(x_ref[r0:r0 + rb, :]), axis=-1, keepdims=True)
               for r0 in range(0, tm, rb)]
        rs = lax.rsqrt(jnp.concatenate(ssq, axis=0) * (1.0 / x_ref.shape[1]) + EPS)
        for c in range(n_lane_tiles):
            cs = slice(c * LANES, (c + 1) * LANES)
            hf = x_ref[:, cs] * rs * g_ref[:, cs]
            h_ref[0, :, cs] = hf.astype(BF16)
            for slot, d in ((1, 4), (2, 16)):
                by_class = jnp.swapaxes(hf.reshape(tm // d, d, LANES), 0, 1)
                h_ref[slot, :, cs] = by_class.reshape(tm, LANES).astype(BF16)

    slot = (j >= T_G1).astype(jnp.int32) + (j >= T_G2).astype(jnp.int32)
    q_scale = scale_ref[j]

    def step(epi):
        wb_ref[...] = w_ref[...].astype(BF16)
        lane = lax.broadcasted_iota(jnp.int32, (1, LANES), 1)
        first_half = lane % HEAD_DIM < ROPE_HALF
        if epi == EPI_SIGMOID:
            bias = mb_ref[pl.ds(jnp.maximum(wtile_ref[j] - FIRST_MERGE_TILE, 0), 1), :]
        block = PROJ_ROWS // 2 if epi == EPI_ROPE else PROJ_ROWS
        for r0 in range(0, tm, block):
            rows = pl.ds(r0, block)
            acc = jnp.dot(h_ref[slot, rows, :], wb_ref[...], preferred_element_type=F32)
            if epi == EPI_ROPE:
                c, s = cos_ref[rows, :], sin_ref[rows, :]
            for t in range(WIDTH // LANES):
                lanes = slice(t * LANES, (t + 1) * LANES)
                a = acc[:, lanes]
                if epi == EPI_ROPE:
                    partner = jnp.where(first_half, pltpu.roll(a, LANES - ROPE_HALF, 1),
                                        pltpu.roll(a, ROPE_HALF, 1))
                    a = (a * c + partner * s) * q_scale
                elif epi == EPI_SCALE:
                    a = a * q_scale
                elif epi == EPI_SIGMOID:
                    a = _sigmoid(a + bias[:, lanes])
                else:
                    a = a * _sigmoid(a)
                o_ref[rows, lanes] = a.astype(BF16)

    for epi in (EPI_ROPE, EPI_SCALE, EPI_SIGMOID, EPI_SILU):
        pl.when(epi_ref[j] == epi)(functools.partial(step, epi))


def _proj(x2d, gain, w, merge_bias, cos_t, sin_t, *, batch, seq):
    rows, dm = x2d.shape
    tm = seq
    prefetch = (jnp.asarray([st[0] for st in STEPS], jnp.int32),
                jnp.asarray([st[2] for st in STEPS], jnp.int32),
                jnp.asarray([st[3] for st in STEPS], F32))
    mb_rows = merge_bias.reshape(-1, WIDTH)

    def slot_of(j):
        return (j >= T_G1).astype(jnp.int32) + (j >= T_G2).astype(jnp.int32)

    tab_spec = pl.BlockSpec((None, tm, LANES), lambda i, j, *_: (slot_of(j), 0, 0))
    return pl.pallas_call(
        functools.partial(_proj_kernel, tm=tm),
        out_shape=jax.ShapeDtypeStruct((batch, N_STEPS, seq, WIDTH), BF16),
        grid_spec=pltpu.PrefetchScalarGridSpec(
            num_scalar_prefetch=len(prefetch),
            grid=(rows // tm, N_STEPS),
            in_specs=[
                pl.BlockSpec((tm, dm), lambda i, j, *_: (i, 0)),
                pl.BlockSpec((1, dm), lambda i, j, *_: (0, 0)),
                pl.BlockSpec((dm, WIDTH), lambda i, j, wt, ep, sc: (0, wt[j])),
                pl.BlockSpec(mb_rows.shape, lambda i, j, *_: (0, 0)),
                tab_spec, tab_spec,
            ],
            out_specs=pl.BlockSpec((None, None, tm, WIDTH), lambda i, j, *_: (i, j, 0, 0)),
            scratch_shapes=[pltpu.VMEM((3, tm, dm), BF16), pltpu.VMEM((dm, WIDTH), BF16)]),
        compiler_params=pltpu.CompilerParams(
            dimension_semantics=("arbitrary", "arbitrary"),
            vmem_limit_bytes=VMEM_LIMIT),
        name="proj",
    )(*prefetch, x2d, gain, w, mb_rows, cos_t, sin_t)


def _rope_tables(seq):
    pos = np.arange(seq, dtype=np.float64)
    inv = ROPE_THETA ** (-np.arange(ROPE_HALF, dtype=np.float64) * 2.0 / ROPE_DIM)
    ang = pos[:, None] * inv[None, :]
    cos, sin = np.cos(ang), np.sin(ang)
    rest = HEAD_DIM - ROPE_DIM
    c = np.concatenate([cos, cos, np.ones((seq, rest))] * (LANES // HEAD_DIM), axis=-1)
    s = np.concatenate([-sin, sin, np.zeros((seq, rest))] * (LANES // HEAD_DIM), axis=-1)

    def orders(t):
        out = [t]
        for d in SLOT_DIL[1:]:
            out.append(t.reshape(seq // d, d, LANES).transpose(1, 0, 2).reshape(seq, LANES))
        return jnp.asarray(np.stack(out), F32)

    return orders(c), orders(s)


def _kv_kernel(x_ref, g_ref, w_ref, o_ref):
    xf = x_ref[...]
    ms = jnp.mean(xf * xf, axis=-1, keepdims=True)
    h = (xf * lax.rsqrt(ms + EPS) * g_ref[...]).astype(BF16)
    o_ref[...] = jnp.dot(h, w_ref[...].astype(BF16), preferred_element_type=F32).astype(BF16)


def _kv_proj(mem2d, gain, w, *, tm):
    rows, dm = mem2d.shape
    n_out = w.shape[1]
    return pl.pallas_call(
        _kv_kernel,
        out_shape=jax.ShapeDtypeStruct((rows, n_out), BF16),
        grid=(rows // tm,),
        in_specs=[pl.BlockSpec((tm, dm), lambda i: (i, 0)),
                  pl.BlockSpec((1, dm), lambda i: (0, 0)),
                  pl.BlockSpec((dm, n_out), lambda i: (0, 0))],
        out_specs=pl.BlockSpec((tm, n_out), lambda i: (i, 0)),
        compiler_params=pltpu.CompilerParams(
            dimension_semantics=("arbitrary",), vmem_limit_bytes=VMEM_LIMIT),
        name="kv_proj",
    )(mem2d, gain, w)


def _dil_kernel(q0, k0, v0, q1, k1, v1, q2, k2, v2, o_ref,
                og0, lg0, og1, lg1, bias_scr, s_scr, p_scr, *, seq, reach):
    left = lax.broadcasted_iota(jnp.int32, (1, LANES), 1) < HEAD_DIM
    keep_a = jnp.where(left, 1.0, 0.0).astype(BF16)
    keep_b = jnp.where(left, 0.0, 1.0).astype(BF16)
    kw_max = 2 * QBLK

    @pl.when(jnp.logical_and(pl.program_id(0) == 0, pl.program_id(1) == 0))
    def _():
        rel = (lax.broadcasted_iota(jnp.int32, (QBLK, kw_max), 0)
               - lax.broadcasted_iota(jnp.int32, (QBLK, kw_max), 1))
        for n in range(3):
            bias_scr[n] = jnp.where(jnp.abs(rel + n * reach) <= reach, 0.0, NEG)

    def pair_tile(col_a, col_b):
        return jnp.where(left, jnp.broadcast_to(col_a, (QBLK, LANES)),
                         jnp.broadcast_to(col_b, (QBLK, LANES)))

    def keep(o_scr, l_scr, length):
        def finish(r, row0, acc, m, den):
            rows = pl.ds(r * length + row0, QBLK)
            o_scr[rows, :] = acc / den
            l_scr[rows, :] = m + jnp.log2(den)
        return finish

    def combine(r, row0, acc2, m2, den2):
        rows0 = pl.ds(r, QBLK, stride=16)
        rows1 = pl.ds((r % 4) * (seq // 4) + r // 4, QBLK, stride=4)
        l0, l1 = lg0[rows0, :], lg1[rows1, :]
        m = jnp.maximum(jnp.maximum(l0, l1), m2)
        w0, w1, w2 = jnp.exp2(l0 - m), jnp.exp2(l1 - m), jnp.exp2(m2 - m)
        num = w0 * og0[rows0, :] + w1 * og1[rows1, :] + w2 * acc2
        o_ref[r] = (num / (w0 + w1 + w2 * den2)).astype(BF16)

    tiles = []
    for (q_ref, k_ref, v_ref), d, finish in (
            ((q0, k0, v0), 1, keep(og0, lg0, seq)),
            ((q1, k1, v1), 4, keep(og1, lg1, seq // 4)),
            ((q2, k2, v2), 16, combine)):
        length = seq // d
        kw = min(kw_max, length)
        for bi in range(seq // QBLK):
            r, row0 = divmod(bi * QBLK, length)
            ks = min(max(row0 - reach, 0), length - kw)
            tiles.append((q_ref, k_ref, v_ref, r, row0, ks, kw, finish))
    chunks = [tiles[i:i + DIL_CHUNK] for i in range(0, len(tiles), DIL_CHUNK)]

    def scores(c):
        for t, (q_ref, k_ref, _, r, row0, ks, kw, _) in enumerate(chunks[c]):
            q2d = q_ref[r, row0:row0 + QBLK, :]
            qq = jnp.concatenate([q2d * keep_a, q2d * keep_b], axis=0)
            s_scr[c % N_SBUF, t, :, :kw] = lax.dot_general(
                qq, k_ref[r, ks:ks + kw, :], (((1,), (1,)), ((), ())),
                preferred_element_type=F32)

    def softmax_pv(c):
        for t, (_, _, v_ref, r, row0, ks, kw, finish) in enumerate(chunks[c]):
            bias = bias_scr[(row0 - ks) // reach, :, :kw]
            s = s_scr[c % N_SBUF, t, :, :kw] + jnp.concatenate([bias, bias], axis=0)
            m = jnp.max(s, axis=-1, keepdims=True)
            p = jnp.exp2(s - m)
            den = jnp.sum(p, axis=-1, keepdims=True)
            p_scr[c % 2, t, :, :kw] = p.astype(BF16)
            pv = jnp.dot(p_scr[c % 2, t, :, :kw], v_ref[r, ks:ks + kw, :],
                         preferred_element_type=F32)
            finish(r, row0, jnp.where(left, pv[:QBLK], pv[QBLK:]),
                   pair_tile(m[:QBLK], m[QBLK:]), pair_tile(den[:QBLK], den[QBLK:]))

    for c in range(min(DIL_AHEAD, len(chunks))):
        scores(c)
    for c in range(len(chunks)):
        if c + DIL_AHEAD < len(chunks):
            scores(c + DIL_AHEAD)
        softmax_pv(c)


def _dil_attn(proj, *, reach):
    batch, _, seq, _ = proj.shape
    n_hp = WIDTH // LANES

    def spec(first_tile, d, kind):
        return pl.BlockSpec((None, None, d, seq // d, LANES),
                            lambda b, hp: (b, first_tile + kind, 0, 0, hp))

    views = {d: proj.reshape(batch, N_STEPS, d, seq // d, WIDTH) for d in SLOT_DIL}
    f32_rows = pltpu.VMEM((seq, LANES), F32)
    return pl.pallas_call(
        functools.partial(_dil_kernel, seq=seq, reach=reach),
        out_shape=jax.ShapeDtypeStruct((batch, 16, seq // 16, WIDTH), BF16),
        grid=(batch, n_hp),
        in_specs=[spec(t0, d, kind) for t0, d in zip((T_G0, T_G1, T_G2), SLOT_DIL)
                  for kind in range(3)],
        out_specs=pl.BlockSpec((None, 16, seq // 16, LANES), lambda b, hp: (b, 0, 0, hp)),
        scratch_shapes=[f32_rows] * 4 + [
            pltpu.VMEM((3, QBLK, 2 * QBLK), F32),
            pltpu.VMEM((N_SBUF, DIL_CHUNK, 2 * QBLK, 2 * QBLK), F32),
            pltpu.VMEM((2, DIL_CHUNK, 2 * QBLK, 2 * QBLK), BF16)],
        compiler_params=pltpu.CompilerParams(
            dimension_semantics=("arbitrary", "arbitrary"),
            vmem_limit_bytes=VMEM_LIMIT),
        name="dil_attn",
    )(*[views[d] for d in SLOT_DIL for _ in range(3)])


def _na_kernel(q_ref, k_ref, v_ref, rpb_ref, qm_ref, km_ref, vm_ref,
               o_ref, om_ref, bias_scr, s_scr, p_scr, sm_scr, pm_scr, rdm_scr, *, rows):
    kr = min(NA_ROWS, rows)
    nk = kr * GRID_W
    n_var = 2 * NA_ROWS - kr
    left = lax.broadcasted_iota(jnp.int32, (1, LANES), 1) < HEAD_DIM
    keep_a = jnp.where(left, 1.0, 0.0).astype(BF16)
    keep_b = jnp.where(left, 0.0, 1.0).astype(BF16)

    @pl.when(pl.program_id(1) == 0)
    def _():
        qc = lax.broadcasted_iota(jnp.int32, (GRID_W, LANES), 0)
        kc = lax.broadcasted_iota(jnp.int32, (GRID_W, LANES), 1) % GRID_W
        rel = jnp.clip(kc - qc, -(NA_COLS - 1), NA_COLS - 1) + NA_COLS - 1
        c_start = jnp.clip(qc - NA_COLS // 2, 0, GRID_W - NA_COLS)
        valid = jnp.logical_and(kc >= c_start, kc < c_start + NA_COLS)
        for hh in range(LANES // HEAD_DIM):
            head = pl.program_id(0) * (LANES // HEAD_DIM) + hh
            for a in range(2 * NA_ROWS - 1):
                row = rpb_ref[pl.ds(head * (2 * NA_ROWS - 1) + a, 1), :]
                blk = jnp.take_along_axis(jnp.broadcast_to(row, (GRID_W, LANES)), rel, axis=1)
                blk = jnp.where(valid, blk * LOG2E, NEG)
                for j in range(kr):
                    if 0 <= a - j < n_var:
                        half = slice((j % 2) * GRID_W, (j % 2 + 1) * GRID_W)
                        bias_scr[hh, a - j, :, j * GRID_W:(j + 1) * GRID_W] = blk[:, half]

    def window(r):
        r_start = min(max(r - kr // 2, 0), rows - kr)
        return r * GRID_W, r_start * GRID_W, r_start - r + (NA_ROWS - 1)

    def scores(c):
        for t in range(NA_CHUNK):
            row0, ks, _ = window(c * NA_CHUNK + t)
            q2d = q_ref[row0:row0 + GRID_W, :]
            qq = jnp.concatenate([q2d * keep_a, q2d * keep_b], axis=0)
            s_scr[c % 2, t] = lax.dot_general(qq, k_ref[ks:ks + nk, :], (((1,), (1,)), ((), ())),
                                              preferred_element_type=F32)

    def softmax_pv(c):
        for t in range(NA_CHUNK):
            row0, ks, variant = window(c * NA_CHUNK + t)
            dens = []
            for hh in range(LANES // HEAD_DIM):
                rows_h = slice(hh * GRID_W, (hh + 1) * GRID_W)
                s = s_scr[c % 2, t, rows_h, :] + bias_scr[hh, variant]
                m = jnp.max(s, axis=-1, keepdims=True)
                p = jnp.exp2(s - m)
                dens.append(jnp.sum(p, axis=-1, keepdims=True))
                p_scr[t, rows_h, :] = p.astype(BF16)
            pv = jnp.dot(p_scr[t], v_ref[ks:ks + nk, :], preferred_element_type=F32)
            den_t = jnp.where(left, jnp.broadcast_to(dens[0], (GRID_W, LANES)),
                              jnp.broadcast_to(dens[1], (GRID_W, LANES)))
            o_ref[row0:row0 + GRID_W, :] = (
                jnp.where(left, pv[:GRID_W], pv[GRID_W:]) / den_t).astype(BF16)

    sm_scr[...] = lax.dot_general(qm_ref[...], km_ref[...], (((1,), (1,)), ((), ())),
                                  preferred_element_type=F32)

    def mem_softmax(i):
        rows_i = slice(i * QBLK, (i + 1) * QBLK)
        s = sm_scr[rows_i, :]
        m = jnp.max(s, axis=-1, keepdims=True)
        p = jnp.exp2(s - m)
        den = jnp.sum(p, axis=-1, keepdims=True)
        pm_scr[rows_i, :] = p.astype(BF16)
        rdm_scr[rows_i, :] = jnp.broadcast_to(1.0 / den, (QBLK, LANES))

    n_chunks = rows // NA_CHUNK
    mem_tiles = qm_ref.shape[0] // QBLK
    scores(0)
    for c in range(n_chunks):
        if c + 1 < n_chunks:
            scores(c + 1)
        softmax_pv(c)
        for i in range(c * mem_tiles // n_chunks, (c + 1) * mem_tiles // n_chunks):
            mem_softmax(i)
    pv = jnp.dot(pm_scr[...], vm_ref[...], preferred_element_type=F32)
    om_ref[...] = (pv * rdm_scr[...]).astype(BF16)


def _na_mem_attn(proj, rpb, kv_m):
    batch, _, seq, _ = proj.shape
    rows = seq // GRID_W
    kr = min(NA_ROWS, rows)
    nk = kr * GRID_W
    n_hp = WIDTH // LANES
    assert rpb.shape == (WIDTH // HEAD_DIM, 2 * NA_ROWS - 1, 2 * NA_COLS - 1)

    def in_spec(tile):
        return pl.BlockSpec((None, None, seq, LANES), lambda hp, b: (b, tile, 0, hp))

    rpb_rows = jnp.pad(rpb.reshape(-1, rpb.shape[-1]).astype(F32),
                       ((0, 0), (0, LANES - rpb.shape[-1])))
    mem_len = kv_m.shape[1]
    assert MEM_HEADS == n_hp and MEM_HEAD_DIM == LANES
    out_spec = pl.BlockSpec((None, seq, LANES), lambda hp, b: (b, 0, hp))
    return pl.pallas_call(
        functools.partial(_na_kernel, rows=rows),
        out_shape=(jax.ShapeDtypeStruct((batch, seq, WIDTH), BF16),
                   jax.ShapeDtypeStruct((batch, seq, WIDTH), BF16)),
        grid=(n_hp, batch),
        in_specs=[in_spec(T_NA), in_spec(T_NA + 1), in_spec(T_NA + 2),
                  pl.BlockSpec(rpb_rows.shape, lambda hp, b: (0, 0)),
                  in_spec(T_MEMQ),
                  pl.BlockSpec((None, mem_len, LANES), lambda hp, b: (b, 0, hp)),
                  pl.BlockSpec((None, mem_len, LANES), lambda hp, b: (b, 0, MEM_HEADS + hp))],
        out_specs=(out_spec, out_spec),
        scratch_shapes=[pltpu.VMEM((LANES // HEAD_DIM, 2 * NA_ROWS - kr, GRID_W, nk), F32),
                        pltpu.VMEM((2, NA_CHUNK, 2 * GRID_W, nk), F32),
                        pltpu.VMEM((NA_CHUNK, 2 * GRID_W, nk), BF16),
                        pltpu.VMEM((seq, mem_len), F32), pltpu.VMEM((seq, mem_len), BF16),
                        pltpu.VMEM((seq, LANES), F32)],
        compiler_params=pltpu.CompilerParams(
            dimension_semantics=("arbitrary", "arbitrary"), vmem_limit_bytes=VMEM_LIMIT),
        name="na_mem_attn",
    )(proj, proj, proj, rpb_rows, proj, kv_m, kv_m)


def _merge_kernel(oa, ga, ob, oc, gb, gc, za0, za1, zb0, zb1, zc0, zc1,
                  wa_f32, wb_f32, wc_f32, wo_f32, pn_ref, x_ref, out_ref,
                  u_scr, wa_ref, wb_ref, wc_ref, wo_ref, *, tm):
    @pl.when(pl.program_id(0) == 0)
    def _():
        for src, dst in ((wa_f32, wa_ref), (wb_f32, wb_ref), (wc_f32, wc_ref), (wo_f32, wo_ref)):
            dst[...] = src[...].astype(BF16)

    def silu_gated(o, silu_g):
        return o.astype(F32) * silu_g.astype(F32)

    seg = MERGE_ROWS // 16
    for blk in range(tm // MERGE_ROWS):
        rows = slice(blk * MERGE_ROWS, (blk + 1) * MERGE_ROWS)

        def project(u, w_ref, gate_lo, gate_hi):
            yb = jnp.dot(u.astype(BF16), w_ref[...], preferred_element_type=F32)
            gate = jnp.concatenate([gate_lo[rows, :], gate_hi[rows, :]], axis=-1)
            return gate.astype(F32) * yb

        p16_rows = slice(blk * seg, (blk + 1) * seg)
        ua = silu_gated(oa[:, p16_rows, :].reshape(MERGE_ROWS, WIDTH),
                        ga[:, p16_rows, :].reshape(MERGE_ROWS, WIDTH))
        for r in range(16):
            for c in range(WIDTH // LANES):
                u_scr[blk, c, pl.ds(r, seg, stride=16), :] = ua[r * seg:(r + 1) * seg,
                                                                c * LANES:(c + 1) * LANES]
        ua = jnp.concatenate([u_scr[blk, c] for c in range(WIDTH // LANES)], axis=-1)

        y = (project(ua, wa_ref, za0, za1)
             + project(silu_gated(ob[rows, :], gb[rows, :]), wb_ref, zb0, zb1)
             + project(silu_gated(oc[rows, :], gc[rows, :]), wc_ref, zc0, zc1))
        z = jnp.dot(y.astype(BF16), wo_ref[...], preferred_element_type=F32)
        ms = jnp.mean(z * z, axis=-1, keepdims=True)
        out_ref[rows, :] = x_ref[rows, :] + z * lax.rsqrt(ms + EPS) * pn_ref[...]


def _merge(x2d, proj, out_a16, out_b, out_c, wa, wb, wc, wo, post_norm, *, tm):
    rows, dm = x2d.shape
    batch, _, seq, _ = proj.shape
    per_b = seq // tm
    seg = tm // 16
    proj16 = proj.reshape(batch, N_STEPS, 16, seq // 16, WIDTH)

    def p16_spec(tile):
        return pl.BlockSpec((None, None, 16, seg, WIDTH),
                            lambda i: (i // per_b, tile, 0, i % per_b, 0))

    def nat_spec(tile):
        return pl.BlockSpec((None, None, tm, WIDTH), lambda i: (i // per_b, tile, i % per_b, 0))

    def rows_spec(width):
        return pl.BlockSpec((tm, width), lambda i: (i, 0))

    def full(shape):
        return pl.BlockSpec(shape, lambda i: (0, 0), pipeline_mode=pl.Buffered(1))

    return pl.pallas_call(
        functools.partial(_merge_kernel, tm=tm),
        out_shape=jax.ShapeDtypeStruct((rows, dm), F32),
        grid=(rows // tm,),
        in_specs=[pl.BlockSpec((None, 16, seg, WIDTH), lambda i: (i // per_b, 0, i % per_b, 0)),
                  p16_spec(T_GATE_A), rows_spec(WIDTH), rows_spec(WIDTH),
                  nat_spec(T_GATE_B), nat_spec(T_GATE_B + 1),
                  nat_spec(T_MERGE_A), nat_spec(T_MERGE_A + 1),
                  nat_spec(T_MERGE_B), nat_spec(T_MERGE_B + 1),
                  nat_spec(T_MERGE_B + 2), nat_spec(T_MERGE_B + 3),
                  full(wa.shape), full(wb.shape), full(wc.shape),
                  full(wo.shape), full(post_norm.shape), rows_spec(dm)],
        out_specs=pl.BlockSpec((tm, dm), lambda i: (i, 0)),
        scratch_shapes=[pltpu.VMEM((tm // MERGE_ROWS, WIDTH // LANES, MERGE_ROWS, LANES), F32)]
        + [pltpu.VMEM(w.shape, BF16) for w in (wa, wb, wc, wo)],
        compiler_params=pltpu.CompilerParams(
            dimension_semantics=("arbitrary",), vmem_limit_bytes=VMEM_LIMIT),
        name="merge",
    )(out_a16, proj16, out_b, out_c, proj, proj, proj, proj, proj, proj, proj, proj,
      wa, wb, wc, wo, post_norm, x2d)


def kernel(x, mem, pre_norm, w_in, merge_bias, na_rpb, mem_norm, w_mem_kv,
           w_branch_a, w_branch_b, w_branch_c, w_out, post_norm):
    b, s, dm = x.shape
    depth = pre_norm.shape[0]
    reach = (DIL_CONFIGS[0][0] // 2) // DIL_CONFIGS[0][1]
    assert all((w // 2) // d == reach for w, d in DIL_CONFIGS) and 2 * reach == QBLK
    assert tuple(d for _, d in DIL_CONFIGS) == SLOT_DIL
    cos_t, sin_t = _rope_tables(s)
    for layer in range(depth):
        x2d = x.reshape(b * s, dm)
        proj = _proj(x2d, pre_norm[layer][None], w_in[layer], merge_bias[layer], cos_t, sin_t,
                     batch=b, seq=s)

        out_a16 = _dil_attn(proj, reach=reach)
        mem2d = mem.reshape(b * mem.shape[1], dm)
        kv_m = _kv_proj(mem2d, mem_norm[layer][None], w_mem_kv[layer], tm=1024)
        out_b, out_c = _na_mem_attn(proj, na_rpb[layer], kv_m.reshape(b, mem.shape[1], -1))

        y = _merge(x2d, proj, out_a16, out_b.reshape(b * s, WIDTH), out_c.reshape(b * s, WIDTH),
                   w_branch_a[layer], w_branch_b[layer], w_branch_c[layer],
                   w_out[layer], post_norm[layer][None], tm=2 * MERGE_ROWS)
        x = y.reshape(b, s, dm)
    return x
```

```python
import functools
import math

import jax
import jax.numpy as jnp
import numpy as np
from jax import lax
from jax.experimental import pallas as pl
from jax.experimental.pallas import tpu as pltpu

D_MODEL = 1024
HEAD_DIM = 64
DIL_CONFIGS = ((128, 1), (512, 4), (2048, 16))
WIDTH = 512
NA_ROWS = 8
NA_COLS = 16
GRID_W = 64
MEM_HEADS = 4
MEM_HEAD_DIM = 128
ROPE_THETA = 500000.0
ROPE_DIM = HEAD_DIM // 4
ROPE_HALF = ROPE_DIM // 2
EPS = 1e-6
NEG = -1e30
LOG2E = math.log2(math.e)

LANES = 128
QBLK = 128
DIL_CHUNK = 1
DIL_AHEAD = 1
N_SBUF = DIL_AHEAD + 1
NA_CHUNK = 8
PROJ_ROWS = 256
MERGE_ROWS = 256
VMEM_LIMIT = 56 * 1024 * 1024

_QS = HEAD_DIM ** -0.5 * LOG2E
_MS = MEM_HEAD_DIM ** -0.5 * LOG2E
EPI_ROPE, EPI_SCALE, EPI_SIGMOID, EPI_SILU = range(4)
STEPS = (
    (0, 0, EPI_ROPE, _QS), (1, 0, EPI_ROPE, 1.0), (2, 0, EPI_SCALE, 1.0),
    (9, 0, EPI_SCALE, _QS), (10, 0, EPI_SCALE, 1.0), (11, 0, EPI_SCALE, 1.0),
    (12, 0, EPI_SCALE, _MS),
    (14, 0, EPI_SILU, 1.0), (15, 0, EPI_SILU, 1.0),
    (18, 0, EPI_SIGMOID, 1.0), (19, 0, EPI_SIGMOID, 1.0),
    (20, 0, EPI_SIGMOID, 1.0), (21, 0, EPI_SIGMOID, 1.0),
    (16, 0, EPI_SIGMOID, 1.0), (17, 0, EPI_SIGMOID, 1.0),
    (3, 1, EPI_ROPE, _QS), (4, 1, EPI_ROPE, 1.0), (5, 1, EPI_SCALE, 1.0),
    (6, 2, EPI_ROPE, _QS), (7, 2, EPI_ROPE, 1.0), (8, 2, EPI_SCALE, 1.0),
    (13, 2, EPI_SILU, 1.0),
)
FIRST_MERGE_TILE = 16
N_STEPS = len(STEPS)
T_G0, T_NA, T_MEMQ, T_GATE_B, T_MERGE_B, T_MERGE_A = 0, 3, 6, 7, 9, 13
T_G1, T_G2, T_GATE_A = 15, 18, 21
SLOT_DIL = (1, 4, 16)

BF16 = jnp.bfloat16
F32 = jnp.float32


def _sigmoid(z):
    return 0.5 * jnp.tanh(0.5 * z) + 0.5


def _proj_kernel(wtile_ref, epi_ref, scale_ref, x_ref, g_ref, w_ref, mb_ref,
                 cos_ref, sin_ref, o_ref, h_ref, *, tm):
    j = pl.program_id(1)
    n_lane_tiles = x_ref.shape[1] // LANES

    @pl.when(j == 0)
    def _():
        rb = 256
        ssq = [jnp.sum(jnp.square(x_ref[r0:r0 + rb, :]), axis=-1, keepdims=True)
               for r0 in range(0, tm, rb)]
        rs = lax.rsqrt(jnp.concatenate(ssq, axis=0) * (1.0 / x_ref.shape[1]) + EPS)
        for c in range(n_lane_tiles):
            cs = slice(c * LANES, (c + 1) * LANES)
            hf = x_ref[:, cs] * rs * g_ref[:, cs]
            h_ref[0, :, cs] = hf.astype(BF16)
            for slot, d in ((1, 4), (2, 16)):
                by_class = jnp.swapaxes(hf.reshape(tm // d, d, LANES), 0, 1)
                h_ref[slot, :, cs] = by_class.reshape(tm, LANES).astype(BF16)

    slot = (j >= T_G1).astype(jnp.int32) + (j >= T_G2).astype(jnp.int32)
    q_scale = scale_ref[j]

    def step(epi):
        lane = lax.broadcasted_iota(jnp.int32, (1, LANES), 1)
        first_half = lane % HEAD_DIM < ROPE_HALF
        if epi == EPI_SIGMOID:
            bias = mb_ref[pl.ds(jnp.maximum(wtile_ref[j] - FIRST_MERGE_TILE, 0), 1), :]
        block = PROJ_ROWS // 2 if epi == EPI_ROPE else PROJ_ROWS
        for r0 in range(0, tm, block):
            rows = pl.ds(r0, block)
            acc = jnp.dot(h_ref[slot, rows, :], w_ref[...], preferred_element_type=F32)
            if epi == EPI_ROPE:
                c, s = cos_ref[rows, :], sin_ref[rows, :]
            for t in range(WIDTH // LANES):
                lanes = slice(t * LANES, (t + 1) * LANES)
                a = acc[:, lanes]
                if epi == EPI_ROPE:
                    partner = jnp.where(first_half, pltpu.roll(a, LANES - ROPE_HALF, 1),
                                        pltpu.roll(a, ROPE_HALF, 1))
                    a = (a * c + partner * s) * q_scale
                elif epi == EPI_SCALE:
                    a = a * q_scale
                elif epi == EPI_SIGMOID:
                    a = _sigmoid(a + bias[:, lanes])
                else:
                    a = a * _sigmoid(a)
                o_ref[rows, lanes] = a.astype(BF16)

    for epi in (EPI_ROPE, EPI_SCALE, EPI_SIGMOID, EPI_SILU):
        pl.when(epi_ref[j] == epi)(functools.partial(step, epi))


def _proj(x2d, gain, w, merge_bias, cos_t, sin_t, *, batch, seq):
    rows, dm = x2d.shape
    tm = seq
    prefetch = (jnp.asarray([st[0] for st in STEPS], jnp.int32),
                jnp.asarray([st[2] for st in STEPS], jnp.int32),
                jnp.asarray([st[3] for st in STEPS], F32))
    mb_rows = merge_bias.reshape(-1, WIDTH)

    def slot_of(j):
        return (j >= T_G1).astype(jnp.int32) + (j >= T_G2).astype(jnp.int32)

    tab_spec = pl.BlockSpec((None, tm, LANES), lambda i, j, *_: (slot_of(j), 0, 0))
    return pl.pallas_call(
        functools.partial(_proj_kernel, tm=tm),
        out_shape=jax.ShapeDtypeStruct((batch, N_STEPS, seq, WIDTH), BF16),
        grid_spec=pltpu.PrefetchScalarGridSpec(
            num_scalar_prefetch=len(prefetch),
            grid=(rows // tm, N_STEPS),
            in_specs=[
                pl.BlockSpec((tm, dm), lambda i, j, *_: (i, 0)),
                pl.BlockSpec((1, dm), lambda i, j, *_: (0, 0)),
                pl.BlockSpec((dm, WIDTH), lambda i, j, wt, ep, sc: (0, wt[j])),
                pl.BlockSpec(mb_rows.shape, lambda i, j, *_: (0, 0)),
                tab_spec, tab_spec,
            ],
            out_specs=pl.BlockSpec((None, None, tm, WIDTH), lambda i, j, *_: (i, j, 0, 0)),
            scratch_shapes=[pltpu.VMEM((3, tm, dm), BF16)]),
        compiler_params=pltpu.CompilerParams(
            dimension_semantics=("arbitrary", "arbitrary"),
            vmem_limit_bytes=VMEM_LIMIT),
        name="proj",
    )(*prefetch, x2d, gain, w, mb_rows, cos_t, sin_t)


def _rope_tables(seq):
    pos = np.arange(seq, dtype=np.float64)
    inv = ROPE_THETA ** (-np.arange(ROPE_HALF, dtype=np.float64) * 2.0 / ROPE_DIM)
    ang = pos[:, None] * inv[None, :]
    cos, sin = np.cos(ang), np.sin(ang)
    rest = HEAD_DIM - ROPE_DIM
    c = np.concatenate([cos, cos, np.ones((seq, rest))] * (LANES // HEAD_DIM), axis=-1)
    s = np.concatenate([-sin, sin, np.zeros((seq, rest))] * (LANES // HEAD_DIM), axis=-1)

    def orders(t):
        out = [t]
        for d in SLOT_DIL[1:]:
            out.append(t.reshape(seq // d, d, LANES).transpose(1, 0, 2).reshape(seq, LANES))
        return jnp.asarray(np.stack(out), F32)

    return orders(c), orders(s)


def _kv_kernel(x_ref, g_ref, w_ref, o_ref):
    xf = x_ref[...]
    ms = jnp.mean(xf * xf, axis=-1, keepdims=True)
    h = (xf * lax.rsqrt(ms + EPS) * g_ref[...]).astype(BF16)
    o_ref[...] = jnp.dot(h, w_ref[...].astype(BF16), preferred_element_type=F32).astype(BF16)


def _kv_proj(mem2d, gain, w, *, tm):
    rows, dm = mem2d.shape
    n_out = w.shape[1]
    return pl.pallas_call(
        _kv_kernel,
        out_shape=jax.ShapeDtypeStruct((rows, n_out), BF16),
        grid=(rows // tm,),
        in_specs=[pl.BlockSpec((tm, dm), lambda i: (i, 0)),
                  pl.BlockSpec((1, dm), lambda i: (0, 0)),
                  pl.BlockSpec((dm, n_out), lambda i: (0, 0))],
        out_specs=pl.BlockSpec((tm, n_out), lambda i: (i, 0)),
        compiler_params=pltpu.CompilerParams(
            dimension_semantics=("arbitrary",), vmem_limit_bytes=VMEM_LIMIT),
        name="kv_proj",
    )(mem2d, gain, w)


def _dil_kernel(q0, k0, v0, q1, k1, v1, q2, k2, v2, o_ref,
                og0, lg0, og1, lg1, bias_scr, s_scr, p_scr, *, seq, reach):
    left = lax.broadcasted_iota(jnp.int32, (1, LANES), 1) < HEAD_DIM
    keep_a = jnp.where(left, 1.0, 0.0).astype(BF16)
    keep_b = jnp.where(left, 0.0, 1.0).astype(BF16)
    kw_max = 2 * QBLK

    @pl.when(jnp.logical_and(pl.program_id(0) == 0, pl.program_id(1) == 0))
    def _():
        rel = (lax.broadcasted_iota(jnp.int32, (QBLK, kw_max), 0)
               - lax.broadcasted_iota(jnp.int32, (QBLK, kw_max), 1))
        for n in range(3):
            bias_scr[n] = jnp.where(jnp.abs(rel + n * reach) <= reach, 0.0, NEG)

    def pair_tile(col_a, col_b):
        return jnp.where(left, jnp.broadcast_to(col_a, (QBLK, LANES)),
                         jnp.broadcast_to(col_b, (QBLK, LANES)))

    def keep(o_scr, l_scr, length):
        def finish(r, row0, acc, m, den):
            rows = pl.ds(r * length + row0, QBLK)
            o_scr[rows, :] = acc / den
            l_scr[rows, :] = m + jnp.log2(den)
        return finish

    def combine(r, row0, acc2, m2, den2):
        rows0 = pl.ds(r, QBLK, stride=16)
        rows1 = pl.ds((r % 4) * (seq // 4) + r // 4, QBLK, stride=4)
        l0, l1 = lg0[rows0, :], lg1[rows1, :]
        m = jnp.maximum(jnp.maximum(l0, l1), m2)
        w0, w1, w2 = jnp.exp2(l0 - m), jnp.exp2(l1 - m), jnp.exp2(m2 - m)
        num = w0 * og0[rows0, :] + w1 * og1[rows1, :] + w2 * acc2
        o_ref[r] = (num / (w0 + w1 + w2 * den2)).astype(BF16)

    tiles = []
    for (q_ref, k_ref, v_ref), d, finish in (
            ((q0, k0, v0), 1, keep(og0, lg0, seq)),
            ((q1, k1, v1), 4, keep(og1, lg1, seq // 4)),
            ((q2, k2, v2), 16, combine)):
        length = seq // d
        kw = min(kw_max, length)
        for bi in range(seq // QBLK):
            r, row0 = divmod(bi * QBLK, length)
            ks = min(max(row0 - reach, 0), length - kw)
            tiles.append((q_ref, k_ref, v_ref, r, row0, ks, kw, finish))
    chunks = [tiles[i:i + DIL_CHUNK] for i in range(0, len(tiles), DIL_CHUNK)]

    def scores(c):
        for t, (q_ref, k_ref, _, r, row0, ks, kw, _) in enumerate(chunks[c]):
            q2d = q_ref[r, row0:row0 + QBLK, :]
            qq = jnp.concatenate([q2d * keep_a, q2d * keep_b], axis=0)
            s_scr[c % N_SBUF, t, :, :kw] = lax.dot_general(
                qq, k_ref[r, ks:ks + kw, :], (((1,), (1,)), ((), ())),
                preferred_element_type=F32)

    def softmax_pv(c):
        for t, (_, _, v_ref, r, row0, ks, kw, finish) in enumerate(chunks[c]):
            bias = bias_scr[(row0 - ks) // reach, :, :kw]
            s = s_scr[c % N_SBUF, t, :, :kw] + jnp.concatenate([bias, bias], axis=0)
            m = jnp.max(s, axis=-1, keepdims=True)
            p = jnp.exp2(s - m)
            den = jnp.sum(p, axis=-1, keepdims=True)
            p_scr[c % 2, t, :, :kw] = p.astype(BF16)
            pv = jnp.dot(p_scr[c % 2, t, :, :kw], v_ref[r, ks:ks + kw, :],
                         preferred_element_type=F32)
            finish(r, row0, jnp.where(left, pv[:QBLK], pv[QBLK:]),
                   pair_tile(m[:QBLK], m[QBLK:]), pair_tile(den[:QBLK], den[QBLK:]))

    for c in range(min(DIL_AHEAD, len(chunks))):
        scores(c)
    for c in range(len(chunks)):
        if c + DIL_AHEAD < len(chunks):
            scores(c + DIL_AHEAD)
        softmax_pv(c)


def _dil_attn(proj, *, reach):
    batch, _, seq, _ = proj.shape
    n_hp = WIDTH // LANES

    def spec(first_tile, d, kind):
        return pl.BlockSpec((None, None, d, seq // d, LANES),
                            lambda b, hp: (b, first_tile + kind, 0, 0, hp))

    views = {d: proj.reshape(batch, N_STEPS, d, seq // d, WIDTH) for d in SLOT_DIL}
    f32_rows = pltpu.VMEM((seq, LANES), F32)
    return pl.pallas_call(
        functools.partial(_dil_kernel, seq=seq, reach=reach),
        out_shape=jax.ShapeDtypeStruct((batch, 16, seq // 16, WIDTH), BF16),
        grid=(batch, n_hp),
        in_specs=[spec(t0, d, kind) for t0, d in zip((T_G0, T_G1, T_G2), SLOT_DIL)
                  for kind in range(3)],
        out_specs=pl.BlockSpec((None, 16, seq // 16, LANES), lambda b, hp: (b, 0, 0, hp)),
        scratch_shapes=[f32_rows] * 4 + [
            pltpu.VMEM((3, QBLK, 2 * QBLK), F32),
            pltpu.VMEM((N_SBUF, DIL_CHUNK, 2 * QBLK, 2 * QBLK), F32),
            pltpu.VMEM((2, DIL_CHUNK, 2 * QBLK, 2 * QBLK), BF16)],
        compiler_params=pltpu.CompilerParams(
            dimension_semantics=("arbitrary", "arbitrary"),
            vmem_limit_bytes=VMEM_LIMIT),
        name="dil_attn",
    )(*[views[d] for d in SLOT_DIL for _ in range(3)])


def _na_kernel(q_ref, k_ref, v_ref, rpb_ref, qm_ref, km_ref, vm_ref,
               o_ref, om_ref, bias_scr, s_scr, p_scr, sm_scr, pm_scr, rdm_scr, *, rows):
    kr = min(NA_ROWS, rows)
    nk = kr * GRID_W
    n_var = 2 * NA_ROWS - kr
    left = lax.broadcasted_iota(jnp.int32, (1, LANES), 1) < HEAD_DIM
    keep_a = jnp.where(left, 1.0, 0.0).astype(BF16)
    keep_b = jnp.where(left, 0.0, 1.0).astype(BF16)

    @pl.when(pl.program_id(1) == 0)
    def _():
        qc = lax.broadcasted_iota(jnp.int32, (GRID_W, LANES), 0)
        kc = lax.broadcasted_iota(jnp.int32, (GRID_W, LANES), 1) % GRID_W
        rel = jnp.clip(kc - qc, -(NA_COLS - 1), NA_COLS - 1) + NA_COLS - 1
        c_start = jnp.clip(qc - NA_COLS // 2, 0, GRID_W - NA_COLS)
        valid = jnp.logical_and(kc >= c_start, kc < c_start + NA_COLS)
        for hh in range(LANES // HEAD_DIM):
            head = pl.program_id(0) * (LANES // HEAD_DIM) + hh
            for a in range(2 * NA_ROWS - 1):
                row = rpb_ref[pl.ds(head * (2 * NA_ROWS - 1) + a, 1), :]
                blk = jnp.take_along_axis(jnp.broadcast_to(row, (GRID_W, LANES)), rel, axis=1)
                blk = jnp.where(valid, blk * LOG2E, NEG)
                for j in range(kr):
                    if 0 <= a - j < n_var:
                        half = slice((j % 2) * GRID_W, (j % 2 + 1) * GRID_W)
                        bias_scr[hh, a - j, :, j * GRID_W:(j + 1) * GRID_W] = blk[:, half]

    def window(r):
        r_start = min(max(r - kr // 2, 0), rows - kr)
        return r * GRID_W, r_start * GRID_W, r_start - r + (NA_ROWS - 1)

    def scores(c):
        for t in range(NA_CHUNK):
            row0, ks, _ = window(c * NA_CHUNK + t)
            q2d = q_ref[row0:row0 + GRID_W, :]
            qq = jnp.concatenate([q2d * keep_a, q2d * keep_b], axis=0)
            s_scr[c % 2, t] = lax.dot_general(qq, k_ref[ks:ks + nk, :], (((1,), (1,)), ((), ())),
                                              preferred_element_type=F32)

    def softmax_pv(c):
        for t in range(NA_CHUNK):
            row0, ks, variant = window(c * NA_CHUNK + t)
            dens = []
            for hh in range(LANES // HEAD_DIM):
                rows_h = slice(hh * GRID_W, (hh + 1) * GRID_W)
                s = s_scr[c % 2, t, rows_h, :] + bias_scr[hh, variant]
                m = jnp.max(s, axis=-1, keepdims=True)
                p = jnp.exp2(s - m)
                dens.append(jnp.sum(p, axis=-1, keepdims=True))
                p_scr[t, rows_h, :] = p.astype(BF16)
            pv = jnp.dot(p_scr[t], v_ref[ks:ks + nk, :], preferred_element_type=F32)
            den_t = jnp.where(left, jnp.broadcast_to(dens[0], (GRID_W, LANES)),
                              jnp.broadcast_to(dens[1], (GRID_W, LANES)))
            o_ref[row0:row0 + GRID_W, :] = (
                jnp.where(left, pv[:GRID_W], pv[GRID_W:]) / den_t).astype(BF16)

    sm_scr[...] = lax.dot_general(qm_ref[...], km_ref[...], (((1,), (1,)), ((), ())),
                                  preferred_element_type=F32)

    def mem_softmax(i):
        rows_i = slice(i * QBLK, (i + 1) * QBLK)
        s = sm_scr[rows_i, :]
        m = jnp.max(s, axis=-1, keepdims=True)
        p = jnp.exp2(s - m)
        den = jnp.sum(p, axis=-1, keepdims=True)
        pm_scr[rows_i, :] = p.astype(BF16)
        rdm_scr[rows_i, :] = jnp.broadcast_to(1.0 / den, (QBLK, LANES))

    n_chunks = rows // NA_CHUNK
    mem_tiles = qm_ref.shape[0] // QBLK
    scores(0)
    for c in range(n_chunks):
        if c + 1 < n_chunks:
            scores(c + 1)
        softmax_pv(c)
        for i in range(c * mem_tiles // n_chunks, (c + 1) * mem_tiles // n_chunks):
            mem_softmax(i)
    pv = jnp.dot(pm_scr[...], vm_ref[...], preferred_element_type=F32)
    om_ref[...] = (pv * rdm_scr[...]).astype(BF16)


def _na_mem_attn(proj, rpb, kv_m):
    batch, _, seq, _ = proj.shape
    rows = seq // GRID_W
    kr = min(NA_ROWS, rows)
    nk = kr * GRID_W
    n_hp = WIDTH // LANES
    assert rpb.shape == (WIDTH // HEAD_DIM, 2 * NA_ROWS - 1, 2 * NA_COLS - 1)

    def in_spec(tile):
        return pl.BlockSpec((None, None, seq, LANES), lambda hp, b: (b, tile, 0, hp))

    rpb_rows = jnp.pad(rpb.reshape(-1, rpb.shape[-1]).astype(F32),
                       ((0, 0), (0, LANES - rpb.shape[-1])))
    mem_len = kv_m.shape[1]
    assert MEM_HEADS == n_hp and MEM_HEAD_DIM == LANES
    out_spec = pl.BlockSpec((None, seq, LANES), lambda hp, b: (b, 0, hp))
    return pl.pallas_call(
        functools.partial(_na_kernel, rows=rows),
        out_shape=(jax.ShapeDtypeStruct((batch, seq, WIDTH), BF16),
                   jax.ShapeDtypeStruct((batch, seq, WIDTH), BF16)),
        grid=(n_hp, batch),
        in_specs=[in_spec(T_NA), in_spec(T_NA + 1), in_spec(T_NA + 2),
                  pl.BlockSpec(rpb_rows.shape, lambda hp, b: (0, 0)),
                  in_spec(T_MEMQ),
                  pl.BlockSpec((None, mem_len, LANES), lambda hp, b: (b, 0, hp)),
                  pl.BlockSpec((None, mem_len, LANES), lambda hp, b: (b, 0, MEM_HEADS + hp))],
        out_specs=(out_spec, out_spec),
        scratch_shapes=[pltpu.VMEM((LANES // HEAD_DIM, 2 * NA_ROWS - kr, GRID_W, nk), F32),
                        pltpu.VMEM((2, NA_CHUNK, 2 * GRID_W, nk), F32),
                        pltpu.VMEM((NA_CHUNK, 2 * GRID_W, nk), BF16),
                        pltpu.VMEM((seq, mem_len), F32), pltpu.VMEM((seq, mem_len), BF16),
                        pltpu.VMEM((seq, LANES), F32)],
        compiler_params=pltpu.CompilerParams(
            dimension_semantics=("arbitrary", "arbitrary"), vmem_limit_bytes=VMEM_LIMIT),
        name="na_mem_attn",
    )(proj, proj, proj, rpb_rows, proj, kv_m, kv_m)


def _merge_kernel(oa, ga, ob, oc, gb, gc, za0, za1, zb0, zb1, zc0, zc1,
                  wa_f32, wb_f32, wc_f32, wo_f32, pn_ref, x_ref, out_ref,
                  u_scr, wa_ref, wb_ref, wc_ref, wo_ref, *, tm):
    @pl.when(pl.program_id(0) == 0)
    def _():
        for src, dst in ((wa_f32, wa_ref), (wb_f32, wb_ref), (wc_f32, wc_ref), (wo_f32, wo_ref)):
            dst[...] = src[...].astype(BF16)

    def silu_gated(o, silu_g):
        return o.astype(F32) * silu_g.astype(F32)

    seg = MERGE_ROWS // 16
    for blk in range(tm // MERGE_ROWS):
        rows = slice(blk * MERGE_ROWS, (blk + 1) * MERGE_ROWS)

        def project(u, w_ref, gate_lo, gate_hi):
            yb = jnp.dot(u.astype(BF16), w_ref[...], preferred_element_type=F32)
            gate = jnp.concatenate([gate_lo[rows, :], gate_hi[rows, :]], axis=-1)
            return gate.astype(F32) * yb

        p16_rows = slice(blk * seg, (blk + 1) * seg)
        ua = silu_gated(oa[:, p16_rows, :].reshape(MERGE_ROWS, WIDTH),
                        ga[:, p16_rows, :].reshape(MERGE_ROWS, WIDTH))
        for r in range(16):
            for c in range(WIDTH // LANES):
                u_scr[blk, c, pl.ds(r, seg, stride=16), :] = ua[r * seg:(r + 1) * seg,
                                                                c * LANES:(c + 1) * LANES]
        ua = jnp.concatenate([u_scr[blk, c] for c in range(WIDTH // LANES)], axis=-1)

        y = (project(ua, wa_ref, za0, za1)
             + project(silu_gated(ob[rows, :], gb[rows, :]), wb_ref, zb0, zb1)
             + project(silu_gated(oc[rows, :], gc[rows, :]), wc_ref, zc0, zc1))
        z = jnp.dot(y.astype(BF16), wo_ref[...], preferred_element_type=F32)
        ms = jnp.mean(z * z, axis=-1, keepdims=True)
        out_ref[rows, :] = x_ref[rows, :] + z * lax.rsqrt(ms + EPS) * pn_ref[...]


def _merge(x2d, proj, out_a16, out_b, out_c, wa, wb, wc, wo, post_norm, *, tm):
    rows, dm = x2d.shape
    batch, _, seq, _ = proj.shape
    per_b = seq // tm
    seg = tm // 16
    proj16 = proj.reshape(batch, N_STEPS, 16, seq // 16, WIDTH)

    def p16_spec(tile):
        return pl.BlockSpec((None, None, 16, seg, WIDTH),
                            lambda i: (i // per_b, tile, 0, i % per_b, 0))

    def nat_spec(tile):
        return pl.BlockSpec((None, None, tm, WIDTH), lambda i: (i // per_b, tile, i % per_b, 0))

    def rows_spec(width):
        return pl.BlockSpec((tm, width), lambda i: (i, 0))

    def full(shape):
        return pl.BlockSpec(shape, lambda i: (0, 0), pipeline_mode=pl.Buffered(1))

    return pl.pallas_call(
        functools.partial(_merge_kernel, tm=tm),
        out_shape=jax.ShapeDtypeStruct((rows, dm), F32),
        grid=(rows // tm,),
        in_specs=[pl.BlockSpec((None, 16, seg, WIDTH), lambda i: (i // per_b, 0, i % per_b, 0)),
                  p16_spec(T_GATE_A), rows_spec(WIDTH), rows_spec(WIDTH),
                  nat_spec(T_GATE_B), nat_spec(T_GATE_B + 1),
                  nat_spec(T_MERGE_A), nat_spec(T_MERGE_A + 1),
                  nat_spec(T_MERGE_B), nat_spec(T_MERGE_B + 1),
                  nat_spec(T_MERGE_B + 2), nat_spec(T_MERGE_B + 3),
                  full(wa.shape), full(wb.shape), full(wc.shape),
                  full(wo.shape), full(post_norm.shape), rows_spec(dm)],
        out_specs=pl.BlockSpec((tm, dm), lambda i: (i, 0)),
        scratch_shapes=[pltpu.VMEM((tm // MERGE_ROWS, WIDTH // LANES, MERGE_ROWS, LANES), F32)]
        + [pltpu.VMEM(w.shape, BF16) for w in (wa, wb, wc, wo)],
        compiler_params=pltpu.CompilerParams(
            dimension_semantics=("arbitrary",), vmem_limit_bytes=VMEM_LIMIT),
        name="merge",
    )(out_a16, proj16, out_b, out_c, proj, proj, proj, proj, proj, proj, proj, proj,
      wa, wb, wc, wo, post_norm, x2d)


def kernel(x, mem, pre_norm, w_in, merge_bias, na_rpb, mem_norm, w_mem_kv,
           w_branch_a, w_branch_b, w_branch_c, w_out, post_norm):
    b, s, dm = x.shape
    depth = pre_norm.shape[0]
    reach = (DIL_CONFIGS[0][0] // 2) // DIL_CONFIGS[0][1]
    assert all((w // 2) // d == reach for w, d in DIL_CONFIGS) and 2 * reach == QBLK
    assert tuple(d for _, d in DIL_CONFIGS) == SLOT_DIL
    cos_t, sin_t = _rope_tables(s)
    for layer in range(depth):
        x2d = x.reshape(b * s, dm)
        proj = _proj(x2d, pre_norm[layer][None], w_in[layer].astype(BF16), merge_bias[layer],
                     cos_t, sin_t,
                     batch=b, seq=s)

        out_a16 = _dil_attn(proj, reach=reach)
        mem2d = mem.reshape(b * mem.shape[1], dm)
        kv_m = _kv_proj(mem2d, mem_norm[layer][None], w_mem_kv[layer], tm=1024)
        out_b, out_c = _na_mem_attn(proj, na_rpb[layer], kv_m.reshape(b, mem.shape[1], -1))

        y = _merge(x2d, proj, out_a16, out_b.reshape(b * s, WIDTH), out_c.reshape(b * s, WIDTH),
                   w_branch_a[layer], w_branch_b[layer], w_branch_c[layer],
                   w_out[layer], post_norm[layer][None], tm=2 * MERGE_ROWS)
        x = y.reshape(b, s, dm)
    return x
```

```python
import functools
import math

import jax
import jax.numpy as jnp
import numpy as np
from jax import lax
from jax.experimental import pallas as pl
from jax.experimental.pallas import tpu as pltpu

D_MODEL = 1024
HEAD_DIM = 64
DIL_CONFIGS = ((128, 1), (512, 4), (2048, 16))
WIDTH = 512
NA_ROWS = 8
NA_COLS = 16
GRID_W = 64
MEM_HEADS = 4
MEM_HEAD_DIM = 128
ROPE_THETA = 500000.0
ROPE_DIM = HEAD_DIM // 4
ROPE_HALF = ROPE_DIM // 2
EPS = 1e-6
NEG = -1e30
LOG2E = math.log2(math.e)

LANES = 128
QBLK = 128
DIL_CHUNK = 1
DIL_AHEAD = 1
N_SBUF = DIL_AHEAD + 1
NA_CHUNK = 8
PROJ_ROWS = 256
MERGE_ROWS = 256
VMEM_LIMIT = 56 * 1024 * 1024

_QS = HEAD_DIM ** -0.5 * LOG2E
_MS = MEM_HEAD_DIM ** -0.5 * LOG2E
EPI_ROPE, EPI_SCALE, EPI_SIGMOID, EPI_SILU = range(4)
STEPS = (
    (0, 0, EPI_ROPE, _QS), (1, 0, EPI_ROPE, 1.0), (2, 0, EPI_SCALE, 1.0),
    (9, 0, EPI_SCALE, _QS), (10, 0, EPI_SCALE, 1.0), (11, 0, EPI_SCALE, 1.0),
    (12, 0, EPI_SCALE, _MS),
    (14, 0, EPI_SILU, 1.0), (15, 0, EPI_SILU, 1.0),
    (18, 0, EPI_SIGMOID, 1.0), (19, 0, EPI_SIGMOID, 1.0),
    (20, 0, EPI_SIGMOID, 1.0), (21, 0, EPI_SIGMOID, 1.0),
    (16, 0, EPI_SIGMOID, 1.0), (17, 0, EPI_SIGMOID, 1.0),
    (3, 1, EPI_ROPE, _QS), (4, 1, EPI_ROPE, 1.0), (5, 1, EPI_SCALE, 1.0),
    (6, 2, EPI_ROPE, _QS), (7, 2, EPI_ROPE, 1.0), (8, 2, EPI_SCALE, 1.0),
    (13, 2, EPI_SILU, 1.0),
)
FIRST_MERGE_TILE = 16
N_STEPS = len(STEPS)
T_G0, T_NA, T_MEMQ, T_GATE_B, T_MERGE_B, T_MERGE_A = 0, 3, 6, 7, 9, 13
T_G1, T_G2, T_GATE_A = 15, 18, 21
SLOT_DIL = (1, 4, 16)

BF16 = jnp.bfloat16
F32 = jnp.float32


def _sigmoid(z):
    return 0.5 * jnp.tanh(0.5 * z) + 0.5


def _proj_kernel(wtile_ref, kind_ref, scale_ref, x_ref, g_ref, w_ref, mb_ref,
                 cos_ref, sin_ref, o_ref, h_ref, wb_ref, *, tm):
    j = pl.program_id(1)
    n_lane_tiles = x_ref.shape[1] // LANES

    @pl.when(j == 0)
    def _():
        rb = 256
        ssq = [jnp.sum(jnp.square(x_ref[r0:r0 + rb, :]), axis=-1, keepdims=True)
               for r0 in range(0, tm, rb)]
        rs = lax.rsqrt(jnp.concatenate(ssq, axis=0) * (1.0 / x_ref.shape[1]) + EPS)
        for c in range(n_lane_tiles):
            cs = slice(c * LANES, (c + 1) * LANES)
            h_ref[:, cs] = (x_ref[:, cs] * rs * g_ref[:, cs]).astype(BF16)

    q_scale = scale_ref[j]

    def step(epi, d):
        wb_ref[...] = w_ref[...].astype(BF16)
        lane = lax.broadcasted_iota(jnp.int32, (1, LANES), 1)
        first_half = lane % HEAD_DIM < ROPE_HALF
        if epi == EPI_SIGMOID:
            bias = mb_ref[pl.ds(jnp.maximum(wtile_ref[j] - FIRST_MERGE_TILE, 0), 1), :]
        block = PROJ_ROWS // 2 if (epi == EPI_ROPE and d <= 4) else PROJ_ROWS
        per_class = block // d
        for r0 in range(0, tm, block):
            rows = pl.ds(r0, block)
            acc = jnp.dot(h_ref[rows, :], wb_ref[...], preferred_element_type=F32)
            if epi == EPI_ROPE:
                c, s = cos_ref[rows, :], sin_ref[rows, :]
            for t in range(WIDTH // LANES):
                lanes = slice(t * LANES, (t + 1) * LANES)
                a = acc[:, lanes]
                if epi == EPI_ROPE:
                    partner = jnp.where(first_half, pltpu.roll(a, LANES - ROPE_HALF, 1),
                                        pltpu.roll(a, ROPE_HALF, 1))
                    a = (a * c + partner * s) * q_scale
                elif epi == EPI_SCALE:
                    a = a * q_scale
                elif epi == EPI_SIGMOID:
                    a = _sigmoid(a + bias[:, lanes])
                else:
                    a = a * _sigmoid(a)
                if d == 1:
                    o_ref[rows, lanes] = a.astype(BF16)
                else:
                    by_class = jnp.swapaxes(a.reshape(per_class, d, LANES), 0, 1)
                    for r in range(d):
                        o_ref[pl.ds(r * (tm // d) + r0 // d, per_class), lanes] = (
                            by_class[r].astype(BF16))

    for kind in sorted({st[2] * len(SLOT_DIL) + st[1] for st in STEPS}):
        epi, slot = divmod(kind, len(SLOT_DIL))
        pl.when(kind_ref[j] == kind)(functools.partial(step, epi, SLOT_DIL[slot]))


def _proj(x2d, gain, w, merge_bias, cos_t, sin_t, *, batch, seq):
    rows, dm = x2d.shape
    tm = seq
    prefetch = (jnp.asarray([st[0] for st in STEPS], jnp.int32),
                jnp.asarray([st[2] * len(SLOT_DIL) + st[1] for st in STEPS], jnp.int32),
                jnp.asarray([st[3] for st in STEPS], F32))
    mb_rows = merge_bias.reshape(-1, WIDTH)
    tab_spec = pl.BlockSpec((tm, LANES), lambda i, j, *_: (0, 0))
    return pl.pallas_call(
        functools.partial(_proj_kernel, tm=tm),
        out_shape=jax.ShapeDtypeStruct((batch, N_STEPS, seq, WIDTH), BF16),
        grid_spec=pltpu.PrefetchScalarGridSpec(
            num_scalar_prefetch=len(prefetch),
            grid=(rows // tm, N_STEPS),
            in_specs=[
                pl.BlockSpec((tm, dm), lambda i, j, *_: (i, 0)),
                pl.BlockSpec((1, dm), lambda i, j, *_: (0, 0)),
                pl.BlockSpec((dm, WIDTH), lambda i, j, wt, kd, sc: (0, wt[j])),
                pl.BlockSpec(mb_rows.shape, lambda i, j, *_: (0, 0)),
                tab_spec, tab_spec,
            ],
            out_specs=pl.BlockSpec((None, None, tm, WIDTH), lambda i, j, *_: (i, j, 0, 0)),
            scratch_shapes=[pltpu.VMEM((tm, dm), BF16), pltpu.VMEM((dm, WIDTH), BF16)]),
        compiler_params=pltpu.CompilerParams(
            dimension_semantics=("arbitrary", "arbitrary"),
            vmem_limit_bytes=VMEM_LIMIT),
        name="proj",
    )(*prefetch, x2d, gain, w, mb_rows, cos_t, sin_t)


def _rope_tables(seq):
    pos = np.arange(seq, dtype=np.float64)
    inv = ROPE_THETA ** (-np.arange(ROPE_HALF, dtype=np.float64) * 2.0 / ROPE_DIM)
    ang = pos[:, None] * inv[None, :]
    cos, sin = np.cos(ang), np.sin(ang)
    rest = HEAD_DIM - ROPE_DIM
    c = np.concatenate([cos, cos, np.ones((seq, rest))] * (LANES // HEAD_DIM), axis=-1)
    s = np.concatenate([-sin, sin, np.zeros((seq, rest))] * (LANES // HEAD_DIM), axis=-1)

    return jnp.asarray(c, F32), jnp.asarray(s, F32)


def _kv_kernel(x_ref, g_ref, w_ref, o_ref):
    xf = x_ref[...]
    ms = jnp.mean(xf * xf, axis=-1, keepdims=True)
    h = (xf * lax.rsqrt(ms + EPS) * g_ref[...]).astype(BF16)
    o_ref[...] = jnp.dot(h, w_ref[...].astype(BF16), preferred_element_type=F32).astype(BF16)


def _kv_proj(mem2d, gain, w, *, tm):
    rows, dm = mem2d.shape
    n_out = w.shape[1]
    return pl.pallas_call(
        _kv_kernel,
        out_shape=jax.ShapeDtypeStruct((rows, n_out), BF16),
        grid=(rows // tm,),
        in_specs=[pl.BlockSpec((tm, dm), lambda i: (i, 0)),
                  pl.BlockSpec((1, dm), lambda i: (0, 0)),
                  pl.BlockSpec((dm, n_out), lambda i: (0, 0))],
        out_specs=pl.BlockSpec((tm, n_out), lambda i: (i, 0)),
        compiler_params=pltpu.CompilerParams(
            dimension_semantics=("arbitrary",), vmem_limit_bytes=VMEM_LIMIT),
        name="kv_proj",
    )(mem2d, gain, w)


def _dil_kernel(q0, k0, v0, q1, k1, v1, q2, k2, v2, o_ref,
                og0, lg0, og1, lg1, bias_scr, s_scr, p_scr, *, seq, reach):
    left = lax.broadcasted_iota(jnp.int32, (1, LANES), 1) < HEAD_DIM
    keep_a = jnp.where(left, 1.0, 0.0).astype(BF16)
    keep_b = jnp.where(left, 0.0, 1.0).astype(BF16)
    kw_max = 2 * QBLK

    @pl.when(jnp.logical_and(pl.program_id(0) == 0, pl.program_id(1) == 0))
    def _():
        rel = (lax.broadcasted_iota(jnp.int32, (QBLK, kw_max), 0)
               - lax.broadcasted_iota(jnp.int32, (QBLK, kw_max), 1))
        for n in range(3):
            bias_scr[n] = jnp.where(jnp.abs(rel + n * reach) <= reach, 0.0, NEG)

    def pair_tile(col_a, col_b):
        return jnp.where(left, jnp.broadcast_to(col_a, (QBLK, LANES)),
                         jnp.broadcast_to(col_b, (QBLK, LANES)))

    def keep(o_scr, l_scr, length):
        def finish(r, row0, acc, m, den):
            rows = pl.ds(r * length + row0, QBLK)
            o_scr[rows, :] = acc / den
            l_scr[rows, :] = m + jnp.log2(den)
        return finish

    def combine(r, row0, acc2, m2, den2):
        rows0 = pl.ds(r, QBLK, stride=16)
        rows1 = pl.ds((r % 4) * (seq // 4) + r // 4, QBLK, stride=4)
        l0, l1 = lg0[rows0, :], lg1[rows1, :]
        m = jnp.maximum(jnp.maximum(l0, l1), m2)
        w0, w1, w2 = jnp.exp2(l0 - m), jnp.exp2(l1 - m), jnp.exp2(m2 - m)
        num = w0 * og0[rows0, :] + w1 * og1[rows1, :] + w2 * acc2
        o_ref[r] = (num / (w0 + w1 + w2 * den2)).astype(BF16)

    tiles = []
    for (q_ref, k_ref, v_ref), d, finish in (
            ((q0, k0, v0), 1, keep(og0, lg0, seq)),
            ((q1, k1, v1), 4, keep(og1, lg1, seq // 4)),
            ((q2, k2, v2), 16, combine)):
        length = seq // d
        kw = min(kw_max, length)
        for bi in range(seq // QBLK):
            r, row0 = divmod(bi * QBLK, length)
            ks = min(max(row0 - reach, 0), length - kw)
            tiles.append((q_ref, k_ref, v_ref, r, row0, ks, kw, finish))
    chunks = [tiles[i:i + DIL_CHUNK] for i in range(0, len(tiles), DIL_CHUNK)]

    def scores(c):
        for t, (q_ref, k_ref, _, r, row0, ks, kw, _) in enumerate(chunks[c]):
            q2d = q_ref[r, row0:row0 + QBLK, :]
            qq = jnp.concatenate([q2d * keep_a, q2d * keep_b], axis=0)
            s_scr[c % N_SBUF, t, :, :kw] = lax.dot_general(
                qq, k_ref[r, ks:ks + kw, :], (((1,), (1,)), ((), ())),
                preferred_element_type=F32)

    def softmax_pv(c):
        for t, (_, _, v_ref, r, row0, ks, kw, finish) in enumerate(chunks[c]):
            bias = bias_scr[(row0 - ks) // reach, :, :kw]
            s = s_scr[c % N_SBUF, t, :, :kw] + jnp.concatenate([bias, bias], axis=0)
            m = jnp.max(s, axis=-1, keepdims=True)
            p = jnp.exp2(s - m)
            den = jnp.sum(p, axis=-1, keepdims=True)
            p_scr[c % 2, t, :, :kw] = p.astype(BF16)
            pv = jnp.dot(p_scr[c % 2, t, :, :kw], v_ref[r, ks:ks + kw, :],
                         preferred_element_type=F32)
            finish(r, row0, jnp.where(left, pv[:QBLK], pv[QBLK:]),
                   pair_tile(m[:QBLK], m[QBLK:]), pair_tile(den[:QBLK], den[QBLK:]))

    for c in range(min(DIL_AHEAD, len(chunks))):
        scores(c)
    for c in range(len(chunks)):
        if c + DIL_AHEAD < len(chunks):
            scores(c + DIL_AHEAD)
        softmax_pv(c)


def _dil_attn(proj, *, reach):
    batch, _, seq, _ = proj.shape
    n_hp = WIDTH // LANES

    def spec(first_tile, d, kind):
        return pl.BlockSpec((None, None, d, seq // d, LANES),
                            lambda b, hp: (b, first_tile + kind, 0, 0, hp))

    views = {d: proj.reshape(batch, N_STEPS, d, seq // d, WIDTH) for d in SLOT_DIL}
    f32_rows = pltpu.VMEM((seq, LANES), F32)
    return pl.pallas_call(
        functools.partial(_dil_kernel, seq=seq, reach=reach),
        out_shape=jax.ShapeDtypeStruct((batch, 16, seq // 16, WIDTH), BF16),
        grid=(batch, n_hp),
        in_specs=[spec(t0, d, kind) for t0, d in zip((T_G0, T_G1, T_G2), SLOT_DIL)
                  for kind in range(3)],
        out_specs=pl.BlockSpec((None, 16, seq // 16, LANES), lambda b, hp: (b, 0, 0, hp)),
        scratch_shapes=[f32_rows] * 4 + [
            pltpu.VMEM((3, QBLK, 2 * QBLK), F32),
            pltpu.VMEM((N_SBUF, DIL_CHUNK, 2 * QBLK, 2 * QBLK), F32),
            pltpu.VMEM((2, DIL_CHUNK, 2 * QBLK, 2 * QBLK), BF16)],
        compiler_params=pltpu.CompilerParams(
            dimension_semantics=("arbitrary", "arbitrary"),
            vmem_limit_bytes=VMEM_LIMIT),
        name="dil_attn",
    )(*[views[d] for d in SLOT_DIL for _ in range(3)])


def _na_kernel(q_ref, k_ref, v_ref, rpb_ref, qm_ref, km_ref, vm_ref,
               o_ref, om_ref, bias_scr, s_scr, p_scr, sm_scr, pm_scr, rdm_scr, *, rows):
    kr = min(NA_ROWS, rows)
    nk = kr * GRID_W
    n_var = 2 * NA_ROWS - kr
    left = lax.broadcasted_iota(jnp.int32, (1, LANES), 1) < HEAD_DIM
    keep_a = jnp.where(left, 1.0, 0.0).astype(BF16)
    keep_b = jnp.where(left, 0.0, 1.0).astype(BF16)

    @pl.when(pl.program_id(1) == 0)
    def _():
        qc = lax.broadcasted_iota(jnp.int32, (GRID_W, LANES), 0)
        kc = lax.broadcasted_iota(jnp.int32, (GRID_W, LANES), 1) % GRID_W
        rel = jnp.clip(kc - qc, -(NA_COLS - 1), NA_COLS - 1) + NA_COLS - 1
        c_start = jnp.clip(qc - NA_COLS // 2, 0, GRID_W - NA_COLS)
        valid = jnp.logical_and(kc >= c_start, kc < c_start + NA_COLS)
        for hh in range(LANES // HEAD_DIM):
            head = pl.program_id(0) * (LANES // HEAD_DIM) + hh
            for a in range(2 * NA_ROWS - 1):
                row = rpb_ref[pl.ds(head * (2 * NA_ROWS - 1) + a, 1), :]
                blk = jnp.take_along_axis(jnp.broadcast_to(row, (GRID_W, LANES)), rel, axis=1)
                blk = jnp.where(valid, blk * LOG2E, NEG)
                for j in range(kr):
                    if 0 <= a - j < n_var:
                        half = slice((j % 2) * GRID_W, (j % 2 + 1) * GRID_W)
                        bias_scr[hh, a - j, :, j * GRID_W:(j + 1) * GRID_W] = blk[:, half]

    def window(r):
        r_start = min(max(r - kr // 2, 0), rows - kr)
        return r * GRID_W, r_start * GRID_W, r_start - r + (NA_ROWS - 1)

    def scores(c):
        for t in range(NA_CHUNK):
            row0, ks, _ = window(c * NA_CHUNK + t)
            q2d = q_ref[row0:row0 + GRID_W, :]
            qq = jnp.concatenate([q2d * keep_a, q2d * keep_b], axis=0)
            s_scr[c % 2, t] = lax.dot_general(qq, k_ref[ks:ks + nk, :], (((1,), (1,)), ((), ())),
                                              preferred_element_type=F32)

    def softmax_pv(c):
        for t in range(NA_CHUNK):
            row0, ks, variant = window(c * NA_CHUNK + t)
            dens = []
            for hh in range(LANES // HEAD_DIM):
                rows_h = slice(hh * GRID_W, (hh + 1) * GRID_W)
                s = s_scr[c % 2, t, rows_h, :] + bias_scr[hh, variant]
                m = jnp.max(s, axis=-1, keepdims=True)
                p = jnp.exp2(s - m)
                dens.append(jnp.sum(p, axis=-1, keepdims=True))
                p_scr[t, rows_h, :] = p.astype(BF16)
            pv = jnp.dot(p_scr[t], v_ref[ks:ks + nk, :], preferred_element_type=F32)
            den_t = jnp.where(left, jnp.broadcast_to(dens[0], (GRID_W, LANES)),
                              jnp.broadcast_to(dens[1], (GRID_W, LANES)))
            o_ref[row0:row0 + GRID_W, :] = (
                jnp.where(left, pv[:GRID_W], pv[GRID_W:]) / den_t).astype(BF16)

    sm_scr[...] = lax.dot_general(qm_ref[...], km_ref[...], (((1,), (1,)), ((), ())),
                                  preferred_element_type=F32)

    def mem_softmax(i):
        rows_i = slice(i * QBLK, (i + 1) * QBLK)
        s = sm_scr[rows_i, :]
        m = jnp.max(s, axis=-1, keepdims=True)
        p = jnp.exp2(s - m)
        den = jnp.sum(p, axis=-1, keepdims=True)
        pm_scr[rows_i, :] = p.astype(BF16)
        rdm_scr[rows_i, :] = jnp.broadcast_to(1.0 / den, (QBLK, LANES))

    n_chunks = rows // NA_CHUNK
    mem_tiles = qm_ref.shape[0] // QBLK
    scores(0)
    for c in range(n_chunks):
        if c + 1 < n_chunks:
            scores(c + 1)
        softmax_pv(c)
        for i in range(c * mem_tiles // n_chunks, (c + 1) * mem_tiles // n_chunks):
            mem_softmax(i)
    pv = jnp.dot(pm_scr[...], vm_ref[...], preferred_element_type=F32)
    om_ref[...] = (pv * rdm_scr[...]).astype(BF16)


def _na_mem_attn(proj, rpb, kv_m):
    batch, _, seq, _ = proj.shape
    rows = seq // GRID_W
    kr = min(NA_ROWS, rows)
    nk = kr * GRID_W
    n_hp = WIDTH // LANES
    assert rpb.shape == (WIDTH // HEAD_DIM, 2 * NA_ROWS - 1, 2 * NA_COLS - 1)

    def in_spec(tile):
        return pl.BlockSpec((None, None, seq, LANES), lambda hp, b: (b, tile, 0, hp))

    rpb_rows = jnp.pad(rpb.reshape(-1, rpb.shape[-1]).astype(F32),
                       ((0, 0), (0, LANES - rpb.shape[-1])))
    mem_len = kv_m.shape[1]
    assert MEM_HEADS == n_hp and MEM_HEAD_DIM == LANES
    out_spec = pl.BlockSpec((None, seq, LANES), lambda hp, b: (b, 0, hp))
    return pl.pallas_call(
        functools.partial(_na_kernel, rows=rows),
        out_shape=(jax.ShapeDtypeStruct((batch, seq, WIDTH), BF16),
                   jax.ShapeDtypeStruct((batch, seq, WIDTH), BF16)),
        grid=(n_hp, batch),
        in_specs=[in_spec(T_NA), in_spec(T_NA + 1), in_spec(T_NA + 2),
                  pl.BlockSpec(rpb_rows.shape, lambda hp, b: (0, 0)),
                  in_spec(T_MEMQ),
                  pl.BlockSpec((None, mem_len, LANES), lambda hp, b: (b, 0, hp)),
                  pl.BlockSpec((None, mem_len, LANES), lambda hp, b: (b, 0, MEM_HEADS + hp))],
        out_specs=(out_spec, out_spec),
        scratch_shapes=[pltpu.VMEM((LANES // HEAD_DIM, 2 * NA_ROWS - kr, GRID_W, nk), F32),
                        pltpu.VMEM((2, NA_CHUNK, 2 * GRID_W, nk), F32),
                        pltpu.VMEM((NA_CHUNK, 2 * GRID_W, nk), BF16),
                        pltpu.VMEM((seq, mem_len), F32), pltpu.VMEM((seq, mem_len), BF16),
                        pltpu.VMEM((seq, LANES), F32)],
        compiler_params=pltpu.CompilerParams(
            dimension_semantics=("arbitrary", "arbitrary"), vmem_limit_bytes=VMEM_LIMIT),
        name="na_mem_attn",
    )(proj, proj, proj, rpb_rows, proj, kv_m, kv_m)


def _merge_kernel(oa, ga, ob, oc, gb, gc, za0, za1, zb0, zb1, zc0, zc1,
                  wa_f32, wb_f32, wc_f32, wo_f32, pn_ref, x_ref, out_ref,
                  u_scr, wa_ref, wb_ref, wc_ref, wo_ref, *, tm):
    @pl.when(pl.program_id(0) == 0)
    def _():
        for src, dst in ((wa_f32, wa_ref), (wb_f32, wb_ref), (wc_f32, wc_ref), (wo_f32, wo_ref)):
            dst[...] = src[...].astype(BF16)

    def silu_gated(o, silu_g):
        return o.astype(F32) * silu_g.astype(F32)

    seg = MERGE_ROWS // 16
    for blk in range(tm // MERGE_ROWS):
        rows = slice(blk * MERGE_ROWS, (blk + 1) * MERGE_ROWS)

        def project(u, w_ref, gate_lo, gate_hi):
            yb = jnp.dot(u.astype(BF16), w_ref[...], preferred_element_type=F32)
            gate = jnp.concatenate([gate_lo[rows, :], gate_hi[rows, :]], axis=-1)
            return gate.astype(F32) * yb

        p16_rows = slice(blk * seg, (blk + 1) * seg)
        ua = silu_gated(oa[:, p16_rows, :].reshape(MERGE_ROWS, WIDTH),
                        ga[:, p16_rows, :].reshape(MERGE_ROWS, WIDTH))
        for r in range(16):
            for c in range(WIDTH // LANES):
                u_scr[blk, c, pl.ds(r, seg, stride=16), :] = ua[r * seg:(r + 1) * seg,
                                                                c * LANES:(c + 1) * LANES]
        ua = jnp.concatenate([u_scr[blk, c] for c in range(WIDTH // LANES)], axis=-1)

        y = (project(ua, wa_ref, za0, za1)
             + project(silu_gated(ob[rows, :], gb[rows, :]), wb_ref, zb0, zb1)
             + project(silu_gated(oc[rows, :], gc[rows, :]), wc_ref, zc0, zc1))
        z = jnp.dot(y.astype(BF16), wo_ref[...], preferred_element_type=F32)
        ms = jnp.mean(z * z, axis=-1, keepdims=True)
        out_ref[rows, :] = x_ref[rows, :] + z * lax.rsqrt(ms + EPS) * pn_ref[...]


def _merge(x2d, proj, out_a16, out_b, out_c, wa, wb, wc, wo, post_norm, *, tm):
    rows, dm = x2d.shape
    batch, _, seq, _ = proj.shape
    per_b = seq // tm
    seg = tm // 16
    proj16 = proj.reshape(batch, N_STEPS, 16, seq // 16, WIDTH)

    def p16_spec(tile):
        return pl.BlockSpec((None, None, 16, seg, WIDTH),
                            lambda i: (i // per_b, tile, 0, i % per_b, 0))

    def nat_spec(tile):
        return pl.BlockSpec((None, None, tm, WIDTH), lambda i: (i // per_b, tile, i % per_b, 0))

    def rows_spec(width):
        return pl.BlockSpec((tm, width), lambda i: (i, 0))

    def full(shape):
        return pl.BlockSpec(shape, lambda i: (0, 0), pipeline_mode=pl.Buffered(1))

    return pl.pallas_call(
        functools.partial(_merge_kernel, tm=tm),
        out_shape=jax.ShapeDtypeStruct((rows, dm), F32),
        grid=(rows // tm,),
        in_specs=[pl.BlockSpec((None, 16, seg, WIDTH), lambda i: (i // per_b, 0, i % per_b, 0)),
                  p16_spec(T_GATE_A), rows_spec(WIDTH), rows_spec(WIDTH),
                  nat_spec(T_GATE_B), nat_spec(T_GATE_B + 1),
                  nat_spec(T_MERGE_A), nat_spec(T_MERGE_A + 1),
                  nat_spec(T_MERGE_B), nat_spec(T_MERGE_B + 1),
                  nat_spec(T_MERGE_B + 2), nat_spec(T_MERGE_B + 3),
                  full(wa.shape), full(wb.shape), full(wc.shape),
                  full(wo.shape), full(post_norm.shape), rows_spec(dm)],
        out_specs=pl.BlockSpec((tm, dm), lambda i: (i, 0)),
        scratch_shapes=[pltpu.VMEM((tm // MERGE_ROWS, WIDTH // LANES, MERGE_ROWS, LANES), F32)]
        + [pltpu.VMEM(w.shape, BF16) for w in (wa, wb, wc, wo)],
        compiler_params=pltpu.CompilerParams(
            dimension_semantics=("arbitrary",), vmem_limit_bytes=VMEM_LIMIT),
        name="merge",
    )(out_a16, proj16, out_b, out_c, proj, proj, proj, proj, proj, proj, proj, proj,
      wa, wb, wc, wo, post_norm, x2d)


def kernel(x, mem, pre_norm, w_in, merge_bias, na_rpb, mem_norm, w_mem_kv,
           w_branch_a, w_branch_b, w_branch_c, w_out, post_norm):
    b, s, dm = x.shape
    depth = pre_norm.shape[0]
    reach = (DIL_CONFIGS[0][0] // 2) // DIL_CONFIGS[0][1]
    assert all((w // 2) // d == reach for w, d in DIL_CONFIGS) and 2 * reach == QBLK
    assert tuple(d for _, d in DIL_CONFIGS) == SLOT_DIL
    cos_t, sin_t = _rope_tables(s)
    for layer in range(depth):
        x2d = x.reshape(b * s, dm)
        proj = _proj(x2d, pre_norm[layer][None], w_in[layer], merge_bias[layer], cos_t, sin_t,
                     batch=b, seq=s)

        out_a16 = _dil_attn(proj, reach=reach)
        mem2d = mem.reshape(b * mem.shape[1], dm)
        kv_m = _kv_proj(mem2d, mem_norm[layer][None], w_mem_kv[layer], tm=1024)
        out_b, out_c = _na_mem_attn(proj, na_rpb[layer], kv_m.reshape(b, mem.shape[1], -1))

        y = _merge(x2d, proj, out_a16, out_b.reshape(b * s, WIDTH), out_c.reshape(b * s, WIDTH),
                   w_branch_a[layer], w_branch_b[layer], w_branch_c[layer],
                   w_out[layer], post_norm[layer][None], tm=2 * MERGE_ROWS)
        x = y.reshape(b, s, dm)
    return x
```

```python
import functools
import math

import jax
import jax.numpy as jnp
import numpy as np
from jax import lax
from jax.experimental import pallas as pl
from jax.experimental.pallas import tpu as pltpu

D_MODEL = 1024
HEAD_DIM = 64
DIL_CONFIGS = ((128, 1), (512, 4), (2048, 16))
WIDTH = 512
NA_ROWS = 8
NA_COLS = 16
GRID_W = 64
MEM_HEADS = 4
MEM_HEAD_DIM = 128
ROPE_THETA = 500000.0
ROPE_DIM = HEAD_DIM // 4
ROPE_HALF = ROPE_DIM // 2
EPS = 1e-6
NEG = -1e30
LOG2E = math.log2(math.e)

LANES = 128
QBLK = 128
DIL_CHUNK = 1
DIL_AHEAD = 1
N_SBUF = DIL_AHEAD + 1
NA_CHUNK = 8
PROJ_ROWS = 1024
MERGE_ROWS = 256
VMEM_LIMIT = 56 * 1024 * 1024

_QS = HEAD_DIM ** -0.5 * LOG2E
_MS = MEM_HEAD_DIM ** -0.5 * LOG2E
EPI_ROPE, EPI_SCALE, EPI_SIGMOID, EPI_SILU = range(4)
STEPS = (
    (0, 0, EPI_ROPE, _QS), (1, 0, EPI_ROPE, 1.0), (2, 0, EPI_SCALE, 1.0),
    (9, 0, EPI_SCALE, _QS), (10, 0, EPI_SCALE, 1.0), (11, 0, EPI_SCALE, 1.0),
    (12, 0, EPI_SCALE, _MS),
    (14, 0, EPI_SILU, 1.0), (15, 0, EPI_SILU, 1.0),
    (18, 0, EPI_SIGMOID, 1.0), (19, 0, EPI_SIGMOID, 1.0),
    (20, 0, EPI_SIGMOID, 1.0), (21, 0, EPI_SIGMOID, 1.0),
    (16, 0, EPI_SIGMOID, 1.0), (17, 0, EPI_SIGMOID, 1.0),
    (3, 1, EPI_ROPE, _QS), (4, 1, EPI_ROPE, 1.0), (5, 1, EPI_SCALE, 1.0),
    (6, 2, EPI_ROPE, _QS), (7, 2, EPI_ROPE, 1.0), (8, 2, EPI_SCALE, 1.0),
    (13, 2, EPI_SILU, 1.0),
)
FIRST_MERGE_TILE = 16
N_STEPS = len(STEPS)
T_G0, T_NA, T_MEMQ, T_GATE_B, T_MERGE_B, T_MERGE_A = 0, 3, 6, 7, 9, 13
T_G1, T_G2, T_GATE_A = 15, 18, 21
SLOT_DIL = (1, 4, 16)

BF16 = jnp.bfloat16
F32 = jnp.float32


def _sigmoid(z):
    return 0.5 * jnp.tanh(0.5 * z) + 0.5


def _proj_kernel(wtile_ref, kind_ref, scale_ref, x_ref, g_ref, w_ref, mb_ref,
                 cos_ref, sin_ref, o_ref, h_ref, wb_ref, *, tm):
    j = pl.program_id(1)
    n_lane_tiles = x_ref.shape[1] // LANES

    @pl.when(j == 0)
    def _():
        rb = 256
        ssq = [jnp.sum(jnp.square(x_ref[r0:r0 + rb, :]), axis=-1, keepdims=True)
               for r0 in range(0, tm, rb)]
        rs = lax.rsqrt(jnp.concatenate(ssq, axis=0) * (1.0 / x_ref.shape[1]) + EPS)
        for c in range(n_lane_tiles):
            cs = slice(c * LANES, (c + 1) * LANES)
            h_ref[:, cs] = (x_ref[:, cs] * rs * g_ref[:, cs]).astype(BF16)

    q_scale = scale_ref[j]

    def step(epi, d):
        wb_ref[...] = w_ref[...].astype(BF16)
        lane = lax.broadcasted_iota(jnp.int32, (1, LANES), 1)
        first_half = lane % HEAD_DIM < ROPE_HALF
        if epi == EPI_SIGMOID:
            bias = mb_ref[pl.ds(jnp.maximum(wtile_ref[j] - FIRST_MERGE_TILE, 0), 1), :]
        block = PROJ_ROWS // 2 if (epi == EPI_ROPE and d <= 4) else PROJ_ROWS
        per_class = block // d
        for r0 in range(0, tm, block):
            rows = pl.ds(r0, block)
            acc = jnp.dot(h_ref[rows, :], wb_ref[...], preferred_element_type=F32)
            if epi == EPI_ROPE:
                c, s = cos_ref[rows, :], sin_ref[rows, :]
            for t in range(WIDTH // LANES):
                lanes = slice(t * LANES, (t + 1) * LANES)
                a = acc[:, lanes]
                if epi == EPI_ROPE:
                    partner = jnp.where(first_half, pltpu.roll(a, LANES - ROPE_HALF, 1),
                                        pltpu.roll(a, ROPE_HALF, 1))
                    a = (a * c + partner * s) * q_scale
                elif epi == EPI_SCALE:
                    a = a * q_scale
                elif epi == EPI_SIGMOID:
                    a = _sigmoid(a + bias[:, lanes])
                else:
                    a = a * _sigmoid(a)
                if d == 1:
                    o_ref[rows, lanes] = a.astype(BF16)
                else:
                    by_class = jnp.swapaxes(a.reshape(per_class, d, LANES), 0, 1)
                    for r in range(d):
                        o_ref[pl.ds(r * (tm // d) + r0 // d, per_class), lanes] = (
                            by_class[r].astype(BF16))

    for kind in sorted({st[2] * len(SLOT_DIL) + st[1] for st in STEPS}):
        epi, slot = divmod(kind, len(SLOT_DIL))
        pl.when(kind_ref[j] == kind)(functools.partial(step, epi, SLOT_DIL[slot]))


def _proj(x2d, gain, w, merge_bias, cos_t, sin_t, *, batch, seq):
    rows, dm = x2d.shape
    tm = seq
    prefetch = (jnp.asarray([st[0] for st in STEPS], jnp.int32),
                jnp.asarray([st[2] * len(SLOT_DIL) + st[1] for st in STEPS], jnp.int32),
                jnp.asarray([st[3] for st in STEPS], F32))
    mb_rows = merge_bias.reshape(-1, WIDTH)
    tab_spec = pl.BlockSpec((tm, LANES), lambda i, j, *_: (0, 0))
    return pl.pallas_call(
        functools.partial(_proj_kernel, tm=tm),
        out_shape=jax.ShapeDtypeStruct((batch, N_STEPS, seq, WIDTH), BF16),
        grid_spec=pltpu.PrefetchScalarGridSpec(
            num_scalar_prefetch=len(prefetch),
            grid=(rows // tm, N_STEPS),
            in_specs=[
                pl.BlockSpec((tm, dm), lambda i, j, *_: (i, 0)),
                pl.BlockSpec((1, dm), lambda i, j, *_: (0, 0)),
                pl.BlockSpec((dm, WIDTH), lambda i, j, wt, kd, sc: (0, wt[j])),
                pl.BlockSpec(mb_rows.shape, lambda i, j, *_: (0, 0)),
                tab_spec, tab_spec,
            ],
            out_specs=pl.BlockSpec((None, None, tm, WIDTH), lambda i, j, *_: (i, j, 0, 0)),
            scratch_shapes=[pltpu.VMEM((tm, dm), BF16), pltpu.VMEM((dm, WIDTH), BF16)]),
        compiler_params=pltpu.CompilerParams(
            dimension_semantics=("arbitrary", "arbitrary"),
            vmem_limit_bytes=VMEM_LIMIT),
        name="proj",
    )(*prefetch, x2d, gain, w, mb_rows, cos_t, sin_t)


def _rope_tables(seq):
    pos = np.arange(seq, dtype=np.float64)
    inv = ROPE_THETA ** (-np.arange(ROPE_HALF, dtype=np.float64) * 2.0 / ROPE_DIM)
    ang = pos[:, None] * inv[None, :]
    cos, sin = np.cos(ang), np.sin(ang)
    rest = HEAD_DIM - ROPE_DIM
    c = np.concatenate([cos, cos, np.ones((seq, rest))] * (LANES // HEAD_DIM), axis=-1)
    s = np.concatenate([-sin, sin, np.zeros((seq, rest))] * (LANES // HEAD_DIM), axis=-1)

    return jnp.asarray(c, F32), jnp.asarray(s, F32)


def _kv_kernel(x_ref, g_ref, w_ref, o_ref):
    xf = x_ref[...]
    ms = jnp.mean(xf * xf, axis=-1, keepdims=True)
    h = (xf * lax.rsqrt(ms + EPS) * g_ref[...]).astype(BF16)
    o_ref[...] = jnp.dot(h, w_ref[...].astype(BF16), preferred_element_type=F32).astype(BF16)


def _kv_proj(mem2d, gain, w, *, tm):
    rows, dm = mem2d.shape
    n_out = w.shape[1]
    return pl.pallas_call(
        _kv_kernel,
        out_shape=jax.ShapeDtypeStruct((rows, n_out), BF16),
        grid=(rows // tm,),
        in_specs=[pl.BlockSpec((tm, dm), lambda i: (i, 0)),
                  pl.BlockSpec((1, dm), lambda i: (0, 0)),
                  pl.BlockSpec((dm, n_out), lambda i: (0, 0))],
        out_specs=pl.BlockSpec((tm, n_out), lambda i: (i, 0)),
        compiler_params=pltpu.CompilerParams(
            dimension_semantics=("arbitrary",), vmem_limit_bytes=VMEM_LIMIT),
        name="kv_proj",
    )(mem2d, gain, w)


def _dil_kernel(q0, k0, v0, q1, k1, v1, q2, k2, v2, o_ref,
                og0, lg0, og1, lg1, bias_scr, s_scr, p_scr, *, seq, reach):
    left = lax.broadcasted_iota(jnp.int32, (1, LANES), 1) < HEAD_DIM
    keep_a = jnp.where(left, 1.0, 0.0).astype(BF16)
    keep_b = jnp.where(left, 0.0, 1.0).astype(BF16)
    kw_max = 2 * QBLK

    @pl.when(jnp.logical_and(pl.program_id(0) == 0, pl.program_id(1) == 0))
    def _():
        rel = (lax.broadcasted_iota(jnp.int32, (QBLK, kw_max), 0)
               - lax.broadcasted_iota(jnp.int32, (QBLK, kw_max), 1))
        for n in range(3):
            bias_scr[n] = jnp.where(jnp.abs(rel + n * reach) <= reach, 0.0, NEG)

    def pair_tile(col_a, col_b):
        return jnp.where(left, jnp.broadcast_to(col_a, (QBLK, LANES)),
                         jnp.broadcast_to(col_b, (QBLK, LANES)))

    def keep(o_scr, l_scr, length):
        def finish(r, row0, acc, m, den):
            rows = pl.ds(r * length + row0, QBLK)
            o_scr[rows, :] = acc / den
            l_scr[rows, :] = m + jnp.log2(den)
        return finish

    def combine(r, row0, acc2, m2, den2):
        rows0 = pl.ds(r, QBLK, stride=16)
        rows1 = pl.ds((r % 4) * (seq // 4) + r // 4, QBLK, stride=4)
        l0, l1 = lg0[rows0, :], lg1[rows1, :]
        m = jnp.maximum(jnp.maximum(l0, l1), m2)
        w0, w1, w2 = jnp.exp2(l0 - m), jnp.exp2(l1 - m), jnp.exp2(m2 - m)
        num = w0 * og0[rows0, :] + w1 * og1[rows1, :] + w2 * acc2
        o_ref[r] = (num / (w0 + w1 + w2 * den2)).astype(BF16)

    tiles = []
    for (q_ref, k_ref, v_ref), d, finish in (
            ((q0, k0, v0), 1, keep(og0, lg0, seq)),
            ((q1, k1, v1), 4, keep(og1, lg1, seq // 4)),
            ((q2, k2, v2), 16, combine)):
        length = seq // d
        kw = min(kw_max, length)
        for bi in range(seq // QBLK):
            r, row0 = divmod(bi * QBLK, length)
            ks = min(max(row0 - reach, 0), length - kw)
            tiles.append((q_ref, k_ref, v_ref, r, row0, ks, kw, finish))
    chunks = [tiles[i:i + DIL_CHUNK] for i in range(0, len(tiles), DIL_CHUNK)]

    def scores(c):
        for t, (q_ref, k_ref, _, r, row0, ks, kw, _) in enumerate(chunks[c]):
            q2d = q_ref[r, row0:row0 + QBLK, :]
            qq = jnp.concatenate([q2d * keep_a, q2d * keep_b], axis=0)
            s_scr[c % N_SBUF, t, :, :kw] = lax.dot_general(
                qq, k_ref[r, ks:ks + kw, :], (((1,), (1,)), ((), ())),
                preferred_element_type=F32)

    def softmax_pv(c):
        for t, (_, _, v_ref, r, row0, ks, kw, finish) in enumerate(chunks[c]):
            bias = bias_scr[(row0 - ks) // reach, :, :kw]
            s = s_scr[c % N_SBUF, t, :, :kw] + jnp.concatenate([bias, bias], axis=0)
            m = jnp.max(s, axis=-1, keepdims=True)
            p = jnp.exp2(s - m)
            den = jnp.sum(p, axis=-1, keepdims=True)
            p_scr[c % 2, t, :, :kw] = p.astype(BF16)
            pv = jnp.dot(p_scr[c % 2, t, :, :kw], v_ref[r, ks:ks + kw, :],
                         preferred_element_type=F32)
            finish(r, row0, jnp.where(left, pv[:QBLK], pv[QBLK:]),
                   pair_tile(m[:QBLK], m[QBLK:]), pair_tile(den[:QBLK], den[QBLK:]))

    for c in range(min(DIL_AHEAD, len(chunks))):
        scores(c)
    for c in range(len(chunks)):
        if c + DIL_AHEAD < len(chunks):
            scores(c + DIL_AHEAD)
        softmax_pv(c)


def _dil_attn(proj, *, reach):
    batch, _, seq, _ = proj.shape
    n_hp = WIDTH // LANES

    def spec(first_tile, d, kind):
        return pl.BlockSpec((None, None, d, seq // d, LANES),
                            lambda b, hp: (b, first_tile + kind, 0, 0, hp))

    views = {d: proj.reshape(batch, N_STEPS, d, seq // d, WIDTH) for d in SLOT_DIL}
    f32_rows = pltpu.VMEM((seq, LANES), F32)
    return pl.pallas_call(
        functools.partial(_dil_kernel, seq=seq, reach=reach),
        out_shape=jax.ShapeDtypeStruct((batch, 16, seq // 16, WIDTH), BF16),
        grid=(batch, n_hp),
        in_specs=[spec(t0, d, kind) for t0, d in zip((T_G0, T_G1, T_G2), SLOT_DIL)
                  for kind in range(3)],
        out_specs=pl.BlockSpec((None, 16, seq // 16, LANES), lambda b, hp: (b, 0, 0, hp)),
        scratch_shapes=[f32_rows] * 4 + [
            pltpu.VMEM((3, QBLK, 2 * QBLK), F32),
            pltpu.VMEM((N_SBUF, DIL_CHUNK, 2 * QBLK, 2 * QBLK), F32),
            pltpu.VMEM((2, DIL_CHUNK, 2 * QBLK, 2 * QBLK), BF16)],
        compiler_params=pltpu.CompilerParams(
            dimension_semantics=("arbitrary", "arbitrary"),
            vmem_limit_bytes=VMEM_LIMIT),
        name="dil_attn",
    )(*[views[d] for d in SLOT_DIL for _ in range(3)])


def _na_kernel(q_ref, k_ref, v_ref, rpb_ref, qm_ref, km_ref, vm_ref,
               o_ref, om_ref, bias_scr, s_scr, p_scr, sm_scr, pm_scr, rdm_scr, *, rows):
    kr = min(NA_ROWS, rows)
    nk = kr * GRID_W
    n_var = 2 * NA_ROWS - kr
    left = lax.broadcasted_iota(jnp.int32, (1, LANES), 1) < HEAD_DIM
    keep_a = jnp.where(left, 1.0, 0.0).astype(BF16)
    keep_b = jnp.where(left, 0.0, 1.0).astype(BF16)

    @pl.when(pl.program_id(1) == 0)
    def _():
        qc = lax.broadcasted_iota(jnp.int32, (GRID_W, LANES), 0)
        kc = lax.broadcasted_iota(jnp.int32, (GRID_W, LANES), 1) % GRID_W
        rel = jnp.clip(kc - qc, -(NA_COLS - 1), NA_COLS - 1) + NA_COLS - 1
        c_start = jnp.clip(qc - NA_COLS // 2, 0, GRID_W - NA_COLS)
        valid = jnp.logical_and(kc >= c_start, kc < c_start + NA_COLS)
        for hh in range(LANES // HEAD_DIM):
            head = pl.program_id(0) * (LANES // HEAD_DIM) + hh
            for a in range(2 * NA_ROWS - 1):
                row = rpb_ref[pl.ds(head * (2 * NA_ROWS - 1) + a, 1), :]
                blk = jnp.take_along_axis(jnp.broadcast_to(row, (GRID_W, LANES)), rel, axis=1)
                blk = jnp.where(valid, blk * LOG2E, NEG)
                for j in range(kr):
                    if 0 <= a - j < n_var:
                        half = slice((j % 2) * GRID_W, (j % 2 + 1) * GRID_W)
                        bias_scr[hh, a - j, :, j * GRID_W:(j + 1) * GRID_W] = blk[:, half]

    def window(r):
        r_start = min(max(r - kr // 2, 0), rows - kr)
        return r * GRID_W, r_start * GRID_W, r_start - r + (NA_ROWS - 1)

    def scores(c):
        for t in range(NA_CHUNK):
            row0, ks, _ = window(c * NA_CHUNK + t)
            q2d = q_ref[row0:row0 + GRID_W, :]
            qq = jnp.concatenate([q2d * keep_a, q2d * keep_b], axis=0)
            s_scr[c % 2, t] = lax.dot_general(qq, k_ref[ks:ks + nk, :], (((1,), (1,)), ((), ())),
                                              preferred_element_type=F32)

    def softmax_pv(c):
        for t in range(NA_CHUNK):
            row0, ks, variant = window(c * NA_CHUNK + t)
            dens = []
            for hh in range(LANES // HEAD_DIM):
                rows_h = slice(hh * GRID_W, (hh + 1) * GRID_W)
                s = s_scr[c % 2, t, rows_h, :] + bias_scr[hh, variant]
                m = jnp.max(s, axis=-1, keepdims=True)
                p = jnp.exp2(s - m)
                dens.append(jnp.sum(p, axis=-1, keepdims=True))
                p_scr[t, rows_h, :] = p.astype(BF16)
            pv = jnp.dot(p_scr[t], v_ref[ks:ks + nk, :], preferred_element_type=F32)
            den_t = jnp.where(left, jnp.broadcast_to(dens[0], (GRID_W, LANES)),
                              jnp.broadcast_to(dens[1], (GRID_W, LANES)))
            o_ref[row0:row0 + GRID_W, :] = (
                jnp.where(left, pv[:GRID_W], pv[GRID_W:]) / den_t).astype(BF16)

    sm_scr[...] = lax.dot_general(qm_ref[...], km_ref[...], (((1,), (1,)), ((), ())),
                                  preferred_element_type=F32)

    def mem_softmax(i):
        rows_i = slice(i * QBLK, (i + 1) * QBLK)
        s = sm_scr[rows_i, :]
        m = jnp.max(s, axis=-1, keepdims=True)
        p = jnp.exp2(s - m)
        den = jnp.sum(p, axis=-1, keepdims=True)
        pm_scr[rows_i, :] = p.astype(BF16)
        rdm_scr[rows_i, :] = jnp.broadcast_to(1.0 / den, (QBLK, LANES))

    n_chunks = rows // NA_CHUNK
    mem_tiles = qm_ref.shape[0] // QBLK
    scores(0)
    for c in range(n_chunks):
        if c + 1 < n_chunks:
            scores(c + 1)
        softmax_pv(c)
        for i in range(c * mem_tiles // n_chunks, (c + 1) * mem_tiles // n_chunks):
            mem_softmax(i)
    pv = jnp.dot(pm_scr[...], vm_ref[...], preferred_element_type=F32)
    om_ref[...] = (pv * rdm_scr[...]).astype(BF16)


def _na_mem_attn(proj, rpb, kv_m):
    batch, _, seq, _ = proj.shape
    rows = seq // GRID_W
    kr = min(NA_ROWS, rows)
    nk = kr * GRID_W
    n_hp = WIDTH // LANES
    assert rpb.shape == (WIDTH // HEAD_DIM, 2 * NA_ROWS - 1, 2 * NA_COLS - 1)

    def in_spec(tile):
        return pl.BlockSpec((None, None, seq, LANES), lambda hp, b: (b, tile, 0, hp))

    rpb_rows = jnp.pad(rpb.reshape(-1, rpb.shape[-1]).astype(F32),
                       ((0, 0), (0, LANES - rpb.shape[-1])))
    mem_len = kv_m.shape[1]
    assert MEM_HEADS == n_hp and MEM_HEAD_DIM == LANES
    out_spec = pl.BlockSpec((None, seq, LANES), lambda hp, b: (b, 0, hp))
    return pl.pallas_call(
        functools.partial(_na_kernel, rows=rows),
        out_shape=(jax.ShapeDtypeStruct((batch, seq, WIDTH), BF16),
                   jax.ShapeDtypeStruct((batch, seq, WIDTH), BF16)),
        grid=(n_hp, batch),
        in_specs=[in_spec(T_NA), in_spec(T_NA + 1), in_spec(T_NA + 2),
                  pl.BlockSpec(rpb_rows.shape, lambda hp, b: (0, 0)),
                  in_spec(T_MEMQ),
                  pl.BlockSpec((None, mem_len, LANES), lambda hp, b: (b, 0, hp)),
                  pl.BlockSpec((None, mem_len, LANES), lambda hp, b: (b, 0, MEM_HEADS + hp))],
        out_specs=(out_spec, out_spec),
        scratch_shapes=[pltpu.VMEM((LANES // HEAD_DIM, 2 * NA_ROWS - kr, GRID_W, nk), F32),
                        pltpu.VMEM((2, NA_CHUNK, 2 * GRID_W, nk), F32),
                        pltpu.VMEM((NA_CHUNK, 2 * GRID_W, nk), BF16),
                        pltpu.VMEM((seq, mem_len), F32), pltpu.VMEM((seq, mem_len), BF16),
                        pltpu.VMEM((seq, LANES), F32)],
        compiler_params=pltpu.CompilerParams(
            dimension_semantics=("arbitrary", "arbitrary"), vmem_limit_bytes=VMEM_LIMIT),
        name="na_mem_attn",
    )(proj, proj, proj, rpb_rows, proj, kv_m, kv_m)


def _merge_kernel(oa, ga, ob, oc, gb, gc, za0, za1, zb0, zb1, zc0, zc1,
                  wa_f32, wb_f32, wc_f32, wo_f32, pn_ref, x_ref, out_ref,
                  u_scr, wa_ref, wb_ref, wc_ref, wo_ref, *, tm):
    @pl.when(pl.program_id(0) == 0)
    def _():
        for src, dst in ((wa_f32, wa_ref), (wb_f32, wb_ref), (wc_f32, wc_ref), (wo_f32, wo_ref)):
            dst[...] = src[...].astype(BF16)

    def silu_gated(o, silu_g):
        return o.astype(F32) * silu_g.astype(F32)

    seg = MERGE_ROWS // 16
    for blk in range(tm // MERGE_ROWS):
        rows = slice(blk * MERGE_ROWS, (blk + 1) * MERGE_ROWS)

        def project(u, w_ref, gate_lo, gate_hi):
            yb = jnp.dot(u.astype(BF16), w_ref[...], preferred_element_type=F32)
            gate = jnp.concatenate([gate_lo[rows, :], gate_hi[rows, :]], axis=-1)
            return gate.astype(F32) * yb

        p16_rows = slice(blk * seg, (blk + 1) * seg)
        ua = silu_gated(oa[:, p16_rows, :].reshape(MERGE_ROWS, WIDTH),
                        ga[:, p16_rows, :].reshape(MERGE_ROWS, WIDTH))
        for r in range(16):
            for c in range(WIDTH // LANES):
                u_scr[blk, c, pl.ds(r, seg, stride=16), :] = ua[r * seg:(r + 1) * seg,
                                                                c * LANES:(c + 1) * LANES]
        ua = jnp.concatenate([u_scr[blk, c] for c in range(WIDTH // LANES)], axis=-1)

        y = (project(ua, wa_ref, za0, za1)
             + project(silu_gated(ob[rows, :], gb[rows, :]), wb_ref, zb0, zb1)
             + project(silu_gated(oc[rows, :], gc[rows, :]), wc_ref, zc0, zc1))
        z = jnp.dot(y.astype(BF16), wo_ref[...], preferred_element_type=F32)
        ms = jnp.mean(z * z, axis=-1, keepdims=True)
        out_ref[rows, :] = x_ref[rows, :] + z * lax.rsqrt(ms + EPS) * pn_ref[...]


def _merge(x2d, proj, out_a16, out_b, out_c, wa, wb, wc, wo, post_norm, *, tm):
    rows, dm = x2d.shape
    batch, _, seq, _ = proj.shape
    per_b = seq // tm
    seg = tm // 16
    proj16 = proj.reshape(batch, N_STEPS, 16, seq // 16, WIDTH)

    def p16_spec(tile):
        return pl.BlockSpec((None, None, 16, seg, WIDTH),
                            lambda i: (i // per_b, tile, 0, i % per_b, 0))

    def nat_spec(tile):
        return pl.BlockSpec((None, None, tm, WIDTH), lambda i: (i // per_b, tile, i % per_b, 0))

    def rows_spec(width):
        return pl.BlockSpec((tm, width), lambda i: (i, 0))

    def full(shape):
        return pl.BlockSpec(shape, lambda i: (0, 0), pipeline_mode=pl.Buffered(1))

    return pl.pallas_call(
        functools.partial(_merge_kernel, tm=tm),
        out_shape=jax.ShapeDtypeStruct((rows, dm), F32),
        grid=(rows // tm,),
        in_specs=[pl.BlockSpec((None, 16, seg, WIDTH), lambda i: (i // per_b, 0, i % per_b, 0)),
                  p16_spec(T_GATE_A), rows_spec(WIDTH), rows_spec(WIDTH),
                  nat_spec(T_GATE_B), nat_spec(T_GATE_B + 1),
                  nat_spec(T_MERGE_A), nat_spec(T_MERGE_A + 1),
                  nat_spec(T_MERGE_B), nat_spec(T_MERGE_B + 1),
                  nat_spec(T_MERGE_B + 2), nat_spec(T_MERGE_B + 3),
                  full(wa.shape), full(wb.shape), full(wc.shape),
                  full(wo.shape), full(post_norm.shape), rows_spec(dm)],
        out_specs=pl.BlockSpec((tm, dm), lambda i: (i, 0)),
        scratch_shapes=[pltpu.VMEM((tm // MERGE_ROWS, WIDTH // LANES, MERGE_ROWS, LANES), F32)]
        + [pltpu.VMEM(w.shape, BF16) for w in (wa, wb, wc, wo)],
        compiler_params=pltpu.CompilerParams(
            dimension_semantics=("arbitrary",), vmem_limit_bytes=VMEM_LIMIT),
        name="merge",
    )(out_a16, proj16, out_b, out_c, proj, proj, proj, proj, proj, proj, proj, proj,
      wa, wb, wc, wo, post_norm, x2d)


def kernel(x, mem, pre_norm, w_in, merge_bias, na_rpb, mem_norm, w_mem_kv,
           w_branch_a, w_branch_b, w_branch_c, w_out, post_norm):
    b, s, dm = x.shape
    depth = pre_norm.shape[0]
    reach = (DIL_CONFIGS[0][0] // 2) // DIL_CONFIGS[0][1]
    assert all((w // 2) // d == reach for w, d in DIL_CONFIGS) and 2 * reach == QBLK
    assert tuple(d for _, d in DIL_CONFIGS) == SLOT_DIL
    cos_t, sin_t = _rope_tables(s)
    for layer in range(depth):
        x2d = x.reshape(b * s, dm)
        proj = _proj(x2d, pre_norm[layer][None], w_in[layer], merge_bias[layer], cos_t, sin_t,
                     batch=b, seq=s)

        out_a16 = _dil_attn(proj, reach=reach)
        mem2d = mem.reshape(b * mem.shape[1], dm)
        kv_m = _kv_proj(mem2d, mem_norm[layer][None], w_mem_kv[layer], tm=1024)
        out_b, out_c = _na_mem_attn(proj, na_rpb[layer], kv_m.reshape(b, mem.shape[1], -1))

        y = _merge(x2d, proj, out_a16, out_b.reshape(b * s, WIDTH), out_c.reshape(b * s, WIDTH),
                   w_branch_a[layer], w_branch_b[layer], w_branch_c[layer],
                   w_out[layer], post_norm[layer][None], tm=2 * MERGE_ROWS)
        x = y.reshape(b, s, dm)
    return x
```

```python
import functools
import math

import jax
import jax.numpy as jnp
import numpy as np
from jax import lax
from jax.experimental import pallas as pl
from jax.experimental.pallas import tpu as pltpu

D_MODEL = 1024
HEAD_DIM = 64
DIL_CONFIGS = ((128, 1), (512, 4), (2048, 16))
WIDTH = 512
NA_ROWS = 8
NA_COLS = 16
GRID_W = 64
MEM_HEADS = 4
MEM_HEAD_DIM = 128
ROPE_THETA = 500000.0
ROPE_DIM = HEAD_DIM // 4
ROPE_HALF = ROPE_DIM // 2
EPS = 1e-6
NEG = -1e30
LOG2E = math.log2(math.e)

LANES = 128
QBLK = 128
DIL_CHUNK = 1
DIL_AHEAD = 1
N_SBUF = DIL_AHEAD + 1
NA_CHUNK = 8
PROJ_ROWS = 2048
MERGE_ROWS = 256
VMEM_LIMIT = 56 * 1024 * 1024

_QS = HEAD_DIM ** -0.5 * LOG2E
_MS = MEM_HEAD_DIM ** -0.5 * LOG2E
EPI_ROPE, EPI_SCALE, EPI_SIGMOID, EPI_SILU = range(4)
STEPS = (
    (0, 0, EPI_ROPE, _QS), (1, 0, EPI_ROPE, 1.0), (2, 0, EPI_SCALE, 1.0),
    (9, 0, EPI_SCALE, _QS), (10, 0, EPI_SCALE, 1.0), (11, 0, EPI_SCALE, 1.0),
    (12, 0, EPI_SCALE, _MS),
    (14, 0, EPI_SILU, 1.0), (15, 0, EPI_SILU, 1.0),
    (18, 0, EPI_SIGMOID, 1.0), (19, 0, EPI_SIGMOID, 1.0),
    (20, 0, EPI_SIGMOID, 1.0), (21, 0, EPI_SIGMOID, 1.0),
    (16, 0, EPI_SIGMOID, 1.0), (17, 0, EPI_SIGMOID, 1.0),
    (3, 1, EPI_ROPE, _QS), (4, 1, EPI_ROPE, 1.0), (5, 1, EPI_SCALE, 1.0),
    (6, 2, EPI_ROPE, _QS), (7, 2, EPI_ROPE, 1.0), (8, 2, EPI_SCALE, 1.0),
    (13, 2, EPI_SILU, 1.0),
)
FIRST_MERGE_TILE = 16
N_STEPS = len(STEPS)
T_G0, T_NA, T_MEMQ, T_GATE_B, T_MERGE_B, T_MERGE_A = 0, 3, 6, 7, 9, 13
T_G1, T_G2, T_GATE_A = 15, 18, 21
SLOT_DIL = (1, 4, 16)

BF16 = jnp.bfloat16
F32 = jnp.float32


def _sigmoid(z):
    return 0.5 * jnp.tanh(0.5 * z) + 0.5


def _proj_kernel(wtile_ref, kind_ref, scale_ref, x_ref, g_ref, w_ref, mb_ref,
                 cos_ref, sin_ref, o_ref, h_ref, wb_ref, *, tm):
    j = pl.program_id(1)
    n_lane_tiles = x_ref.shape[1] // LANES

    @pl.when(j == 0)
    def _():
        rb = 256
        ssq = [jnp.sum(jnp.square(x_ref[r0:r0 + rb, :]), axis=-1, keepdims=True)
               for r0 in range(0, tm, rb)]
        rs = lax.rsqrt(jnp.concatenate(ssq, axis=0) * (1.0 / x_ref.shape[1]) + EPS)
        for c in range(n_lane_tiles):
            cs = slice(c * LANES, (c + 1) * LANES)
            h_ref[:, cs] = (x_ref[:, cs] * rs * g_ref[:, cs]).astype(BF16)

    q_scale = scale_ref[j]

    def step(epi, d):
        wb_ref[...] = w_ref[...].astype(BF16)
        lane = lax.broadcasted_iota(jnp.int32, (1, LANES), 1)
        first_half = lane % HEAD_DIM < ROPE_HALF
        if epi == EPI_SIGMOID:
            bias = mb_ref[pl.ds(jnp.maximum(wtile_ref[j] - FIRST_MERGE_TILE, 0), 1), :]
        block = PROJ_ROWS // 2 if (epi == EPI_ROPE and d <= 4) else PROJ_ROWS
        per_class = block // d
        for r0 in range(0, tm, block):
            rows = pl.ds(r0, block)
            acc = jnp.dot(h_ref[rows, :], wb_ref[...], preferred_element_type=F32)
            if epi == EPI_ROPE:
                c, s = cos_ref[rows, :], sin_ref[rows, :]
            for t in range(WIDTH // LANES):
                lanes = slice(t * LANES, (t + 1) * LANES)
                a = acc[:, lanes]
                if epi == EPI_ROPE:
                    partner = jnp.where(first_half, pltpu.roll(a, LANES - ROPE_HALF, 1),
                                        pltpu.roll(a, ROPE_HALF, 1))
                    a = (a * c + partner * s) * q_scale
                elif epi == EPI_SCALE:
                    a = a * q_scale
                elif epi == EPI_SIGMOID:
                    a = _sigmoid(a + bias[:, lanes])
                else:
                    a = a * _sigmoid(a)
                if d == 1:
                    o_ref[rows, lanes] = a.astype(BF16)
                else:
                    by_class = jnp.swapaxes(a.reshape(per_class, d, LANES), 0, 1)
                    for r in range(d):
                        o_ref[pl.ds(r * (tm // d) + r0 // d, per_class), lanes] = (
                            by_class[r].astype(BF16))

    for kind in sorted({st[2] * len(SLOT_DIL) + st[1] for st in STEPS}):
        epi, slot = divmod(kind, len(SLOT_DIL))
        pl.when(kind_ref[j] == kind)(functools.partial(step, epi, SLOT_DIL[slot]))


def _proj(x2d, gain, w, merge_bias, cos_t, sin_t, *, batch, seq):
    rows, dm = x2d.shape
    tm = seq
    prefetch = (jnp.asarray([st[0] for st in STEPS], jnp.int32),
                jnp.asarray([st[2] * len(SLOT_DIL) + st[1] for st in STEPS], jnp.int32),
                jnp.asarray([st[3] for st in STEPS], F32))
    mb_rows = merge_bias.reshape(-1, WIDTH)
    tab_spec = pl.BlockSpec((tm, LANES), lambda i, j, *_: (0, 0))
    return pl.pallas_call(
        functools.partial(_proj_kernel, tm=tm),
        out_shape=jax.ShapeDtypeStruct((batch, N_STEPS, seq, WIDTH), BF16),
        grid_spec=pltpu.PrefetchScalarGridSpec(
            num_scalar_prefetch=len(prefetch),
            grid=(rows // tm, N_STEPS),
            in_specs=[
                pl.BlockSpec((tm, dm), lambda i, j, *_: (i, 0)),
                pl.BlockSpec((1, dm), lambda i, j, *_: (0, 0)),
                pl.BlockSpec((dm, WIDTH), lambda i, j, wt, kd, sc: (0, wt[j])),
                pl.BlockSpec(mb_rows.shape, lambda i, j, *_: (0, 0)),
                tab_spec, tab_spec,
            ],
            out_specs=pl.BlockSpec((None, None, tm, WIDTH), lambda i, j, *_: (i, j, 0, 0)),
            scratch_shapes=[pltpu.VMEM((tm, dm), BF16), pltpu.VMEM((dm, WIDTH), BF16)]),
        compiler_params=pltpu.CompilerParams(
            dimension_semantics=("arbitrary", "arbitrary"),
            vmem_limit_bytes=VMEM_LIMIT),
        name="proj",
    )(*prefetch, x2d, gain, w, mb_rows, cos_t, sin_t)


def _rope_tables(seq):
    pos = np.arange(seq, dtype=np.float64)
    inv = ROPE_THETA ** (-np.arange(ROPE_HALF, dtype=np.float64) * 2.0 / ROPE_DIM)
    ang = pos[:, None] * inv[None, :]
    cos, sin = np.cos(ang), np.sin(ang)
    rest = HEAD_DIM - ROPE_DIM
    c = np.concatenate([cos, cos, np.ones((seq, rest))] * (LANES // HEAD_DIM), axis=-1)
    s = np.concatenate([-sin, sin, np.zeros((seq, rest))] * (LANES // HEAD_DIM), axis=-1)

    return jnp.asarray(c, F32), jnp.asarray(s, F32)


def _kv_kernel(x_ref, g_ref, w_ref, o_ref):
    xf = x_ref[...]
    ms = jnp.mean(xf * xf, axis=-1, keepdims=True)
    h = (xf * lax.rsqrt(ms + EPS) * g_ref[...]).astype(BF16)
    o_ref[...] = jnp.dot(h, w_ref[...].astype(BF16), preferred_element_type=F32).astype(BF16)


def _kv_proj(mem2d, gain, w, *, tm):
    rows, dm = mem2d.shape
    n_out = w.shape[1]
    return pl.pallas_call(
        _kv_kernel,
        out_shape=jax.ShapeDtypeStruct((rows, n_out), BF16),
        grid=(rows // tm,),
        in_specs=[pl.BlockSpec((tm, dm), lambda i: (i, 0)),
                  pl.BlockSpec((1, dm), lambda i: (0, 0)),
                  pl.BlockSpec((dm, n_out), lambda i: (0, 0))],
        out_specs=pl.BlockSpec((tm, n_out), lambda i: (i, 0)),
        compiler_params=pltpu.CompilerParams(
            dimension_semantics=("arbitrary",), vmem_limit_bytes=VMEM_LIMIT),
        name="kv_proj",
    )(mem2d, gain, w)


def _dil_kernel(q0, k0, v0, q1, k1, v1, q2, k2, v2, o_ref,
                og0, lg0, og1, lg1, bias_scr, s_scr, p_scr, *, seq, reach):
    left = lax.broadcasted_iota(jnp.int32, (1, LANES), 1) < HEAD_DIM
    keep_a = jnp.where(left, 1.0, 0.0).astype(BF16)
    keep_b = jnp.where(left, 0.0, 1.0).astype(BF16)
    kw_max = 2 * QBLK

    @pl.when(jnp.logical_and(pl.program_id(0) == 0, pl.program_id(1) == 0))
    def _():
        rel = (lax.broadcasted_iota(jnp.int32, (QBLK, kw_max), 0)
               - lax.broadcasted_iota(jnp.int32, (QBLK, kw_max), 1))
        for n in range(3):
            bias_scr[n] = jnp.where(jnp.abs(rel + n * reach) <= reach, 0.0, NEG)

    def pair_tile(col_a, col_b):
        return jnp.where(left, jnp.broadcast_to(col_a, (QBLK, LANES)),
                         jnp.broadcast_to(col_b, (QBLK, LANES)))

    def keep(o_scr, l_scr, length):
        def finish(r, row0, acc, m, den):
            rows = pl.ds(r * length + row0, QBLK)
            o_scr[rows, :] = acc / den
            l_scr[rows, :] = m + jnp.log2(den)
        return finish

    def combine(r, row0, acc2, m2, den2):
        rows0 = pl.ds(r, QBLK, stride=16)
        rows1 = pl.ds((r % 4) * (seq // 4) + r // 4, QBLK, stride=4)
        l0, l1 = lg0[rows0, :], lg1[rows1, :]
        m = jnp.maximum(jnp.maximum(l0, l1), m2)
        w0, w1, w2 = jnp.exp2(l0 - m), jnp.exp2(l1 - m), jnp.exp2(m2 - m)
        num = w0 * og0[rows0, :] + w1 * og1[rows1, :] + w2 * acc2
        o_ref[r] = (num / (w0 + w1 + w2 * den2)).astype(BF16)

    tiles = []
    for (q_ref, k_ref, v_ref), d, finish in (
            ((q0, k0, v0), 1, keep(og0, lg0, seq)),
            ((q1, k1, v1), 4, keep(og1, lg1, seq // 4)),
            ((q2, k2, v2), 16, combine)):
        length = seq // d
        kw = min(kw_max, length)
        for bi in range(seq // QBLK):
            r, row0 = divmod(bi * QBLK, length)
            ks = min(max(row0 - reach, 0), length - kw)
            tiles.append((q_ref, k_ref, v_ref, r, row0, ks, kw, finish))
    chunks = [tiles[i:i + DIL_CHUNK] for i in range(0, len(tiles), DIL_CHUNK)]

    def scores(c):
        for t, (q_ref, k_ref, _, r, row0, ks, kw, _) in enumerate(chunks[c]):
            q2d = q_ref[r, row0:row0 + QBLK, :]
            qq = jnp.concatenate([q2d * keep_a, q2d * keep_b], axis=0)
            s_scr[c % N_SBUF, t, :, :kw] = lax.dot_general(
                qq, k_ref[r, ks:ks + kw, :], (((1,), (1,)), ((), ())),
                preferred_element_type=F32)

    def softmax_pv(c):
        for t, (_, _, v_ref, r, row0, ks, kw, finish) in enumerate(chunks[c]):
            bias = bias_scr[(row0 - ks) // reach, :, :kw]
            s = s_scr[c % N_SBUF, t, :, :kw] + jnp.concatenate([bias, bias], axis=0)
            m = jnp.max(s, axis=-1, keepdims=True)
            p = jnp.exp2(s - m)
            den = jnp.sum(p, axis=-1, keepdims=True)
            p_scr[c % 2, t, :, :kw] = p.astype(BF16)
            pv = jnp.dot(p_scr[c % 2, t, :, :kw], v_ref[r, ks:ks + kw, :],
                         preferred_element_type=F32)
            finish(r, row0, jnp.where(left, pv[:QBLK], pv[QBLK:]),
                   pair_tile(m[:QBLK], m[QBLK:]), pair_tile(den[:QBLK], den[QBLK:]))

    for c in range(min(DIL_AHEAD, len(chunks))):
        scores(c)
    for c in range(len(chunks)):
        if c + DIL_AHEAD < len(chunks):
            scores(c + DIL_AHEAD)
        softmax_pv(c)


def _dil_attn(proj, *, reach):
    batch, _, seq, _ = proj.shape
    n_hp = WIDTH // LANES

    def spec(first_tile, d, kind):
        return pl.BlockSpec((None, None, d, seq // d, LANES),
                            lambda b, hp: (b, first_tile + kind, 0, 0, hp))

    views = {d: proj.reshape(batch, N_STEPS, d, seq // d, WIDTH) for d in SLOT_DIL}
    f32_rows = pltpu.VMEM((seq, LANES), F32)
    return pl.pallas_call(
        functools.partial(_dil_kernel, seq=seq, reach=reach),
        out_shape=jax.ShapeDtypeStruct((batch, 16, seq // 16, WIDTH), BF16),
        grid=(batch, n_hp),
        in_specs=[spec(t0, d, kind) for t0, d in zip((T_G0, T_G1, T_G2), SLOT_DIL)
                  for kind in range(3)],
        out_specs=pl.BlockSpec((None, 16, seq // 16, LANES), lambda b, hp: (b, 0, 0, hp)),
        scratch_shapes=[f32_rows] * 4 + [
            pltpu.VMEM((3, QBLK, 2 * QBLK), F32),
            pltpu.VMEM((N_SBUF, DIL_CHUNK, 2 * QBLK, 2 * QBLK), F32),
            pltpu.VMEM((2, DIL_CHUNK, 2 * QBLK, 2 * QBLK), BF16)],
        compiler_params=pltpu.CompilerParams(
            dimension_semantics=("arbitrary", "arbitrary"),
            vmem_limit_bytes=VMEM_LIMIT),
        name="dil_attn",
    )(*[views[d] for d in SLOT_DIL for _ in range(3)])


def _na_kernel(q_ref, k_ref, v_ref, rpb_ref, qm_ref, km_ref, vm_ref,
               o_ref, om_ref, bias_scr, s_scr, p_scr, sm_scr, pm_scr, rdm_scr, *, rows):
    kr = min(NA_ROWS, rows)
    nk = kr * GRID_W
    n_var = 2 * NA_ROWS - kr
    left = lax.broadcasted_iota(jnp.int32, (1, LANES), 1) < HEAD_DIM
    keep_a = jnp.where(left, 1.0, 0.0).astype(BF16)
    keep_b = jnp.where(left, 0.0, 1.0).astype(BF16)

    @pl.when(pl.program_id(1) == 0)
    def _():
        qc = lax.broadcasted_iota(jnp.int32, (GRID_W, LANES), 0)
        kc = lax.broadcasted_iota(jnp.int32, (GRID_W, LANES), 1) % GRID_W
        rel = jnp.clip(kc - qc, -(NA_COLS - 1), NA_COLS - 1) + NA_COLS - 1
        c_start = jnp.clip(qc - NA_COLS // 2, 0, GRID_W - NA_COLS)
        valid = jnp.logical_and(kc >= c_start, kc < c_start + NA_COLS)
        for hh in range(LANES // HEAD_DIM):
            head = pl.program_id(0) * (LANES // HEAD_DIM) + hh
            for a in range(2 * NA_ROWS - 1):
                row = rpb_ref[pl.ds(head * (2 * NA_ROWS - 1) + a, 1), :]
                blk = jnp.take_along_axis(jnp.broadcast_to(row, (GRID_W, LANES)), rel, axis=1)
                blk = jnp.where(valid, blk * LOG2E, NEG)
                for j in range(kr):
                    if 0 <= a - j < n_var:
                        half = slice((j % 2) * GRID_W, (j % 2 + 1) * GRID_W)
                        bias_scr[hh, a - j, :, j * GRID_W:(j + 1) * GRID_W] = blk[:, half]

    def window(r):
        r_start = min(max(r - kr // 2, 0), rows - kr)
        return r * GRID_W, r_start * GRID_W, r_start - r + (NA_ROWS - 1)

    def scores(c):
        for t in range(NA_CHUNK):
            row0, ks, _ = window(c * NA_CHUNK + t)
            q2d = q_ref[row0:row0 + GRID_W, :]
            qq = jnp.concatenate([q2d * keep_a, q2d * keep_b], axis=0)
            s_scr[c % 2, t] = lax.dot_general(qq, k_ref[ks:ks + nk, :], (((1,), (1,)), ((), ())),
                                              preferred_element_type=F32)

    def softmax_pv(c):
        for t in range(NA_CHUNK):
            row0, ks, variant = window(c * NA_CHUNK + t)
            dens = []
            for hh in range(LANES // HEAD_DIM):
                rows_h = slice(hh * GRID_W, (hh + 1) * GRID_W)
                s = s_scr[c % 2, t, rows_h, :] + bias_scr[hh, variant]
                m = jnp.max(s, axis=-1, keepdims=True)
                p = jnp.exp2(s - m)
                dens.append(jnp.sum(p, axis=-1, keepdims=True))
                p_scr[t, rows_h, :] = p.astype(BF16)
            pv = jnp.dot(p_scr[t], v_ref[ks:ks + nk, :], preferred_element_type=F32)
            den_t = jnp.where(left, jnp.broadcast_to(dens[0], (GRID_W, LANES)),
                              jnp.broadcast_to(dens[1], (GRID_W, LANES)))
            o_ref[row0:row0 + GRID_W, :] = (
                jnp.where(left, pv[:GRID_W], pv[GRID_W:]) / den_t).astype(BF16)

    sm_scr[...] = lax.dot_general(qm_ref[...], km_ref[...], (((1,), (1,)), ((), ())),
                                  preferred_element_type=F32)

    def mem_softmax(i):
        rows_i = slice(i * QBLK, (i + 1) * QBLK)
        s = sm_scr[rows_i, :]
        m = jnp.max(s, axis=-1, keepdims=True)
        p = jnp.exp2(s - m)
        den = jnp.sum(p, axis=-1, keepdims=True)
        pm_scr[rows_i, :] = p.astype(BF16)
        rdm_scr[rows_i, :] = jnp.broadcast_to(1.0 / den, (QBLK, LANES))

    n_chunks = rows // NA_CHUNK
    mem_tiles = qm_ref.shape[0] // QBLK
    scores(0)
    for c in range(n_chunks):
        if c + 1 < n_chunks:
            scores(c + 1)
        softmax_pv(c)
        for i in range(c * mem_tiles // n_chunks, (c + 1) * mem_tiles // n_chunks):
            mem_softmax(i)
    pv = jnp.dot(pm_scr[...], vm_ref[...], preferred_element_type=F32)
    om_ref[...] = (pv * rdm_scr[...]).astype(BF16)


def _na_mem_attn(proj, rpb, kv_m):
    batch, _, seq, _ = proj.shape
    rows = seq // GRID_W
    kr = min(NA_ROWS, rows)
    nk = kr * GRID_W
    n_hp = WIDTH // LANES
    assert rpb.shape == (WIDTH // HEAD_DIM, 2 * NA_ROWS - 1, 2 * NA_COLS - 1)

    def in_spec(tile):
        return pl.BlockSpec((None, None, seq, LANES), lambda hp, b: (b, tile, 0, hp))

    rpb_rows = jnp.pad(rpb.reshape(-1, rpb.shape[-1]).astype(F32),
                       ((0, 0), (0, LANES - rpb.shape[-1])))
    mem_len = kv_m.shape[1]
    assert MEM_HEADS == n_hp and MEM_HEAD_DIM == LANES
    out_spec = pl.BlockSpec((None, seq, LANES), lambda hp, b: (b, 0, hp))
    return pl.pallas_call(
        functools.partial(_na_kernel, rows=rows),
        out_shape=(jax.ShapeDtypeStruct((batch, seq, WIDTH), BF16),
                   jax.ShapeDtypeStruct((batch, seq, WIDTH), BF16)),
        grid=(n_hp, batch),
        in_specs=[in_spec(T_NA), in_spec(T_NA + 1), in_spec(T_NA + 2),
                  pl.BlockSpec(rpb_rows.shape, lambda hp, b: (0, 0)),
                  in_spec(T_MEMQ),
                  pl.BlockSpec((None, mem_len, LANES), lambda hp, b: (b, 0, hp)),
                  pl.BlockSpec((None, mem_len, LANES), lambda hp, b: (b, 0, MEM_HEADS + hp))],
        out_specs=(out_spec, out_spec),
        scratch_shapes=[pltpu.VMEM((LANES // HEAD_DIM, 2 * NA_ROWS - kr, GRID_W, nk), F32),
                        pltpu.VMEM((2, NA_CHUNK, 2 * GRID_W, nk), F32),
                        pltpu.VMEM((NA_CHUNK, 2 * GRID_W, nk), BF16),
                        pltpu.VMEM((seq, mem_len), F32), pltpu.VMEM((seq, mem_len), BF16),
                        pltpu.VMEM((seq, LANES), F32)],
        compiler_params=pltpu.CompilerParams(
            dimension_semantics=("arbitrary", "arbitrary"), vmem_limit_bytes=VMEM_LIMIT),
        name="na_mem_attn",
    )(proj, proj, proj, rpb_rows, proj, kv_m, kv_m)


def _merge_kernel(oa, ga, ob, oc, gb, gc, za0, za1, zb0, zb1, zc0, zc1,
                  wa_f32, wb_f32, wc_f32, wo_f32, pn_ref, x_ref, out_ref,
                  u_scr, wa_ref, wb_ref, wc_ref, wo_ref, *, tm):
    @pl.when(pl.program_id(0) == 0)
    def _():
        for src, dst in ((wa_f32, wa_ref), (wb_f32, wb_ref), (wc_f32, wc_ref), (wo_f32, wo_ref)):
            dst[...] = src[...].astype(BF16)

    def silu_gated(o, silu_g):
        return o.astype(F32) * silu_g.astype(F32)

    seg = MERGE_ROWS // 16
    for blk in range(tm // MERGE_ROWS):
        rows = slice(blk * MERGE_ROWS, (blk + 1) * MERGE_ROWS)

        def project(u, w_ref, gate_lo, gate_hi):
            yb = jnp.dot(u.astype(BF16), w_ref[...], preferred_element_type=F32)
            gate = jnp.concatenate([gate_lo[rows, :], gate_hi[rows, :]], axis=-1)
            return gate.astype(F32) * yb

        p16_rows = slice(blk * seg, (blk + 1) * seg)
        ua = silu_gated(oa[:, p16_rows, :].reshape(MERGE_ROWS, WIDTH),
                        ga[:, p16_rows, :].reshape(MERGE_ROWS, WIDTH))
        for r in range(16):
            for c in range(WIDTH // LANES):
                u_scr[blk, c, pl.ds(r, seg, stride=16), :] = ua[r * seg:(r + 1) * seg,
                                                                c * LANES:(c + 1) * LANES]
        ua = jnp.concatenate([u_scr[blk, c] for c in range(WIDTH // LANES)], axis=-1)

        y = (project(ua, wa_ref, za0, za1)
             + project(silu_gated(ob[rows, :], gb[rows, :]), wb_ref, zb0, zb1)
             + project(silu_gated(oc[rows, :], gc[rows, :]), wc_ref, zc0, zc1))
        z = jnp.dot(y.astype(BF16), wo_ref[...], preferred_element_type=F32)
        ms = jnp.mean(z * z, axis=-1, keepdims=True)
        out_ref[rows, :] = x_ref[rows, :] + z * lax.rsqrt(ms + EPS) * pn_ref[...]


def _merge(x2d, proj, out_a16, out_b, out_c, wa, wb, wc, wo, post_norm, *, tm):
    rows, dm = x2d.shape
    batch, _, seq, _ = proj.shape
    per_b = seq // tm
    seg = tm // 16
    proj16 = proj.reshape(batch, N_STEPS, 16, seq // 16, WIDTH)

    def p16_spec(tile):
        return pl.BlockSpec((None, None, 16, seg, WIDTH),
                            lambda i: (i // per_b, tile, 0, i % per_b, 0))

    def nat_spec(tile):
        return pl.BlockSpec((None, None, tm, WIDTH), lambda i: (i // per_b, tile, i % per_b, 0))

    def rows_spec(width):
        return pl.BlockSpec((tm, width), lambda i: (i, 0))

    def full(shape):
        return pl.BlockSpec(shape, lambda i: (0, 0), pipeline_mode=pl.Buffered(1))

    return pl.pallas_call(
        functools.partial(_merge_kernel, tm=tm),
        out_shape=jax.ShapeDtypeStruct((rows, dm), F32),
        grid=(rows // tm,),
        in_specs=[pl.BlockSpec((None, 16, seg, WIDTH), lambda i: (i // per_b, 0, i % per_b, 0)),
                  p16_spec(T_GATE_A), rows_spec(WIDTH), rows_spec(WIDTH),
                  nat_spec(T_GATE_B), nat_spec(T_GATE_B + 1),
                  nat_spec(T_MERGE_A), nat_spec(T_MERGE_A + 1),
                  nat_spec(T_MERGE_B), nat_spec(T_MERGE_B + 1),
                  nat_spec(T_MERGE_B + 2), nat_spec(T_MERGE_B + 3),
                  full(wa.shape), full(wb.shape), full(wc.shape),
                  full(wo.shape), full(post_norm.shape), rows_spec(dm)],
        out_specs=pl.BlockSpec((tm, dm), lambda i: (i, 0)),
        scratch_shapes=[pltpu.VMEM((tm // MERGE_ROWS, WIDTH // LANES, MERGE_ROWS, LANES), F32)]
        + [pltpu.VMEM(w.shape, BF16) for w in (wa, wb, wc, wo)],
        compiler_params=pltpu.CompilerParams(
            dimension_semantics=("arbitrary",), vmem_limit_bytes=VMEM_LIMIT),
        name="merge",
    )(out_a16, proj16, out_b, out_c, proj, proj, proj, proj, proj, proj, proj, proj,
      wa, wb, wc, wo, post_norm, x2d)


def kernel(x, mem, pre_norm, w_in, merge_bias, na_rpb, mem_norm, w_mem_kv,
           w_branch_a, w_branch_b, w_branch_c, w_out, post_norm):
    b, s, dm = x.shape
    depth = pre_norm.shape[0]
    reach = (DIL_CONFIGS[0][0] // 2) // DIL_CONFIGS[0][1]
    assert all((w // 2) // d == reach for w, d in DIL_CONFIGS) and 2 * reach == QBLK
    assert tuple(d for _, d in DIL_CONFIGS) == SLOT_DIL
    cos_t, sin_t = _rope_tables(s)
    for layer in range(depth):
        x2d = x.reshape(b * s, dm)
        proj = _proj(x2d, pre_norm[layer][None], w_in[layer], merge_bias[layer], cos_t, sin_t,
                     batch=b, seq=s)

        out_a16 = _dil_attn(proj, reach=reach)
        mem2d = mem.reshape(b * mem.shape[1], dm)
        kv_m = _kv_proj(mem2d, mem_norm[layer][None], w_mem_kv[layer], tm=1024)
        out_b, out_c = _na_mem_attn(proj, na_rpb[layer], kv_m.reshape(b, mem.shape[1], -1))

        y = _merge(x2d, proj, out_a16, out_b.reshape(b * s, WIDTH), out_c.reshape(b * s, WIDTH),
                   w_branch_a[layer], w_branch_b[layer], w_branch_c[layer],
                   w_out[layer], post_norm[layer][None], tm=2 * MERGE_ROWS)
        x = y.reshape(b, s, dm)
    return x
```

```python
import functools
import math

import jax
import jax.numpy as jnp
import numpy as np
from jax import lax
from jax.experimental import pallas as pl
from jax.experimental.pallas import tpu as pltpu

D_MODEL = 1024
HEAD_DIM = 64
DIL_CONFIGS = ((128, 1), (512, 4), (2048, 16))
WIDTH = 512
NA_ROWS = 8
NA_COLS = 16
GRID_W = 64
MEM_HEADS = 4
MEM_HEAD_DIM = 128
ROPE_THETA = 500000.0
ROPE_DIM = HEAD_DIM // 4
ROPE_HALF = ROPE_DIM // 2
EPS = 1e-6
NEG = -1e30
LOG2E = math.log2(math.e)

LANES = 128
QBLK = 128
DIL_CHUNK = 1
DIL_AHEAD = 1
N_SBUF = DIL_AHEAD + 1
NA_CHUNK = 8
PROJ_ROWS = 1024
MERGE_ROWS = 512
VMEM_LIMIT = 56 * 1024 * 1024

_QS = HEAD_DIM ** -0.5 * LOG2E
_MS = MEM_HEAD_DIM ** -0.5 * LOG2E
EPI_ROPE, EPI_SCALE, EPI_SIGMOID, EPI_SILU = range(4)
STEPS = (
    (0, 0, EPI_ROPE, _QS), (1, 0, EPI_ROPE, 1.0), (2, 0, EPI_SCALE, 1.0),
    (9, 0, EPI_SCALE, _QS), (10, 0, EPI_SCALE, 1.0), (11, 0, EPI_SCALE, 1.0),
    (12, 0, EPI_SCALE, _MS),
    (14, 0, EPI_SILU, 1.0), (15, 0, EPI_SILU, 1.0),
    (18, 0, EPI_SIGMOID, 1.0), (19, 0, EPI_SIGMOID, 1.0),
    (20, 0, EPI_SIGMOID, 1.0), (21, 0, EPI_SIGMOID, 1.0),
    (16, 0, EPI_SIGMOID, 1.0), (17, 0, EPI_SIGMOID, 1.0),
    (3, 1, EPI_ROPE, _QS), (4, 1, EPI_ROPE, 1.0), (5, 1, EPI_SCALE, 1.0),
    (6, 2, EPI_ROPE, _QS), (7, 2, EPI_ROPE, 1.0), (8, 2, EPI_SCALE, 1.0),
    (13, 2, EPI_SILU, 1.0),
)
FIRST_MERGE_TILE = 16
N_STEPS = len(STEPS)
T_G0, T_NA, T_MEMQ, T_GATE_B, T_MERGE_B, T_MERGE_A = 0, 3, 6, 7, 9, 13
T_G1, T_G2, T_GATE_A = 15, 18, 21
SLOT_DIL = (1, 4, 16)

BF16 = jnp.bfloat16
F32 = jnp.float32


def _sigmoid(z):
    return 0.5 * jnp.tanh(0.5 * z) + 0.5


def _proj_kernel(wtile_ref, kind_ref, scale_ref, x_ref, g_ref, w_ref, mb_ref,
                 cos_ref, sin_ref, o_ref, h_ref, wb_ref, *, tm):
    j = pl.program_id(1)
    n_lane_tiles = x_ref.shape[1] // LANES

    @pl.when(j == 0)
    def _():
        rb = 256
        ssq = [jnp.sum(jnp.square(x_ref[r0:r0 + rb, :]), axis=-1, keepdims=True)
               for r0 in range(0, tm, rb)]
        rs = lax.rsqrt(jnp.concatenate(ssq, axis=0) * (1.0 / x_ref.shape[1]) + EPS)
        for c in range(n_lane_tiles):
            cs = slice(c * LANES, (c + 1) * LANES)
            h_ref[:, cs] = (x_ref[:, cs] * rs * g_ref[:, cs]).astype(BF16)

    q_scale = scale_ref[j]

    def step(epi, d):
        wb_ref[...] = w_ref[...].astype(BF16)
        lane = lax.broadcasted_iota(jnp.int32, (1, LANES), 1)
        first_half = lane % HEAD_DIM < ROPE_HALF
        if epi == EPI_SIGMOID:
            bias = mb_ref[pl.ds(jnp.maximum(wtile_ref[j] - FIRST_MERGE_TILE, 0), 1), :]
        block = PROJ_ROWS // 2 if (epi == EPI_ROPE and d <= 4) else PROJ_ROWS
        per_class = block // d
        for r0 in range(0, tm, block):
            rows = pl.ds(r0, block)
            acc = jnp.dot(h_ref[rows, :], wb_ref[...], preferred_element_type=F32)
            if epi == EPI_ROPE:
                c, s = cos_ref[rows, :], sin_ref[rows, :]
            for t in range(WIDTH // LANES):
                lanes = slice(t * LANES, (t + 1) * LANES)
                a = acc[:, lanes]
                if epi == EPI_ROPE:
                    partner = jnp.where(first_half, pltpu.roll(a, LANES - ROPE_HALF, 1),
                                        pltpu.roll(a, ROPE_HALF, 1))
                    a = (a * c + partner * s) * q_scale
                elif epi == EPI_SCALE:
                    a = a * q_scale
                elif epi == EPI_SIGMOID:
                    a = _sigmoid(a + bias[:, lanes])
                else:
                    a = a * _sigmoid(a)
                if d == 1:
                    o_ref[rows, lanes] = a.astype(BF16)
                else:
                    by_class = jnp.swapaxes(a.reshape(per_class, d, LANES), 0, 1)
                    for r in range(d):
                        o_ref[pl.ds(r * (tm // d) + r0 // d, per_class), lanes] = (
                            by_class[r].astype(BF16))

    for kind in sorted({st[2] * len(SLOT_DIL) + st[1] for st in STEPS}):
        epi, slot = divmod(kind, len(SLOT_DIL))
        pl.when(kind_ref[j] == kind)(functools.partial(step, epi, SLOT_DIL[slot]))


def _proj(x2d, gain, w, merge_bias, cos_t, sin_t, *, batch, seq):
    rows, dm = x2d.shape
    tm = seq
    prefetch = (jnp.asarray([st[0] for st in STEPS], jnp.int32),
                jnp.asarray([st[2] * len(SLOT_DIL) + st[1] for st in STEPS], jnp.int32),
                jnp.asarray([st[3] for st in STEPS], F32))
    mb_rows = merge_bias.reshape(-1, WIDTH)
    tab_spec = pl.BlockSpec((tm, LANES), lambda i, j, *_: (0, 0))
    return pl.pallas_call(
        functools.partial(_proj_kernel, tm=tm),
        out_shape=jax.ShapeDtypeStruct((batch, N_STEPS, seq, WIDTH), BF16),
        grid_spec=pltpu.PrefetchScalarGridSpec(
            num_scalar_prefetch=len(prefetch),
            grid=(rows // tm, N_STEPS),
            in_specs=[
                pl.BlockSpec((tm, dm), lambda i, j, *_: (i, 0)),
                pl.BlockSpec((1, dm), lambda i, j, *_: (0, 0)),
                pl.BlockSpec((dm, WIDTH), lambda i, j, wt, kd, sc: (0, wt[j])),
                pl.BlockSpec(mb_rows.shape, lambda i, j, *_: (0, 0)),
                tab_spec, tab_spec,
            ],
            out_specs=pl.BlockSpec((None, None, tm, WIDTH), lambda i, j, *_: (i, j, 0, 0)),
            scratch_shapes=[pltpu.VMEM((tm, dm), BF16), pltpu.VMEM((dm, WIDTH), BF16)]),
        compiler_params=pltpu.CompilerParams(
            dimension_semantics=("arbitrary", "arbitrary"),
            vmem_limit_bytes=VMEM_LIMIT),
        name="proj",
    )(*prefetch, x2d, gain, w, mb_rows, cos_t, sin_t)


def _rope_tables(seq):
    pos = np.arange(seq, dtype=np.float64)
    inv = ROPE_THETA ** (-np.arange(ROPE_HALF, dtype=np.float64) * 2.0 / ROPE_DIM)
    ang = pos[:, None] * inv[None, :]
    cos, sin = np.cos(ang), np.sin(ang)
    rest = HEAD_DIM - ROPE_DIM
    c = np.concatenate([cos, cos, np.ones((seq, rest))] * (LANES // HEAD_DIM), axis=-1)
    s = np.concatenate([-sin, sin, np.zeros((seq, rest))] * (LANES // HEAD_DIM), axis=-1)

    return jnp.asarray(c, F32), jnp.asarray(s, F32)


def _kv_kernel(x_ref, g_ref, w_ref, o_ref):
    xf = x_ref[...]
    ms = jnp.mean(xf * xf, axis=-1, keepdims=True)
    h = (xf * lax.rsqrt(ms + EPS) * g_ref[...]).astype(BF16)
    o_ref[...] = jnp.dot(h, w_ref[...].astype(BF16), preferred_element_type=F32).astype(BF16)


def _kv_proj(mem2d, gain, w, *, tm):
    rows, dm = mem2d.shape
    n_out = w.shape[1]
    return pl.pallas_call(
        _kv_kernel,
        out_shape=jax.ShapeDtypeStruct((rows, n_out), BF16),
        grid=(rows // tm,),
        in_specs=[pl.BlockSpec((tm, dm), lambda i: (i, 0)),
                  pl.BlockSpec((1, dm), lambda i: (0, 0)),
                  pl.BlockSpec((dm, n_out), lambda i: (0, 0))],
        out_specs=pl.BlockSpec((tm, n_out), lambda i: (i, 0)),
        compiler_params=pltpu.CompilerParams(
            dimension_semantics=("arbitrary",), vmem_limit_bytes=VMEM_LIMIT),
        name="kv_proj",
    )(mem2d, gain, w)


def _dil_kernel(q0, k0, v0, q1, k1, v1, q2, k2, v2, o_ref,
                og0, lg0, og1, lg1, bias_scr, s_scr, p_scr, *, seq, reach):
    left = lax.broadcasted_iota(jnp.int32, (1, LANES), 1) < HEAD_DIM
    keep_a = jnp.where(left, 1.0, 0.0).astype(BF16)
    keep_b = jnp.where(left, 0.0, 1.0).astype(BF16)
    kw_max = 2 * QBLK

    @pl.when(jnp.logical_and(pl.program_id(0) == 0, pl.program_id(1) == 0))
    def _():
        rel = (lax.broadcasted_iota(jnp.int32, (QBLK, kw_max), 0)
               - lax.broadcasted_iota(jnp.int32, (QBLK, kw_max), 1))
        for n in range(3):
            bias_scr[n] = jnp.where(jnp.abs(rel + n * reach) <= reach, 0.0, NEG)

    def pair_tile(col_a, col_b):
        return jnp.where(left, jnp.broadcast_to(col_a, (QBLK, LANES)),
                         jnp.broadcast_to(col_b, (QBLK, LANES)))

    def keep(o_scr, l_scr, length):
        def finish(r, row0, acc, m, den):
            rows = pl.ds(r * length + row0, QBLK)
            o_scr[rows, :] = acc / den
            l_scr[rows, :] = m + jnp.log2(den)
        return finish

    def combine(r, row0, acc2, m2, den2):
        rows0 = pl.ds(r, QBLK, stride=16)
        rows1 = pl.ds((r % 4) * (seq // 4) + r // 4, QBLK, stride=4)
        l0, l1 = lg0[rows0, :], lg1[rows1, :]
        m = jnp.maximum(jnp.maximum(l0, l1), m2)
        w0, w1, w2 = jnp.exp2(l0 - m), jnp.exp2(l1 - m), jnp.exp2(m2 - m)
        num = w0 * og0[rows0, :] + w1 * og1[rows1, :] + w2 * acc2
        o_ref[r] = (num / (w0 + w1 + w2 * den2)).astype(BF16)

    tiles = []
    for (q_ref, k_ref, v_ref), d, finish in (
            ((q0, k0, v0), 1, keep(og0, lg0, seq)),
            ((q1, k1, v1), 4, keep(og1, lg1, seq // 4)),
            ((q2, k2, v2), 16, combine)):
        length = seq // d
        kw = min(kw_max, length)
        for bi in range(seq // QBLK):
            r, row0 = divmod(bi * QBLK, length)
            ks = min(max(row0 - reach, 0), length - kw)
            tiles.append((q_ref, k_ref, v_ref, r, row0, ks, kw, finish))
    chunks = [tiles[i:i + DIL_CHUNK] for i in range(0, len(tiles), DIL_CHUNK)]

    def scores(c):
        for t, (q_ref, k_ref, _, r, row0, ks, kw, _) in enumerate(chunks[c]):
            q2d = q_ref[r, row0:row0 + QBLK, :]
            qq = jnp.concatenate([q2d * keep_a, q2d * keep_b], axis=0)
            s_scr[c % N_SBUF, t, :, :kw] = lax.dot_general(
                qq, k_ref[r, ks:ks + kw, :], (((1,), (1,)), ((), ())),
                preferred_element_type=F32)

    def softmax_pv(c):
        for t, (_, _, v_ref, r, row0, ks, kw, finish) in enumerate(chunks[c]):
            bias = bias_scr[(row0 - ks) // reach, :, :kw]
            s = s_scr[c % N_SBUF, t, :, :kw] + jnp.concatenate([bias, bias], axis=0)
            m = jnp.max(s, axis=-1, keepdims=True)
            p = jnp.exp2(s - m)
            den = jnp.sum(p, axis=-1, keepdims=True)
            p_scr[c % 2, t, :, :kw] = p.astype(BF16)
            pv = jnp.dot(p_scr[c % 2, t, :, :kw], v_ref[r, ks:ks + kw, :],
                         preferred_element_type=F32)
            finish(r, row0, jnp.where(left, pv[:QBLK], pv[QBLK:]),
                   pair_tile(m[:QBLK], m[QBLK:]), pair_tile(den[:QBLK], den[QBLK:]))

    for c in range(min(DIL_AHEAD, len(chunks))):
        scores(c)
    for c in range(len(chunks)):
        if c + DIL_AHEAD < len(chunks):
            scores(c + DIL_AHEAD)
        softmax_pv(c)


def _dil_attn(proj, *, reach):
    batch, _, seq, _ = proj.shape
    n_hp = WIDTH // LANES

    def spec(first_tile, d, kind):
        return pl.BlockSpec((None, None, d, seq // d, LANES),
                            lambda b, hp: (b, first_tile + kind, 0, 0, hp))

    views = {d: proj.reshape(batch, N_STEPS, d, seq // d, WIDTH) for d in SLOT_DIL}
    f32_rows = pltpu.VMEM((seq, LANES), F32)
    return pl.pallas_call(
        functools.partial(_dil_kernel, seq=seq, reach=reach),
        out_shape=jax.ShapeDtypeStruct((batch, 16, seq // 16, WIDTH), BF16),
        grid=(batch, n_hp),
        in_specs=[spec(t0, d, kind) for t0, d in zip((T_G0, T_G1, T_G2), SLOT_DIL)
                  for kind in range(3)],
        out_specs=pl.BlockSpec((None, 16, seq // 16, LANES), lambda b, hp: (b, 0, 0, hp)),
        scratch_shapes=[f32_rows] * 4 + [
            pltpu.VMEM((3, QBLK, 2 * QBLK), F32),
            pltpu.VMEM((N_SBUF, DIL_CHUNK, 2 * QBLK, 2 * QBLK), F32),
            pltpu.VMEM((2, DIL_CHUNK, 2 * QBLK, 2 * QBLK), BF16)],
        compiler_params=pltpu.CompilerParams(
            dimension_semantics=("arbitrary", "arbitrary"),
            vmem_limit_bytes=VMEM_LIMIT),
        name="dil_attn",
    )(*[views[d] for d in SLOT_DIL for _ in range(3)])


def _na_kernel(q_ref, k_ref, v_ref, rpb_ref, qm_ref, km_ref, vm_ref,
               o_ref, om_ref, bias_scr, s_scr, p_scr, sm_scr, pm_scr, rdm_scr, *, rows):
    kr = min(NA_ROWS, rows)
    nk = kr * GRID_W
    n_var = 2 * NA_ROWS - kr
    left = lax.broadcasted_iota(jnp.int32, (1, LANES), 1) < HEAD_DIM
    keep_a = jnp.where(left, 1.0, 0.0).astype(BF16)
    keep_b = jnp.where(left, 0.0, 1.0).astype(BF16)

    @pl.when(pl.program_id(1) == 0)
    def _():
        qc = lax.broadcasted_iota(jnp.int32, (GRID_W, LANES), 0)
        kc = lax.broadcasted_iota(jnp.int32, (GRID_W, LANES), 1) % GRID_W
        rel = jnp.clip(kc - qc, -(NA_COLS - 1), NA_COLS - 1) + NA_COLS - 1
        c_start = jnp.clip(qc - NA_COLS // 2, 0, GRID_W - NA_COLS)
        valid = jnp.logical_and(kc >= c_start, kc < c_start + NA_COLS)
        for hh in range(LANES // HEAD_DIM):
            head = pl.program_id(0) * (LANES // HEAD_DIM) + hh
            for a in range(2 * NA_ROWS - 1):
                row = rpb_ref[pl.ds(head * (2 * NA_ROWS - 1) + a, 1), :]
                blk = jnp.take_along_axis(jnp.broadcast_to(row, (GRID_W, LANES)), rel, axis=1)
                blk = jnp.where(valid, blk * LOG2E, NEG)
                for j in range(kr):
                    if 0 <= a - j < n_var:
                        half = slice((j % 2) * GRID_W, (j % 2 + 1) * GRID_W)
                        bias_scr[hh, a - j, :, j * GRID_W:(j + 1) * GRID_W] = blk[:, half]

    def window(r):
        r_start = min(max(r - kr // 2, 0), rows - kr)
        return r * GRID_W, r_start * GRID_W, r_start - r + (NA_ROWS - 1)

    def scores(c):
        for t in range(NA_CHUNK):
            row0, ks, _ = window(c * NA_CHUNK + t)
            q2d = q_ref[row0:row0 + GRID_W, :]
            qq = jnp.concatenate([q2d * keep_a, q2d * keep_b], axis=0)
            s_scr[c % 2, t] = lax.dot_general(qq, k_ref[ks:ks + nk, :], (((1,), (1,)), ((), ())),
                                              preferred_element_type=F32)

    def softmax_pv(c):
        for t in range(NA_CHUNK):
            row0, ks, variant = window(c * NA_CHUNK + t)
            dens = []
            for hh in range(LANES // HEAD_DIM):
                rows_h = slice(hh * GRID_W, (hh + 1) * GRID_W)
                s = s_scr[c % 2, t, rows_h, :] + bias_scr[hh, variant]
                m = jnp.max(s, axis=-1, keepdims=True)
                p = jnp.exp2(s - m)
                dens.append(jnp.sum(p, axis=-1, keepdims=True))
                p_scr[t, rows_h, :] = p.astype(BF16)
            pv = jnp.dot(p_scr[t], v_ref[ks:ks + nk, :], preferred_element_type=F32)
            den_t = jnp.where(left, jnp.broadcast_to(dens[0], (GRID_W, LANES)),
                              jnp.broadcast_to(dens[1], (GRID_W, LANES)))
            o_ref[row0:row0 + GRID_W, :] = (
                jnp.where(left, pv[:GRID_W], pv[GRID_W:]) / den_t).astype(BF16)

    sm_scr[...] = lax.dot_general(qm_ref[...], km_ref[...], (((1,), (1,)), ((), ())),
                                  preferred_element_type=F32)

    def mem_softmax(i):
        rows_i = slice(i * QBLK, (i + 1) * QBLK)
        s = sm_scr[rows_i, :]
        m = jnp.max(s, axis=-1, keepdims=True)
        p = jnp.exp2(s - m)
        den = jnp.sum(p, axis=-1, keepdims=True)
        pm_scr[rows_i, :] = p.astype(BF16)
        rdm_scr[rows_i, :] = jnp.broadcast_to(1.0 / den, (QBLK, LANES))

    n_chunks = rows // NA_CHUNK
    mem_tiles = qm_ref.shape[0] // QBLK
    scores(0)
    for c in range(n_chunks):
        if c + 1 < n_chunks:
            scores(c + 1)
        softmax_pv(c)
        for i in range(c * mem_tiles // n_chunks, (c + 1) * mem_tiles // n_chunks):
            mem_softmax(i)
    pv = jnp.dot(pm_scr[...], vm_ref[...], preferred_element_type=F32)
    om_ref[...] = (pv * rdm_scr[...]).astype(BF16)


def _na_mem_attn(proj, rpb, kv_m):
    batch, _, seq, _ = proj.shape
    rows = seq // GRID_W
    kr = min(NA_ROWS, rows)
    nk = kr * GRID_W
    n_hp = WIDTH // LANES
    assert rpb.shape == (WIDTH // HEAD_DIM, 2 * NA_ROWS - 1, 2 * NA_COLS - 1)

    def in_spec(tile):
        return pl.BlockSpec((None, None, seq, LANES), lambda hp, b: (b, tile, 0, hp))

    rpb_rows = jnp.pad(rpb.reshape(-1, rpb.shape[-1]).astype(F32),
                       ((0, 0), (0, LANES - rpb.shape[-1])))
    mem_len = kv_m.shape[1]
    assert MEM_HEADS == n_hp and MEM_HEAD_DIM == LANES
    out_spec = pl.BlockSpec((None, seq, LANES), lambda hp, b: (b, 0, hp))
    return pl.pallas_call(
        functools.partial(_na_kernel, rows=rows),
        out_shape=(jax.ShapeDtypeStruct((batch, seq, WIDTH), BF16),
                   jax.ShapeDtypeStruct((batch, seq, WIDTH), BF16)),
        grid=(n_hp, batch),
        in_specs=[in_spec(T_NA), in_spec(T_NA + 1), in_spec(T_NA + 2),
                  pl.BlockSpec(rpb_rows.shape, lambda hp, b: (0, 0)),
                  in_spec(T_MEMQ),
                  pl.BlockSpec((None, mem_len, LANES), lambda hp, b: (b, 0, hp)),
                  pl.BlockSpec((None, mem_len, LANES), lambda hp, b: (b, 0, MEM_HEADS + hp))],
        out_specs=(out_spec, out_spec),
        scratch_shapes=[pltpu.VMEM((LANES // HEAD_DIM, 2 * NA_ROWS - kr, GRID_W, nk), F32),
                        pltpu.VMEM((2, NA_CHUNK, 2 * GRID_W, nk), F32),
                        pltpu.VMEM((NA_CHUNK, 2 * GRID_W, nk), BF16),
                        pltpu.VMEM((seq, mem_len), F32), pltpu.VMEM((seq, mem_len), BF16),
                        pltpu.VMEM((seq, LANES), F32)],
        compiler_params=pltpu.CompilerParams(
            dimension_semantics=("arbitrary", "arbitrary"), vmem_limit_bytes=VMEM_LIMIT),
        name="na_mem_attn",
    )(proj, proj, proj, rpb_rows, proj, kv_m, kv_m)


def _merge_kernel(oa, ga, ob, oc, gb, gc, za0, za1, zb0, zb1, zc0, zc1,
                  wa_f32, wb_f32, wc_f32, wo_f32, pn_ref, x_ref, out_ref,
                  u_scr, wa_ref, wb_ref, wc_ref, wo_ref, *, tm):
    @pl.when(pl.program_id(0) == 0)
    def _():
        for src, dst in ((wa_f32, wa_ref), (wb_f32, wb_ref), (wc_f32, wc_ref), (wo_f32, wo_ref)):
            dst[...] = src[...].astype(BF16)

    def silu_gated(o, silu_g):
        return o.astype(F32) * silu_g.astype(F32)

    seg = MERGE_ROWS // 16
    for blk in range(tm // MERGE_ROWS):
        rows = slice(blk * MERGE_ROWS, (blk + 1) * MERGE_ROWS)

        def project(u, w_ref, gate_lo, gate_hi):
            yb = jnp.dot(u.astype(BF16), w_ref[...], preferred_element_type=F32)
            gate = jnp.concatenate([gate_lo[rows, :], gate_hi[rows, :]], axis=-1)
            return gate.astype(F32) * yb

        p16_rows = slice(blk * seg, (blk + 1) * seg)
        ua = silu_gated(oa[:, p16_rows, :].reshape(MERGE_ROWS, WIDTH),
                        ga[:, p16_rows, :].reshape(MERGE_ROWS, WIDTH))
        for r in range(16):
            for c in range(WIDTH // LANES):
                u_scr[blk, c, pl.ds(r, seg, stride=16), :] = ua[r * seg:(r + 1) * seg,
                                                                c * LANES:(c + 1) * LANES]
        ua = jnp.concatenate([u_scr[blk, c] for c in range(WIDTH // LANES)], axis=-1)

        y = (project(ua, wa_ref, za0, za1)
             + project(silu_gated(ob[rows, :], gb[rows, :]), wb_ref, zb0, zb1)
             + project(silu_gated(oc[rows, :], gc[rows, :]), wc_ref, zc0, zc1))
        z = jnp.dot(y.astype(BF16), wo_ref[...], preferred_element_type=F32)
        ms = jnp.mean(z * z, axis=-1, keepdims=True)
        out_ref[rows, :] = x_ref[rows, :] + z * lax.rsqrt(ms + EPS) * pn_ref[...]


def _merge(x2d, proj, out_a16, out_b, out_c, wa, wb, wc, wo, post_norm, *, tm):
    rows, dm = x2d.shape
    batch, _, seq, _ = proj.shape
    per_b = seq // tm
    seg = tm // 16
    proj16 = proj.reshape(batch, N_STEPS, 16, seq // 16, WIDTH)

    def p16_spec(tile):
        return pl.BlockSpec((None, None, 16, seg, WIDTH),
                            lambda i: (i // per_b, tile, 0, i % per_b, 0))

    def nat_spec(tile):
        return pl.BlockSpec((None, None, tm, WIDTH), lambda i: (i // per_b, tile, i % per_b, 0))

    def rows_spec(width):
        return pl.BlockSpec((tm, width), lambda i: (i, 0))

    def full(shape):
        return pl.BlockSpec(shape, lambda i: (0, 0), pipeline_mode=pl.Buffered(1))

    return pl.pallas_call(
        functools.partial(_merge_kernel, tm=tm),
        out_shape=jax.ShapeDtypeStruct((rows, dm), F32),
        grid=(rows // tm,),
        in_specs=[pl.BlockSpec((None, 16, seg, WIDTH), lambda i: (i // per_b, 0, i % per_b, 0)),
                  p16_spec(T_GATE_A), rows_spec(WIDTH), rows_spec(WIDTH),
                  nat_spec(T_GATE_B), nat_spec(T_GATE_B + 1),
                  nat_spec(T_MERGE_A), nat_spec(T_MERGE_A + 1),
                  nat_spec(T_MERGE_B), nat_spec(T_MERGE_B + 1),
                  nat_spec(T_MERGE_B + 2), nat_spec(T_MERGE_B + 3),
                  full(wa.shape), full(wb.shape), full(wc.shape),
                  full(wo.shape), full(post_norm.shape), rows_spec(dm)],
        out_specs=pl.BlockSpec((tm, dm), lambda i: (i, 0)),
        scratch_shapes=[pltpu.VMEM((tm // MERGE_ROWS, WIDTH // LANES, MERGE_ROWS, LANES), F32)]
        + [pltpu.VMEM(w.shape, BF16) for w in (wa, wb, wc, wo)],
        compiler_params=pltpu.CompilerParams(
            dimension_semantics=("arbitrary",), vmem_limit_bytes=VMEM_LIMIT),
        name="merge",
    )(out_a16, proj16, out_b, out_c, proj, proj, proj, proj, proj, proj, proj, proj,
      wa, wb, wc, wo, post_norm, x2d)


def kernel(x, mem, pre_norm, w_in, merge_bias, na_rpb, mem_norm, w_mem_kv,
           w_branch_a, w_branch_b, w_branch_c, w_out, post_norm):
    b, s, dm = x.shape
    depth = pre_norm.shape[0]
    reach = (DIL_CONFIGS[0][0] // 2) // DIL_CONFIGS[0][1]
    assert all((w // 2) // d == reach for w, d in DIL_CONFIGS) and 2 * reach == QBLK
    assert tuple(d for _, d in DIL_CONFIGS) == SLOT_DIL
    cos_t, sin_t = _rope_tables(s)
    for layer in range(depth):
        x2d = x.reshape(b * s, dm)
        proj = _proj(x2d, pre_norm[layer][None], w_in[layer], merge_bias[layer], cos_t, sin_t,
                     batch=b, seq=s)

        out_a16 = _dil_attn(proj, reach=reach)
        mem2d = mem.reshape(b * mem.shape[1], dm)
        kv_m = _kv_proj(mem2d, mem_norm[layer][None], w_mem_kv[layer], tm=1024)
        out_b, out_c = _na_mem_attn(proj, na_rpb[layer], kv_m.reshape(b, mem.shape[1], -1))

        y = _merge(x2d, proj, out_a16, out_b.reshape(b * s, WIDTH), out_c.reshape(b * s, WIDTH),
                   w_branch_a[layer], w_branch_b[layer], w_branch_c[layer],
                   w_out[layer], post_norm[layer][None], tm=MERGE_ROWS)
        x = y.reshape(b, s, dm)
    return x
```

```python
import functools
import math

import jax
import jax.numpy as jnp
import numpy as np
from jax import lax
from jax.experimental import pallas as pl
from jax.experimental.pallas import tpu as pltpu

HEAD_DIM = 64
DIL_CONFIGS = ((128, 1), (512, 4), (2048, 16))
WIDTH = 512
NA_ROWS = 8
NA_COLS = 16
GRID_W = 64
MEM_HEADS = 4
MEM_HEAD_DIM = 128
ROPE_THETA = 500000.0
ROPE_DIM = HEAD_DIM // 4
ROPE_HALF = ROPE_DIM // 2
EPS = 1e-6
NEG = -1e30
LOG2E = math.log2(math.e)

LANES = 128
QBLK = 128
DIL_CHUNK = 1
DIL_AHEAD = 1
N_SBUF = DIL_AHEAD + 1
NA_CHUNK = 8
PROJ_ROWS = 1024
MERGE_ROWS = 512
VMEM_LIMIT = 56 * 1024 * 1024

_QS = HEAD_DIM ** -0.5 * LOG2E
_MS = MEM_HEAD_DIM ** -0.5 * LOG2E
EPI_ROPE, EPI_SCALE, EPI_SIGMOID, EPI_SILU = range(4)
STEPS = (
    (0, 0, EPI_ROPE, _QS), (1, 0, EPI_ROPE, 1.0), (2, 0, EPI_SCALE, 1.0),
    (9, 0, EPI_SCALE, _QS), (10, 0, EPI_SCALE, 1.0), (11, 0, EPI_SCALE, 1.0),
    (12, 0, EPI_SCALE, _MS),
    (14, 0, EPI_SILU, 1.0), (15, 0, EPI_SILU, 1.0),
    (18, 0, EPI_SIGMOID, 1.0), (19, 0, EPI_SIGMOID, 1.0),
    (20, 0, EPI_SIGMOID, 1.0), (21, 0, EPI_SIGMOID, 1.0),
    (16, 0, EPI_SIGMOID, 1.0), (17, 0, EPI_SIGMOID, 1.0),
    (3, 1, EPI_ROPE, _QS), (4, 1, EPI_ROPE, 1.0), (5, 1, EPI_SCALE, 1.0),
    (6, 2, EPI_ROPE, _QS), (7, 2, EPI_ROPE, 1.0), (8, 2, EPI_SCALE, 1.0),
    (13, 2, EPI_SILU, 1.0),
)
FIRST_MERGE_TILE = 16
N_STEPS = len(STEPS)
T_G0, T_NA, T_MEMQ, T_GATE_B, T_MERGE_B, T_MERGE_A = 0, 3, 6, 7, 9, 13
T_G1, T_G2, T_GATE_A = 15, 18, 21
SLOT_DIL = (1, 4, 16)

BF16 = jnp.bfloat16
F32 = jnp.float32


def _sigmoid(z):
    return 0.5 * jnp.tanh(0.5 * z) + 0.5


def _proj_kernel(wtile_ref, kind_ref, scale_ref, x_ref, g_ref, w_ref, mb_ref,
                 cos_ref, sin_ref, o_ref, h_ref, wb_ref, *, tm):
    j = pl.program_id(1)
    n_lane_tiles = x_ref.shape[1] // LANES

    @pl.when(j == 0)
    def _():
        rb = 256
        ssq = [jnp.sum(jnp.square(x_ref[r0:r0 + rb, :]), axis=-1, keepdims=True)
               for r0 in range(0, tm, rb)]
        rs = lax.rsqrt(jnp.concatenate(ssq, axis=0) * (1.0 / x_ref.shape[1]) + EPS)
        for c in range(n_lane_tiles):
            cs = slice(c * LANES, (c + 1) * LANES)
            h_ref[:, cs] = (x_ref[:, cs] * rs * g_ref[:, cs]).astype(BF16)

    q_scale = scale_ref[j]

    def step(epi, d):
        wb_ref[...] = w_ref[...].astype(BF16)
        lane = lax.broadcasted_iota(jnp.int32, (1, LANES), 1)
        first_half = lane % HEAD_DIM < ROPE_HALF
        if epi == EPI_SIGMOID:
            bias = mb_ref[pl.ds(jnp.maximum(wtile_ref[j] - FIRST_MERGE_TILE, 0), 1), :]
        block = PROJ_ROWS // 4 if epi == EPI_ROPE else PROJ_ROWS
        per_class = block // d
        for r0 in range(0, tm, block):
            rows = pl.ds(r0, block)
            acc = jnp.dot(h_ref[rows, :], wb_ref[...], preferred_element_type=F32)
            if epi == EPI_ROPE:
                c, s = cos_ref[rows, :], sin_ref[rows, :]
            for t in range(WIDTH // LANES):
                lanes = slice(t * LANES, (t + 1) * LANES)
                a = acc[:, lanes]
                if epi == EPI_ROPE:
                    partner = jnp.where(first_half, pltpu.roll(a, LANES - ROPE_HALF, 1),
                                        pltpu.roll(a, ROPE_HALF, 1))
                    a = (a * c + partner * s) * q_scale
                elif epi == EPI_SCALE:
                    a = a * q_scale
                elif epi == EPI_SIGMOID:
                    a = _sigmoid(a + bias[:, lanes])
                else:
                    a = a * _sigmoid(a)
                if d == 1:
                    o_ref[rows, lanes] = a.astype(BF16)
                else:
                    by_class = jnp.swapaxes(a.reshape(per_class, d, LANES), 0, 1)
                    for r in range(d):
                        o_ref[pl.ds(r * (tm // d) + r0 // d, per_class), lanes] = (
                            by_class[r].astype(BF16))

    for kind in sorted({st[2] * len(SLOT_DIL) + st[1] for st in STEPS}):
        epi, slot = divmod(kind, len(SLOT_DIL))
        pl.when(kind_ref[j] == kind)(functools.partial(step, epi, SLOT_DIL[slot]))


def _proj(x2d, gain, w, merge_bias, cos_t, sin_t, *, batch, seq):
    rows, dm = x2d.shape
    tm = seq
    prefetch = (jnp.asarray([st[0] for st in STEPS], jnp.int32),
                jnp.asarray([st[2] * len(SLOT_DIL) + st[1] for st in STEPS], jnp.int32),
                jnp.asarray([st[3] for st in STEPS], F32))
    mb_rows = merge_bias.reshape(-1, WIDTH)
    tab_spec = pl.BlockSpec((tm, LANES), lambda i, j, *_: (0, 0))
    return pl.pallas_call(
        functools.partial(_proj_kernel, tm=tm),
        out_shape=jax.ShapeDtypeStruct((batch, N_STEPS, seq, WIDTH), BF16),
        grid_spec=pltpu.PrefetchScalarGridSpec(
            num_scalar_prefetch=len(prefetch),
            grid=(rows // tm, N_STEPS),
            in_specs=[
                pl.BlockSpec((tm, dm), lambda i, j, *_: (i, 0)),
                pl.BlockSpec((1, dm), lambda i, j, *_: (0, 0)),
                pl.BlockSpec((dm, WIDTH), lambda i, j, wt, kd, sc: (0, wt[j])),
                pl.BlockSpec(mb_rows.shape, lambda i, j, *_: (0, 0)),
                tab_spec, tab_spec,
            ],
            out_specs=pl.BlockSpec((None, None, tm, WIDTH), lambda i, j, *_: (i, j, 0, 0)),
            scratch_shapes=[pltpu.VMEM((tm, dm), BF16), pltpu.VMEM((dm, WIDTH), BF16)]),
        compiler_params=pltpu.CompilerParams(
            dimension_semantics=("arbitrary", "arbitrary"),
            vmem_limit_bytes=VMEM_LIMIT),
        name="proj",
    )(*prefetch, x2d, gain, w, mb_rows, cos_t, sin_t)


def _rope_tables(seq):
    pos = np.arange(seq, dtype=np.float64)
    inv = ROPE_THETA ** (-np.arange(ROPE_HALF, dtype=np.float64) * 2.0 / ROPE_DIM)
    ang = pos[:, None] * inv[None, :]
    cos, sin = np.cos(ang), np.sin(ang)
    rest = HEAD_DIM - ROPE_DIM
    c = np.concatenate([cos, cos, np.ones((seq, rest))] * (LANES // HEAD_DIM), axis=-1)
    s = np.concatenate([-sin, sin, np.zeros((seq, rest))] * (LANES // HEAD_DIM), axis=-1)

    return jnp.asarray(c, F32), jnp.asarray(s, F32)


def _kv_kernel(x_ref, g_ref, w_ref, o_ref):
    xf = x_ref[...]
    ms = jnp.mean(xf * xf, axis=-1, keepdims=True)
    h = (xf * lax.rsqrt(ms + EPS) * g_ref[...]).astype(BF16)
    o_ref[...] = jnp.dot(h, w_ref[...].astype(BF16), preferred_element_type=F32).astype(BF16)


def _kv_proj(mem2d, gain, w, *, tm):
    rows, dm = mem2d.shape
    n_out = w.shape[1]
    return pl.pallas_call(
        _kv_kernel,
        out_shape=jax.ShapeDtypeStruct((rows, n_out), BF16),
        grid=(rows // tm,),
        in_specs=[pl.BlockSpec((tm, dm), lambda i: (i, 0)),
                  pl.BlockSpec((1, dm), lambda i: (0, 0)),
                  pl.BlockSpec((dm, n_out), lambda i: (0, 0))],
        out_specs=pl.BlockSpec((tm, n_out), lambda i: (i, 0)),
        compiler_params=pltpu.CompilerParams(
            dimension_semantics=("arbitrary",), vmem_limit_bytes=VMEM_LIMIT),
        name="kv_proj",
    )(mem2d, gain, w)


def _dil_kernel(q0, k0, v0, q1, k1, v1, q2, k2, v2, o_ref,
                og0, lg0, og1, lg1, bias_scr, s_scr, p_scr, *, seq, reach):
    left = lax.broadcasted_iota(jnp.int32, (1, LANES), 1) < HEAD_DIM
    keep_a = jnp.where(left, 1.0, 0.0).astype(BF16)
    keep_b = jnp.where(left, 0.0, 1.0).astype(BF16)
    kw_max = 2 * QBLK

    @pl.when(jnp.logical_and(pl.program_id(0) == 0, pl.program_id(1) == 0))
    def _():
        rel = (lax.broadcasted_iota(jnp.int32, (QBLK, kw_max), 0)
               - lax.broadcasted_iota(jnp.int32, (QBLK, kw_max), 1))
        for n in range(3):
            bias_scr[n] = jnp.where(jnp.abs(rel + n * reach) <= reach, 0.0, NEG)

    def pair_tile(col_a, col_b):
        return jnp.where(left, jnp.broadcast_to(col_a, (QBLK, LANES)),
                         jnp.broadcast_to(col_b, (QBLK, LANES)))

    def keep(o_scr, l_scr, length):
        def finish(r, row0, acc, m, den):
            rows = pl.ds(r * length + row0, QBLK)
            o_scr[rows, :] = acc / den
            l_scr[rows, :] = m + jnp.log2(den)
        return finish

    def combine(r, row0, acc2, m2, den2):
        rows0 = pl.ds(r, QBLK, stride=16)
        rows1 = pl.ds((r % 4) * (seq // 4) + r // 4, QBLK, stride=4)
        l0, l1 = lg0[rows0, :], lg1[rows1, :]
        m = jnp.maximum(jnp.maximum(l0, l1), m2)
        w0, w1, w2 = jnp.exp2(l0 - m), jnp.exp2(l1 - m), jnp.exp2(m2 - m)
        num = w0 * og0[rows0, :] + w1 * og1[rows1, :] + w2 * acc2
        o_ref[r] = (num / (w0 + w1 + w2 * den2)).astype(BF16)

    tiles = []
    for (q_ref, k_ref, v_ref), d, finish in (
            ((q0, k0, v0), 1, keep(og0, lg0, seq)),
            ((q1, k1, v1), 4, keep(og1, lg1, seq // 4)),
            ((q2, k2, v2), 16, combine)):
        length = seq // d
        kw = min(kw_max, length)
        for bi in range(seq // QBLK):
            r, row0 = divmod(bi * QBLK, length)
            ks = min(max(row0 - reach, 0), length - kw)
            tiles.append((q_ref, k_ref, v_ref, r, row0, ks, kw, finish))
    chunks = [tiles[i:i + DIL_CHUNK] for i in range(0, len(tiles), DIL_CHUNK)]

    def scores(c):
        for t, (q_ref, k_ref, _, r, row0, ks, kw, _) in enumerate(chunks[c]):
            q2d = q_ref[r, row0:row0 + QBLK, :]
            qq = jnp.concatenate([q2d * keep_a, q2d * keep_b], axis=0)
            s_scr[c % N_SBUF, t, :, :kw] = lax.dot_general(
                qq, k_ref[r, ks:ks + kw, :], (((1,), (1,)), ((), ())),
                preferred_element_type=F32)

    def softmax_pv(c):
        for t, (_, _, v_ref, r, row0, ks, kw, finish) in enumerate(chunks[c]):
            bias = bias_scr[(row0 - ks) // reach, :, :kw]
            s = s_scr[c % N_SBUF, t, :, :kw] + jnp.concatenate([bias, bias], axis=0)
            m = jnp.max(s, axis=-1, keepdims=True)
            p = jnp.exp2(s - m)
            den = jnp.sum(p, axis=-1, keepdims=True)
            p_scr[c % 2, t, :, :kw] = p.astype(BF16)
            pv = jnp.dot(p_scr[c % 2, t, :, :kw], v_ref[r, ks:ks + kw, :],
                         preferred_element_type=F32)
            finish(r, row0, jnp.where(left, pv[:QBLK], pv[QBLK:]),
                   pair_tile(m[:QBLK], m[QBLK:]), pair_tile(den[:QBLK], den[QBLK:]))

    for c in range(min(DIL_AHEAD, len(chunks))):
        scores(c)
    for c in range(len(chunks)):
        if c + DIL_AHEAD < len(chunks):
            scores(c + DIL_AHEAD)
        softmax_pv(c)


def _dil_attn(proj, *, reach):
    batch, _, seq, _ = proj.shape
    n_hp = WIDTH // LANES

    def spec(first_tile, d, kind):
        return pl.BlockSpec((None, None, d, seq // d, LANES),
                            lambda b, hp: (b, first_tile + kind, 0, 0, hp))

    views = {d: proj.reshape(batch, N_STEPS, d, seq // d, WIDTH) for d in SLOT_DIL}
    f32_rows = pltpu.VMEM((seq, LANES), F32)
    return pl.pallas_call(
        functools.partial(_dil_kernel, seq=seq, reach=reach),
        out_shape=jax.ShapeDtypeStruct((batch, 16, seq // 16, WIDTH), BF16),
        grid=(batch, n_hp),
        in_specs=[spec(t0, d, kind) for t0, d in zip((T_G0, T_G1, T_G2), SLOT_DIL)
                  for kind in range(3)],
        out_specs=pl.BlockSpec((None, 16, seq // 16, LANES), lambda b, hp: (b, 0, 0, hp)),
        scratch_shapes=[f32_rows] * 4 + [
            pltpu.VMEM((3, QBLK, 2 * QBLK), F32),
            pltpu.VMEM((N_SBUF, DIL_CHUNK, 2 * QBLK, 2 * QBLK), F32),
            pltpu.VMEM((2, DIL_CHUNK, 2 * QBLK, 2 * QBLK), BF16)],
        compiler_params=pltpu.CompilerParams(
            dimension_semantics=("arbitrary", "arbitrary"),
            vmem_limit_bytes=VMEM_LIMIT),
        name="dil_attn",
    )(*[views[d] for d in SLOT_DIL for _ in range(3)])


def _na_kernel(q_ref, k_ref, v_ref, rpb_ref, qm_ref, km_ref, vm_ref,
               o_ref, om_ref, bias_scr, s_scr, p_scr, sm_scr, pm_scr, rdm_scr, *, rows):
    kr = min(NA_ROWS, rows)
    nk = kr * GRID_W
    n_var = 2 * NA_ROWS - kr
    left = lax.broadcasted_iota(jnp.int32, (1, LANES), 1) < HEAD_DIM
    keep_a = jnp.where(left, 1.0, 0.0).astype(BF16)
    keep_b = jnp.where(left, 0.0, 1.0).astype(BF16)

    @pl.when(pl.program_id(1) == 0)
    def _():
        qc = lax.broadcasted_iota(jnp.int32, (GRID_W, LANES), 0)
        kc = lax.broadcasted_iota(jnp.int32, (GRID_W, LANES), 1) % GRID_W
        rel = jnp.clip(kc - qc, -(NA_COLS - 1), NA_COLS - 1) + NA_COLS - 1
        c_start = jnp.clip(qc - NA_COLS // 2, 0, GRID_W - NA_COLS)
        valid = jnp.logical_and(kc >= c_start, kc < c_start + NA_COLS)
        for hh in range(LANES // HEAD_DIM):
            head = pl.program_id(0) * (LANES // HEAD_DIM) + hh
            for a in range(2 * NA_ROWS - 1):
                row = rpb_ref[pl.ds(head * (2 * NA_ROWS - 1) + a, 1), :]
                blk = jnp.take_along_axis(jnp.broadcast_to(row, (GRID_W, LANES)), rel, axis=1)
                blk = jnp.where(valid, blk * LOG2E, NEG)
                for j in range(kr):
                    if 0 <= a - j < n_var:
                        half = slice((j % 2) * GRID_W, (j % 2 + 1) * GRID_W)
                        bias_scr[hh, a - j, :, j * GRID_W:(j + 1) * GRID_W] = blk[:, half]

    def window(r):
        r_start = min(max(r - kr // 2, 0), rows - kr)
        return r * GRID_W, r_start * GRID_W, r_start - r + (NA_ROWS - 1)

    def scores(c):
        for t in range(NA_CHUNK):
            row0, ks, _ = window(c * NA_CHUNK + t)
            q2d = q_ref[row0:row0 + GRID_W, :]
            qq = jnp.concatenate([q2d * keep_a, q2d * keep_b], axis=0)
            s_scr[c % 2, t] = lax.dot_general(qq, k_ref[ks:ks + nk, :], (((1,), (1,)), ((), ())),
                                              preferred_element_type=F32)

    def softmax_pv(c):
        for t in range(NA_CHUNK):
            row0, ks, variant = window(c * NA_CHUNK + t)
            dens = []
            for hh in range(LANES // HEAD_DIM):
                rows_h = slice(hh * GRID_W, (hh + 1) * GRID_W)
                s = s_scr[c % 2, t, rows_h, :] + bias_scr[hh, variant]
                m = jnp.max(s, axis=-1, keepdims=True)
                p = jnp.exp2(s - m)
                dens.append(jnp.sum(p, axis=-1, keepdims=True))
                p_scr[t, rows_h, :] = p.astype(BF16)
            pv = jnp.dot(p_scr[t], v_ref[ks:ks + nk, :], preferred_element_type=F32)
            den_t = jnp.where(left, jnp.broadcast_to(dens[0], (GRID_W, LANES)),
                              jnp.broadcast_to(dens[1], (GRID_W, LANES)))
            o_ref[row0:row0 + GRID_W, :] = (
                jnp.where(left, pv[:GRID_W], pv[GRID_W:]) / den_t).astype(BF16)

    sm_scr[...] = lax.dot_general(qm_ref[...], km_ref[...], (((1,), (1,)), ((), ())),
                                  preferred_element_type=F32)

    def mem_softmax(i):
        rows_i = slice(i * QBLK, (i + 1) * QBLK)
        s = sm_scr[rows_i, :]
        m = jnp.max(s, axis=-1, keepdims=True)
        p = jnp.exp2(s - m)
        den = jnp.sum(p, axis=-1, keepdims=True)
        pm_scr[rows_i, :] = p.astype(BF16)
        rdm_scr[rows_i, :] = jnp.broadcast_to(1.0 / den, (QBLK, LANES))

    n_chunks = rows // NA_CHUNK
    mem_tiles = qm_ref.shape[0] // QBLK
    scores(0)
    for c in range(n_chunks):
        if c + 1 < n_chunks:
            scores(c + 1)
        softmax_pv(c)
        for i in range(c * mem_tiles // n_chunks, (c + 1) * mem_tiles // n_chunks):
            mem_softmax(i)
    pv = jnp.dot(pm_scr[...], vm_ref[...], preferred_element_type=F32)
    om_ref[...] = (pv * rdm_scr[...]).astype(BF16)


def _na_mem_attn(proj, rpb, kv_m):
    batch, _, seq, _ = proj.shape
    rows = seq // GRID_W
    kr = min(NA_ROWS, rows)
    nk = kr * GRID_W
    n_hp = WIDTH // LANES
    assert rpb.shape == (WIDTH // HEAD_DIM, 2 * NA_ROWS - 1, 2 * NA_COLS - 1)

    def in_spec(tile):
        return pl.BlockSpec((None, None, seq, LANES), lambda hp, b: (b, tile, 0, hp))

    rpb_rows = jnp.pad(rpb.reshape(-1, rpb.shape[-1]).astype(F32),
                       ((0, 0), (0, LANES - rpb.shape[-1])))
    mem_len = kv_m.shape[1]
    assert MEM_HEADS == n_hp and MEM_HEAD_DIM == LANES
    out_spec = pl.BlockSpec((None, seq, LANES), lambda hp, b: (b, 0, hp))
    return pl.pallas_call(
        functools.partial(_na_kernel, rows=rows),
        out_shape=(jax.ShapeDtypeStruct((batch, seq, WIDTH), BF16),
                   jax.ShapeDtypeStruct((batch, seq, WIDTH), BF16)),
        grid=(n_hp, batch),
        in_specs=[in_spec(T_NA), in_spec(T_NA + 1), in_spec(T_NA + 2),
                  pl.BlockSpec(rpb_rows.shape, lambda hp, b: (0, 0)),
                  in_spec(T_MEMQ),
                  pl.BlockSpec((None, mem_len, LANES), lambda hp, b: (b, 0, hp)),
                  pl.BlockSpec((None, mem_len, LANES), lambda hp, b: (b, 0, MEM_HEADS + hp))],
        out_specs=(out_spec, out_spec),
        scratch_shapes=[pltpu.VMEM((LANES // HEAD_DIM, 2 * NA_ROWS - kr, GRID_W, nk), F32),
                        pltpu.VMEM((2, NA_CHUNK, 2 * GRID_W, nk), F32),
                        pltpu.VMEM((NA_CHUNK, 2 * GRID_W, nk), BF16),
                        pltpu.VMEM((seq, mem_len), F32), pltpu.VMEM((seq, mem_len), BF16),
                        pltpu.VMEM((seq, LANES), F32)],
        compiler_params=pltpu.CompilerParams(
            dimension_semantics=("arbitrary", "arbitrary"), vmem_limit_bytes=VMEM_LIMIT),
        name="na_mem_attn",
    )(proj, proj, proj, rpb_rows, proj, kv_m, kv_m)


def _merge_kernel(oa, ga, ob, oc, gb, gc, za0, za1, zb0, zb1, zc0, zc1,
                  wa_f32, wb_f32, wc_f32, wo_f32, pn_ref, x_ref, out_ref,
                  u_scr, wa_ref, wb_ref, wc_ref, wo_ref, *, tm):
    @pl.when(pl.program_id(0) == 0)
    def _():
        for src, dst in ((wa_f32, wa_ref), (wb_f32, wb_ref), (wc_f32, wc_ref), (wo_f32, wo_ref)):
            dst[...] = src[...].astype(BF16)

    def silu_gated(o, silu_g):
        return o.astype(F32) * silu_g.astype(F32)

    seg = MERGE_ROWS // 16
    for blk in range(tm // MERGE_ROWS):
        rows = slice(blk * MERGE_ROWS, (blk + 1) * MERGE_ROWS)

        def project(u, w_ref, gate_lo, gate_hi):
            yb = jnp.dot(u.astype(BF16), w_ref[...], preferred_element_type=F32)
            gate = jnp.concatenate([gate_lo[rows, :], gate_hi[rows, :]], axis=-1)
            return gate.astype(F32) * yb

        p16_rows = slice(blk * seg, (blk + 1) * seg)
        ua = silu_gated(oa[:, p16_rows, :].reshape(MERGE_ROWS, WIDTH),
                        ga[:, p16_rows, :].reshape(MERGE_ROWS, WIDTH))
        for r in range(16):
            for c in range(WIDTH // LANES):
                u_scr[blk, c, pl.ds(r, seg, stride=16), :] = ua[r * seg:(r + 1) * seg,
                                                                c * LANES:(c + 1) * LANES]
        ua = jnp.concatenate([u_scr[blk, c] for c in range(WIDTH // LANES)], axis=-1)

        y = (project(ua, wa_ref, za0, za1)
             + project(silu_gated(ob[rows, :], gb[rows, :]), wb_ref, zb0, zb1)
             + project(silu_gated(oc[rows, :], gc[rows, :]), wc_ref, zc0, zc1))
        z = jnp.dot(y.astype(BF16), wo_ref[...], preferred_element_type=F32)
        ms = jnp.mean(z * z, axis=-1, keepdims=True)
        out_ref[rows, :] = x_ref[rows, :] + z * lax.rsqrt(ms + EPS) * pn_ref[...]


def _merge(x2d, proj, out_a16, out_b, out_c, wa, wb, wc, wo, post_norm, *, tm):
    rows, dm = x2d.shape
    batch, _, seq, _ = proj.shape
    per_b = seq // tm
    seg = tm // 16
    proj16 = proj.reshape(batch, N_STEPS, 16, seq // 16, WIDTH)

    def p16_spec(tile):
        return pl.BlockSpec((None, None, 16, seg, WIDTH),
                            lambda i: (i // per_b, tile, 0, i % per_b, 0))

    def nat_spec(tile):
        return pl.BlockSpec((None, None, tm, WIDTH), lambda i: (i // per_b, tile, i % per_b, 0))

    def rows_spec(width):
        return pl.BlockSpec((tm, width), lambda i: (i, 0))

    def full(shape):
        return pl.BlockSpec(shape, lambda i: (0, 0), pipeline_mode=pl.Buffered(1))

    return pl.pallas_call(
        functools.partial(_merge_kernel, tm=tm),
        out_shape=jax.ShapeDtypeStruct((rows, dm), F32),
        grid=(rows // tm,),
        in_specs=[pl.BlockSpec((None, 16, seg, WIDTH), lambda i: (i // per_b, 0, i % per_b, 0)),
                  p16_spec(T_GATE_A), rows_spec(WIDTH), rows_spec(WIDTH),
                  nat_spec(T_GATE_B), nat_spec(T_GATE_B + 1),
                  nat_spec(T_MERGE_A), nat_spec(T_MERGE_A + 1),
                  nat_spec(T_MERGE_B), nat_spec(T_MERGE_B + 1),
                  nat_spec(T_MERGE_B + 2), nat_spec(T_MERGE_B + 3),
                  full(wa.shape), full(wb.shape), full(wc.shape),
                  full(wo.shape), full(post_norm.shape), rows_spec(dm)],
        out_specs=pl.BlockSpec((tm, dm), lambda i: (i, 0)),
        scratch_shapes=[pltpu.VMEM((tm // MERGE_ROWS, WIDTH // LANES, MERGE_ROWS, LANES), F32)]
        + [pltpu.VMEM(w.shape, BF16) for w in (wa, wb, wc, wo)],
        compiler_params=pltpu.CompilerParams(
            dimension_semantics=("arbitrary",), vmem_limit_bytes=VMEM_LIMIT),
        name="merge",
    )(out_a16, proj16, out_b, out_c, proj, proj, proj, proj, proj, proj, proj, proj,
      wa, wb, wc, wo, post_norm, x2d)


def kernel(x, mem, pre_norm, w_in, merge_bias, na_rpb, mem_norm, w_mem_kv,
           w_branch_a, w_branch_b, w_branch_c, w_out, post_norm):
    b, s, dm = x.shape
    depth = pre_norm.shape[0]
    reach = (DIL_CONFIGS[0][0] // 2) // DIL_CONFIGS[0][1]
    assert all((w // 2) // d == reach for w, d in DIL_CONFIGS) and 2 * reach == QBLK
    assert tuple(d for _, d in DIL_CONFIGS) == SLOT_DIL
    cos_t, sin_t = _rope_tables(s)
    for layer in range(depth):
        x2d = x.reshape(b * s, dm)
        proj = _proj(x2d, pre_norm[layer][None], w_in[layer], merge_bias[layer], cos_t, sin_t,
                     batch=b, seq=s)

        out_a16 = _dil_attn(proj, reach=reach)
        mem2d = mem.reshape(b * mem.shape[1], dm)
        kv_m = _kv_proj(mem2d, mem_norm[layer][None], w_mem_kv[layer], tm=1024)
        out_b, out_c = _na_mem_attn(proj, na_rpb[layer], kv_m.reshape(b, mem.shape[1], -1))

        y = _merge(x2d, proj, out_a16, out_b.reshape(b * s, WIDTH), out_c.reshape(b * s, WIDTH),
                   w_branch_a[layer], w_branch_b[layer], w_branch_c[layer],
                   w_out[layer], post_norm[layer][None], tm=MERGE_ROWS)
        x = y.reshape(b, s, dm)
    return x
```

```python
import functools
import math

import jax
import jax.numpy as jnp
import numpy as np
from jax import lax
from jax.experimental import pallas as pl
from jax.experimental.pallas import tpu as pltpu

HEAD_DIM = 64
DIL_CONFIGS = ((128, 1), (512, 4), (2048, 16))
WIDTH = 512
NA_ROWS = 8
NA_COLS = 16
GRID_W = 64
MEM_HEADS = 4
MEM_HEAD_DIM = 128
ROPE_THETA = 500000.0
ROPE_DIM = HEAD_DIM // 4
ROPE_HALF = ROPE_DIM // 2
EPS = 1e-6
NEG = -1e30
LOG2E = math.log2(math.e)

LANES = 128
QBLK = 128
DIL_CHUNK = 1
DIL_AHEAD = 1
N_SBUF = DIL_AHEAD + 1
NA_CHUNK = 8
PROJ_ROWS = 1024
MERGE_ROWS = 512
VMEM_LIMIT = 56 * 1024 * 1024

_QS = HEAD_DIM ** -0.5 * LOG2E
_MS = MEM_HEAD_DIM ** -0.5 * LOG2E
EPI_ROPE, EPI_SCALE, EPI_SIGMOID, EPI_SILU = range(4)
STEPS = (
    (0, 0, EPI_ROPE, _QS), (1, 0, EPI_ROPE, 1.0), (2, 0, EPI_SCALE, 1.0),
    (9, 0, EPI_SCALE, _QS), (10, 0, EPI_SCALE, 1.0), (11, 0, EPI_SCALE, 1.0),
    (12, 0, EPI_SCALE, _MS),
    (14, 0, EPI_SILU, 1.0), (15, 0, EPI_SILU, 1.0),
    (18, 0, EPI_SIGMOID, 1.0), (19, 0, EPI_SIGMOID, 1.0),
    (20, 0, EPI_SIGMOID, 1.0), (21, 0, EPI_SIGMOID, 1.0),
    (16, 0, EPI_SIGMOID, 1.0), (17, 0, EPI_SIGMOID, 1.0),
    (3, 1, EPI_ROPE, _QS), (4, 1, EPI_ROPE, 1.0), (5, 1, EPI_SCALE, 1.0),
    (6, 2, EPI_ROPE, _QS), (7, 2, EPI_ROPE, 1.0), (8, 2, EPI_SCALE, 1.0),
    (13, 2, EPI_SILU, 1.0),
)
FIRST_MERGE_TILE = 16
N_STEPS = len(STEPS)
T_G0, T_NA, T_MEMQ, T_GATE_B, T_MERGE_B, T_MERGE_A = 0, 3, 6, 7, 9, 13
T_G1, T_G2, T_GATE_A = 15, 18, 21
SLOT_DIL = (1, 4, 16)

BF16 = jnp.bfloat16
F32 = jnp.float32


def _sigmoid(z):
    return 0.5 * jnp.tanh(0.5 * z) + 0.5


def _proj_kernel(wtile_ref, kind_ref, scale_ref, x_ref, g_ref, w_ref, mb_ref,
                 cos_ref, sin_ref, o_ref, h_ref, wb_ref, *, tm):
    j = pl.program_id(1)
    n_lane_tiles = x_ref.shape[1] // LANES

    @pl.when(j == 0)
    def _():
        rb = 256
        ssq = [jnp.sum(jnp.square(x_ref[r0:r0 + rb, :]), axis=-1, keepdims=True)
               for r0 in range(0, tm, rb)]
        rs = lax.rsqrt(jnp.concatenate(ssq, axis=0) * (1.0 / x_ref.shape[1]) + EPS)
        for c in range(n_lane_tiles):
            cs = slice(c * LANES, (c + 1) * LANES)
            h_ref[:, cs] = (x_ref[:, cs] * rs * g_ref[:, cs]).astype(BF16)

    q_scale = scale_ref[j]

    def step(epi, d):
        wb_ref[...] = w_ref[...].astype(BF16)
        lane = lax.broadcasted_iota(jnp.int32, (1, LANES), 1)
        first_half = lane % HEAD_DIM < ROPE_HALF
        if epi == EPI_SIGMOID:
            bias = mb_ref[pl.ds(jnp.maximum(wtile_ref[j] - FIRST_MERGE_TILE, 0), 1), :]
        block = max(PROJ_ROWS // 8, 16 * d) if epi == EPI_ROPE else PROJ_ROWS
        per_class = block // d
        for r0 in range(0, tm, block):
            rows = pl.ds(r0, block)
            acc = jnp.dot(h_ref[rows, :], wb_ref[...], preferred_element_type=F32)
            if epi == EPI_ROPE:
                c, s = cos_ref[rows, :], sin_ref[rows, :]
            for t in range(WIDTH // LANES):
                lanes = slice(t * LANES, (t + 1) * LANES)
                a = acc[:, lanes]
                if epi == EPI_ROPE:
                    partner = jnp.where(first_half, pltpu.roll(a, LANES - ROPE_HALF, 1),
                                        pltpu.roll(a, ROPE_HALF, 1))
                    a = (a * c + partner * s) * q_scale
                elif epi == EPI_SCALE:
                    a = a * q_scale
                elif epi == EPI_SIGMOID:
                    a = _sigmoid(a + bias[:, lanes])
                else:
                    a = a * _sigmoid(a)
                if d == 1:
                    o_ref[rows, lanes] = a.astype(BF16)
                else:
                    by_class = jnp.swapaxes(a.reshape(per_class, d, LANES), 0, 1)
                    for r in range(d):
                        o_ref[pl.ds(r * (tm // d) + r0 // d, per_class), lanes] = (
                            by_class[r].astype(BF16))

    for kind in sorted({st[2] * len(SLOT_DIL) + st[1] for st in STEPS}):
        epi, slot = divmod(kind, len(SLOT_DIL))
        pl.when(kind_ref[j] == kind)(functools.partial(step, epi, SLOT_DIL[slot]))


def _proj(x2d, gain, w, merge_bias, cos_t, sin_t, *, batch, seq):
    rows, dm = x2d.shape
    tm = seq
    prefetch = (jnp.asarray([st[0] for st in STEPS], jnp.int32),
                jnp.asarray([st[2] * len(SLOT_DIL) + st[1] for st in STEPS], jnp.int32),
                jnp.asarray([st[3] for st in STEPS], F32))
    mb_rows = merge_bias.reshape(-1, WIDTH)
    tab_spec = pl.BlockSpec((tm, LANES), lambda i, j, *_: (0, 0))
    return pl.pallas_call(
        functools.partial(_proj_kernel, tm=tm),
        out_shape=jax.ShapeDtypeStruct((batch, N_STEPS, seq, WIDTH), BF16),
        grid_spec=pltpu.PrefetchScalarGridSpec(
            num_scalar_prefetch=len(prefetch),
            grid=(rows // tm, N_STEPS),
            in_specs=[
                pl.BlockSpec((tm, dm), lambda i, j, *_: (i, 0)),
                pl.BlockSpec((1, dm), lambda i, j, *_: (0, 0)),
                pl.BlockSpec((dm, WIDTH), lambda i, j, wt, kd, sc: (0, wt[j])),
                pl.BlockSpec(mb_rows.shape, lambda i, j, *_: (0, 0)),
                tab_spec, tab_spec,
            ],
            out_specs=pl.BlockSpec((None, None, tm, WIDTH), lambda i, j, *_: (i, j, 0, 0)),
            scratch_shapes=[pltpu.VMEM((tm, dm), BF16), pltpu.VMEM((dm, WIDTH), BF16)]),
        compiler_params=pltpu.CompilerParams(
            dimension_semantics=("arbitrary", "arbitrary"),
            vmem_limit_bytes=VMEM_LIMIT),
        name="proj",
    )(*prefetch, x2d, gain, w, mb_rows, cos_t, sin_t)


def _rope_tables(seq):
    pos = np.arange(seq, dtype=np.float64)
    inv = ROPE_THETA ** (-np.arange(ROPE_HALF, dtype=np.float64) * 2.0 / ROPE_DIM)
    ang = pos[:, None] * inv[None, :]
    cos, sin = np.cos(ang), np.sin(ang)
    rest = HEAD_DIM - ROPE_DIM
    c = np.concatenate([cos, cos, np.ones((seq, rest))] * (LANES // HEAD_DIM), axis=-1)
    s = np.concatenate([-sin, sin, np.zeros((seq, rest))] * (LANES // HEAD_DIM), axis=-1)

    return jnp.asarray(c, F32), jnp.asarray(s, F32)


def _kv_kernel(x_ref, g_ref, w_ref, o_ref):
    xf = x_ref[...]
    ms = jnp.mean(xf * xf, axis=-1, keepdims=True)
    h = (xf * lax.rsqrt(ms + EPS) * g_ref[...]).astype(BF16)
    o_ref[...] = jnp.dot(h, w_ref[...].astype(BF16), preferred_element_type=F32).astype(BF16)


def _kv_proj(mem2d, gain, w, *, tm):
    rows, dm = mem2d.shape
    n_out = w.shape[1]
    return pl.pallas_call(
        _kv_kernel,
        out_shape=jax.ShapeDtypeStruct((rows, n_out), BF16),
        grid=(rows // tm,),
        in_specs=[pl.BlockSpec((tm, dm), lambda i: (i, 0)),
                  pl.BlockSpec((1, dm), lambda i: (0, 0)),
                  pl.BlockSpec((dm, n_out), lambda i: (0, 0))],
        out_specs=pl.BlockSpec((tm, n_out), lambda i: (i, 0)),
        compiler_params=pltpu.CompilerParams(
            dimension_semantics=("arbitrary",), vmem_limit_bytes=VMEM_LIMIT),
        name="kv_proj",
    )(mem2d, gain, w)


def _dil_kernel(q0, k0, v0, q1, k1, v1, q2, k2, v2, o_ref,
                og0, lg0, og1, lg1, bias_scr, s_scr, p_scr, *, seq, reach):
    left = lax.broadcasted_iota(jnp.int32, (1, LANES), 1) < HEAD_DIM
    keep_a = jnp.where(left, 1.0, 0.0).astype(BF16)
    keep_b = jnp.where(left, 0.0, 1.0).astype(BF16)
    kw_max = 2 * QBLK

    @pl.when(jnp.logical_and(pl.program_id(0) == 0, pl.program_id(1) == 0))
    def _():
        rel = (lax.broadcasted_iota(jnp.int32, (QBLK, kw_max), 0)
               - lax.broadcasted_iota(jnp.int32, (QBLK, kw_max), 1))
        for n in range(3):
            bias_scr[n] = jnp.where(jnp.abs(rel + n * reach) <= reach, 0.0, NEG)

    def pair_tile(col_a, col_b):
        return jnp.where(left, jnp.broadcast_to(col_a, (QBLK, LANES)),
                         jnp.broadcast_to(col_b, (QBLK, LANES)))

    def keep(o_scr, l_scr, length):
        def finish(r, row0, acc, m, den):
            rows = pl.ds(r * length + row0, QBLK)
            o_scr[rows, :] = acc / den
            l_scr[rows, :] = m + jnp.log2(den)
        return finish

    def combine(r, row0, acc2, m2, den2):
        rows0 = pl.ds(r, QBLK, stride=16)
        rows1 = pl.ds((r % 4) * (seq // 4) + r // 4, QBLK, stride=4)
        l0, l1 = lg0[rows0, :], lg1[rows1, :]
        m = jnp.maximum(jnp.maximum(l0, l1), m2)
        w0, w1, w2 = jnp.exp2(l0 - m), jnp.exp2(l1 - m), jnp.exp2(m2 - m)
        num = w0 * og0[rows0, :] + w1 * og1[rows1, :] + w2 * acc2
        o_ref[r] = (num / (w0 + w1 + w2 * den2)).astype(BF16)

    tiles = []
    for (q_ref, k_ref, v_ref), d, finish in (
            ((q0, k0, v0), 1, keep(og0, lg0, seq)),
            ((q1, k1, v1), 4, keep(og1, lg1, seq // 4)),
            ((q2, k2, v2), 16, combine)):
        length = seq // d
        kw = min(kw_max, length)
        for bi in range(seq // QBLK):
            r, row0 = divmod(bi * QBLK, length)
            ks = min(max(row0 - reach, 0), length - kw)
            tiles.append((q_ref, k_ref, v_ref, r, row0, ks, kw, finish))
    chunks = [tiles[i:i + DIL_CHUNK] for i in range(0, len(tiles), DIL_CHUNK)]

    def scores(c):
        for t, (q_ref, k_ref, _, r, row0, ks, kw, _) in enumerate(chunks[c]):
            q2d = q_ref[r, row0:row0 + QBLK, :]
            qq = jnp.concatenate([q2d * keep_a, q2d * keep_b], axis=0)
            s_scr[c % N_SBUF, t, :, :kw] = lax.dot_general(
                qq, k_ref[r, ks:ks + kw, :], (((1,), (1,)), ((), ())),
                preferred_element_type=F32)

    def softmax_pv(c):
        for t, (_, _, v_ref, r, row0, ks, kw, finish) in enumerate(chunks[c]):
            bias = bias_scr[(row0 - ks) // reach, :, :kw]
            s = s_scr[c % N_SBUF, t, :, :kw] + jnp.concatenate([bias, bias], axis=0)
            m = jnp.max(s, axis=-1, keepdims=True)
            p = jnp.exp2(s - m)
            den = jnp.sum(p, axis=-1, keepdims=True)
            p_scr[c % 2, t, :, :kw] = p.astype(BF16)
            pv = jnp.dot(p_scr[c % 2, t, :, :kw], v_ref[r, ks:ks + kw, :],
                         preferred_element_type=F32)
            finish(r, row0, jnp.where(left, pv[:QBLK], pv[QBLK:]),
                   pair_tile(m[:QBLK], m[QBLK:]), pair_tile(den[:QBLK], den[QBLK:]))

    for c in range(min(DIL_AHEAD, len(chunks))):
        scores(c)
    for c in range(len(chunks)):
        if c + DIL_AHEAD < len(chunks):
            scores(c + DIL_AHEAD)
        softmax_pv(c)


def _dil_attn(proj, *, reach):
    batch, _, seq, _ = proj.shape
    n_hp = WIDTH // LANES

    def spec(first_tile, d, kind):
        return pl.BlockSpec((None, None, d, seq // d, LANES),
                            lambda b, hp: (b, first_tile + kind, 0, 0, hp))

    views = {d: proj.reshape(batch, N_STEPS, d, seq // d, WIDTH) for d in SLOT_DIL}
    f32_rows = pltpu.VMEM((seq, LANES), F32)
    return pl.pallas_call(
        functools.partial(_dil_kernel, seq=seq, reach=reach),
        out_shape=jax.ShapeDtypeStruct((batch, 16, seq // 16, WIDTH), BF16),
        grid=(batch, n_hp),
        in_specs=[spec(t0, d, kind) for t0, d in zip((T_G0, T_G1, T_G2), SLOT_DIL)
                  for kind in range(3)],
        out_specs=pl.BlockSpec((None, 16, seq // 16, LANES), lambda b, hp: (b, 0, 0, hp)),
        scratch_shapes=[f32_rows] * 4 + [
            pltpu.VMEM((3, QBLK, 2 * QBLK), F32),
            pltpu.VMEM((N_SBUF, DIL_CHUNK, 2 * QBLK, 2 * QBLK), F32),
            pltpu.VMEM((2, DIL_CHUNK, 2 * QBLK, 2 * QBLK), BF16)],
        compiler_params=pltpu.CompilerParams(
            dimension_semantics=("arbitrary", "arbitrary"),
            vmem_limit_bytes=VMEM_LIMIT),
        name="dil_attn",
    )(*[views[d] for d in SLOT_DIL for _ in range(3)])


def _na_kernel(q_ref, k_ref, v_ref, rpb_ref, qm_ref, km_ref, vm_ref,
               o_ref, om_ref, bias_scr, s_scr, p_scr, sm_scr, pm_scr, rdm_scr, *, rows):
    kr = min(NA_ROWS, rows)
    nk = kr * GRID_W
    n_var = 2 * NA_ROWS - kr
    left = lax.broadcasted_iota(jnp.int32, (1, LANES), 1) < HEAD_DIM
    keep_a = jnp.where(left, 1.0, 0.0).astype(BF16)
    keep_b = jnp.where(left, 0.0, 1.0).astype(BF16)

    @pl.when(pl.program_id(1) == 0)
    def _():
        qc = lax.broadcasted_iota(jnp.int32, (GRID_W, LANES), 0)
        kc = lax.broadcasted_iota(jnp.int32, (GRID_W, LANES), 1) % GRID_W
        rel = jnp.clip(kc - qc, -(NA_COLS - 1), NA_COLS - 1) + NA_COLS - 1
        c_start = jnp.clip(qc - NA_COLS // 2, 0, GRID_W - NA_COLS)
        valid = jnp.logical_and(kc >= c_start, kc < c_start + NA_COLS)
        for hh in range(LANES // HEAD_DIM):
            head = pl.program_id(0) * (LANES // HEAD_DIM) + hh
            for a in range(2 * NA_ROWS - 1):
                row = rpb_ref[pl.ds(head * (2 * NA_ROWS - 1) + a, 1), :]
                blk = jnp.take_along_axis(jnp.broadcast_to(row, (GRID_W, LANES)), rel, axis=1)
                blk = jnp.where(valid, blk * LOG2E, NEG)
                for j in range(kr):
                    if 0 <= a - j < n_var:
                        half = slice((j % 2) * GRID_W, (j % 2 + 1) * GRID_W)
                        bias_scr[hh, a - j, :, j * GRID_W:(j + 1) * GRID_W] = blk[:, half]

    def window(r):
        r_start = min(max(r - kr // 2, 0), rows - kr)
        return r * GRID_W, r_start * GRID_W, r_start - r + (NA_ROWS - 1)

    def scores(c):
        for t in range(NA_CHUNK):
            row0, ks, _ = window(c * NA_CHUNK + t)
            q2d = q_ref[row0:row0 + GRID_W, :]
            qq = jnp.concatenate([q2d * keep_a, q2d * keep_b], axis=0)
            s_scr[c % 2, t] = lax.dot_general(qq, k_ref[ks:ks + nk, :], (((1,), (1,)), ((), ())),
                                              preferred_element_type=F32)

    def softmax_pv(c):
        for t in range(NA_CHUNK):
            row0, ks, variant = window(c * NA_CHUNK + t)
            dens = []
            for hh in range(LANES // HEAD_DIM):
                rows_h = slice(hh * GRID_W, (hh + 1) * GRID_W)
                s = s_scr[c % 2, t, rows_h, :] + bias_scr[hh, variant]
                m = jnp.max(s, axis=-1, keepdims=True)
                p = jnp.exp2(s - m)
                dens.append(jnp.sum(p, axis=-1, keepdims=True))
                p_scr[t, rows_h, :] = p.astype(BF16)
            pv = jnp.dot(p_scr[t], v_ref[ks:ks + nk, :], preferred_element_type=F32)
            den_t = jnp.where(left, jnp.broadcast_to(dens[0], (GRID_W, LANES)),
                              jnp.broadcast_to(dens[1], (GRID_W, LANES)))
            o_ref[row0:row0 + GRID_W, :] = (
                jnp.where(left, pv[:GRID_W], pv[GRID_W:]) / den_t).astype(BF16)

    sm_scr[...] = lax.dot_general(qm_ref[...], km_ref[...], (((1,), (1,)), ((), ())),
                                  preferred_element_type=F32)

    def mem_softmax(i):
        rows_i = slice(i * QBLK, (i + 1) * QBLK)
        s = sm_scr[rows_i, :]
        m = jnp.max(s, axis=-1, keepdims=True)
        p = jnp.exp2(s - m)
        den = jnp.sum(p, axis=-1, keepdims=True)
        pm_scr[rows_i, :] = p.astype(BF16)
        rdm_scr[rows_i, :] = jnp.broadcast_to(1.0 / den, (QBLK, LANES))

    n_chunks = rows // NA_CHUNK
    mem_tiles = qm_ref.shape[0] // QBLK
    scores(0)
    for c in range(n_chunks):
        if c + 1 < n_chunks:
            scores(c + 1)
        softmax_pv(c)
        for i in range(c * mem_tiles // n_chunks, (c + 1) * mem_tiles // n_chunks):
            mem_softmax(i)
    pv = jnp.dot(pm_scr[...], vm_ref[...], preferred_element_type=F32)
    om_ref[...] = (pv * rdm_scr[...]).astype(BF16)


def _na_mem_attn(proj, rpb, kv_m):
    batch, _, seq, _ = proj.shape
    rows = seq // GRID_W
    kr = min(NA_ROWS, rows)
    nk = kr * GRID_W
    n_hp = WIDTH // LANES
    assert rpb.shape == (WIDTH // HEAD_DIM, 2 * NA_ROWS - 1, 2 * NA_COLS - 1)

    def in_spec(tile):
        return pl.BlockSpec((None, None, seq, LANES), lambda hp, b: (b, tile, 0, hp))

    rpb_rows = jnp.pad(rpb.reshape(-1, rpb.shape[-1]).astype(F32),
                       ((0, 0), (0, LANES - rpb.shape[-1])))
    mem_len = kv_m.shape[1]
    assert MEM_HEADS == n_hp and MEM_HEAD_DIM == LANES
    out_spec = pl.BlockSpec((None, seq, LANES), lambda hp, b: (b, 0, hp))
    return pl.pallas_call(
        functools.partial(_na_kernel, rows=rows),
        out_shape=(jax.ShapeDtypeStruct((batch, seq, WIDTH), BF16),
                   jax.ShapeDtypeStruct((batch, seq, WIDTH), BF16)),
        grid=(n_hp, batch),
        in_specs=[in_spec(T_NA), in_spec(T_NA + 1), in_spec(T_NA + 2),
                  pl.BlockSpec(rpb_rows.shape, lambda hp, b: (0, 0)),
                  in_spec(T_MEMQ),
                  pl.BlockSpec((None, mem_len, LANES), lambda hp, b: (b, 0, hp)),
                  pl.BlockSpec((None, mem_len, LANES), lambda hp, b: (b, 0, MEM_HEADS + hp))],
        out_specs=(out_spec, out_spec),
        scratch_shapes=[pltpu.VMEM((LANES // HEAD_DIM, 2 * NA_ROWS - kr, GRID_W, nk), F32),
                        pltpu.VMEM((2, NA_CHUNK, 2 * GRID_W, nk), F32),
                        pltpu.VMEM((NA_CHUNK, 2 * GRID_W, nk), BF16),
                        pltpu.VMEM((seq, mem_len), F32), pltpu.VMEM((seq, mem_len), BF16),
                        pltpu.VMEM((seq, LANES), F32)],
        compiler_params=pltpu.CompilerParams(
            dimension_semantics=("arbitrary", "arbitrary"), vmem_limit_bytes=VMEM_LIMIT),
        name="na_mem_attn",
    )(proj, proj, proj, rpb_rows, proj, kv_m, kv_m)


def _merge_kernel(oa, ga, ob, oc, gb, gc, za0, za1, zb0, zb1, zc0, zc1,
                  wa_f32, wb_f32, wc_f32, wo_f32, pn_ref, x_ref, out_ref,
                  u_scr, wa_ref, wb_ref, wc_ref, wo_ref, *, tm):
    @pl.when(pl.program_id(0) == 0)
    def _():
        for src, dst in ((wa_f32, wa_ref), (wb_f32, wb_ref), (wc_f32, wc_ref), (wo_f32, wo_ref)):
            dst[...] = src[...].astype(BF16)

    def silu_gated(o, silu_g):
        return o.astype(F32) * silu_g.astype(F32)

    seg = MERGE_ROWS // 16
    for blk in range(tm // MERGE_ROWS):
        rows = slice(blk * MERGE_ROWS, (blk + 1) * MERGE_ROWS)

        def project(u, w_ref, gate_lo, gate_hi):
            yb = jnp.dot(u.astype(BF16), w_ref[...], preferred_element_type=F32)
            gate = jnp.concatenate([gate_lo[rows, :], gate_hi[rows, :]], axis=-1)
            return gate.astype(F32) * yb

        p16_rows = slice(blk * seg, (blk + 1) * seg)
        ua = silu_gated(oa[:, p16_rows, :].reshape(MERGE_ROWS, WIDTH),
                        ga[:, p16_rows, :].reshape(MERGE_ROWS, WIDTH))
        for r in range(16):
            for c in range(WIDTH // LANES):
                u_scr[blk, c, pl.ds(r, seg, stride=16), :] = ua[r * seg:(r + 1) * seg,
                                                                c * LANES:(c + 1) * LANES]
        ua = jnp.concatenate([u_scr[blk, c] for c in range(WIDTH // LANES)], axis=-1)

        y = (project(ua, wa_ref, za0, za1)
             + project(silu_gated(ob[rows, :], gb[rows, :]), wb_ref, zb0, zb1)
             + project(silu_gated(oc[rows, :], gc[rows, :]), wc_ref, zc0, zc1))
        z = jnp.dot(y.astype(BF16), wo_ref[...], preferred_element_type=F32)
        ms = jnp.mean(z * z, axis=-1, keepdims=True)
        out_ref[rows, :] = x_ref[rows, :] + z * lax.rsqrt(ms + EPS) * pn_ref[...]


def _merge(x2d, proj, out_a16, out_b, out_c, wa, wb, wc, wo, post_norm, *, tm):
    rows, dm = x2d.shape
    batch, _, seq, _ = proj.shape
    per_b = seq // tm
    seg = tm // 16
    proj16 = proj.reshape(batch, N_STEPS, 16, seq // 16, WIDTH)

    def p16_spec(tile):
        return pl.BlockSpec((None, None, 16, seg, WIDTH),
                            lambda i: (i // per_b, tile, 0, i % per_b, 0))

    def nat_spec(tile):
        return pl.BlockSpec((None, None, tm, WIDTH), lambda i: (i // per_b, tile, i % per_b, 0))

    def rows_spec(width):
        return pl.BlockSpec((tm, width), lambda i: (i, 0))

    def full(shape):
        return pl.BlockSpec(shape, lambda i: (0, 0), pipeline_mode=pl.Buffered(1))

    return pl.pallas_call(
        functools.partial(_merge_kernel, tm=tm),
        out_shape=jax.ShapeDtypeStruct((rows, dm), F32),
        grid=(rows // tm,),
        in_specs=[pl.BlockSpec((None, 16, seg, WIDTH), lambda i: (i // per_b, 0, i % per_b, 0)),
                  p16_spec(T_GATE_A), rows_spec(WIDTH), rows_spec(WIDTH),
                  nat_spec(T_GATE_B), nat_spec(T_GATE_B + 1),
                  nat_spec(T_MERGE_A), nat_spec(T_MERGE_A + 1),
                  nat_spec(T_MERGE_B), nat_spec(T_MERGE_B + 1),
                  nat_spec(T_MERGE_B + 2), nat_spec(T_MERGE_B + 3),
                  full(wa.shape), full(wb.shape), full(wc.shape),
                  full(wo.shape), full(post_norm.shape), rows_spec(dm)],
        out_specs=pl.BlockSpec((tm, dm), lambda i: (i, 0)),
        scratch_shapes=[pltpu.VMEM((tm // MERGE_ROWS, WIDTH // LANES, MERGE_ROWS, LANES), F32)]
        + [pltpu.VMEM(w.shape, BF16) for w in (wa, wb, wc, wo)],
        compiler_params=pltpu.CompilerParams(
            dimension_semantics=("arbitrary",), vmem_limit_bytes=VMEM_LIMIT),
        name="merge",
    )(out_a16, proj16, out_b, out_c, proj, proj, proj, proj, proj, proj, proj, proj,
      wa, wb, wc, wo, post_norm, x2d)


def kernel(x, mem, pre_norm, w_in, merge_bias, na_rpb, mem_norm, w_mem_kv,
           w_branch_a, w_branch_b, w_branch_c, w_out, post_norm):
    b, s, dm = x.shape
    depth = pre_norm.shape[0]
    reach = (DIL_CONFIGS[0][0] // 2) // DIL_CONFIGS[0][1]
    assert all((w // 2) // d == reach for w, d in DIL_CONFIGS) and 2 * reach == QBLK
    assert tuple(d for _, d in DIL_CONFIGS) == SLOT_DIL
    cos_t, sin_t = _rope_tables(s)
    for layer in range(depth):
        x2d = x.reshape(b * s, dm)
        proj = _proj(x2d, pre_norm[layer][None], w_in[layer], merge_bias[layer], cos_t, sin_t,
                     batch=b, seq=s)

        out_a16 = _dil_attn(proj, reach=reach)
        mem2d = mem.reshape(b * mem.shape[1], dm)
        kv_m = _kv_proj(mem2d, mem_norm[layer][None], w_mem_kv[layer], tm=1024)
        out_b, out_c = _na_mem_attn(proj, na_rpb[layer], kv_m.reshape(b, mem.shape[1], -1))

        y = _merge(x2d, proj, out_a16, out_b.reshape(b * s, WIDTH), out_c.reshape(b * s, WIDTH),
                   w_branch_a[layer], w_branch_b[layer], w_branch_c[layer],
                   w_out[layer], post_norm[layer][None], tm=MERGE_ROWS)
        x = y.reshape(b, s, dm)
    return x
```

```python
import functools
import math

import jax
import jax.numpy as jnp
import numpy as np
from jax import lax
from jax.experimental import pallas as pl
from jax.experimental.pallas import tpu as pltpu

HEAD_DIM = 64
DIL_CONFIGS = ((128, 1), (512, 4), (2048, 16))
WIDTH = 512
NA_ROWS = 8
NA_COLS = 16
GRID_W = 64
MEM_HEADS = 4
MEM_HEAD_DIM = 128
ROPE_THETA = 500000.0
ROPE_DIM = HEAD_DIM // 4
ROPE_HALF = ROPE_DIM // 2
EPS = 1e-6
NEG = -1e30
LOG2E = math.log2(math.e)

LANES = 128
QBLK = 128
DIL_CHUNK = 1
DIL_AHEAD = 1
N_SBUF = DIL_AHEAD + 1
NA_CHUNK = 8
PROJ_ROWS = 1024
MERGE_ROWS = 512
VMEM_LIMIT = 56 * 1024 * 1024

_QS = HEAD_DIM ** -0.5 * LOG2E
_MS = MEM_HEAD_DIM ** -0.5 * LOG2E
EPI_ROPE, EPI_SCALE, EPI_SIGMOID, EPI_SILU = range(4)
STEPS = (
    (0, 0, EPI_ROPE, _QS), (1, 0, EPI_ROPE, 1.0), (2, 0, EPI_SCALE, 1.0),
    (9, 0, EPI_SCALE, _QS), (10, 0, EPI_SCALE, 1.0), (11, 0, EPI_SCALE, 1.0),
    (12, 0, EPI_SCALE, _MS),
    (14, 0, EPI_SILU, 1.0), (15, 0, EPI_SILU, 1.0),
    (18, 0, EPI_SIGMOID, 1.0), (19, 0, EPI_SIGMOID, 1.0),
    (20, 0, EPI_SIGMOID, 1.0), (21, 0, EPI_SIGMOID, 1.0),
    (16, 0, EPI_SIGMOID, 1.0), (17, 0, EPI_SIGMOID, 1.0),
    (3, 1, EPI_ROPE, _QS), (4, 1, EPI_ROPE, 1.0), (5, 1, EPI_SCALE, 1.0),
    (6, 2, EPI_ROPE, _QS), (7, 2, EPI_ROPE, 1.0), (8, 2, EPI_SCALE, 1.0),
    (13, 2, EPI_SILU, 1.0),
)
FIRST_MERGE_TILE = 16
N_STEPS = len(STEPS)
T_G0, T_NA, T_MEMQ, T_GATE_B, T_MERGE_B, T_MERGE_A = 0, 3, 6, 7, 9, 13
T_G1, T_G2, T_GATE_A = 15, 18, 21
SLOT_DIL = (1, 4, 16)

BF16 = jnp.bfloat16
F32 = jnp.float32


def _sigmoid(z):
    return 0.5 * jnp.tanh(0.5 * z) + 0.5


def _proj_kernel(wtile_ref, kind_ref, scale_ref, x_ref, g_ref, w_ref, mb_ref,
                 cos_ref, sin_ref, o_ref, h_ref, wb_ref, *, tm):
    j = pl.program_id(1)
    n_lane_tiles = x_ref.shape[1] // LANES

    @pl.when(j == 0)
    def _():
        rb = 256
        ssq = [jnp.sum(jnp.square(x_ref[r0:r0 + rb, :]), axis=-1, keepdims=True)
               for r0 in range(0, tm, rb)]
        rs = lax.rsqrt(jnp.concatenate(ssq, axis=0) * (1.0 / x_ref.shape[1]) + EPS)
        for c in range(n_lane_tiles):
            cs = slice(c * LANES, (c + 1) * LANES)
            h_ref[:, cs] = (x_ref[:, cs] * rs * g_ref[:, cs]).astype(BF16)

    q_scale = scale_ref[j]

    def step(epi, d):
        wb_ref[...] = w_ref[...].astype(BF16)
        lane = lax.broadcasted_iota(jnp.int32, (1, LANES), 1)
        first_half = lane % HEAD_DIM < ROPE_HALF
        if epi == EPI_SIGMOID:
            bias = mb_ref[pl.ds(jnp.maximum(wtile_ref[j] - FIRST_MERGE_TILE, 0), 1), :]
        block = PROJ_ROWS // 4 if epi == EPI_ROPE else PROJ_ROWS
        per_class = block // d
        for r0 in range(0, tm, block):
            rows = pl.ds(r0, block)
            acc = jnp.dot(h_ref[rows, :], wb_ref[...], preferred_element_type=F32)
            if epi == EPI_ROPE:
                c, s = cos_ref[rows, :], sin_ref[rows, :]
            for t in range(WIDTH // LANES):
                lanes = slice(t * LANES, (t + 1) * LANES)
                a = acc[:, lanes]
                if epi == EPI_ROPE:
                    partner = jnp.where(first_half, pltpu.roll(a, LANES - ROPE_HALF, 1),
                                        pltpu.roll(a, ROPE_HALF, 1))
                    a = (a * c + partner * s) * q_scale
                elif epi == EPI_SCALE:
                    a = a * q_scale
                elif epi == EPI_SIGMOID:
                    a = _sigmoid(a + bias[:, lanes])
                else:
                    a = a * _sigmoid(a)
                if d == 1:
                    o_ref[rows, lanes] = a.astype(BF16)
                else:
                    by_class = jnp.swapaxes(a.reshape(per_class, d, LANES), 0, 1)
                    for r in range(d):
                        o_ref[pl.ds(r * (tm // d) + r0 // d, per_class), lanes] = (
                            by_class[r].astype(BF16))

    for kind in sorted({st[2] * len(SLOT_DIL) + st[1] for st in STEPS}):
        epi, slot = divmod(kind, len(SLOT_DIL))
        pl.when(kind_ref[j] == kind)(functools.partial(step, epi, SLOT_DIL[slot]))


def _proj(x2d, gain, w, merge_bias, cos_t, sin_t, *, batch, seq):
    rows, dm = x2d.shape
    tm = seq
    prefetch = (jnp.asarray([st[0] for st in STEPS], jnp.int32),
                jnp.asarray([st[2] * len(SLOT_DIL) + st[1] for st in STEPS], jnp.int32),
                jnp.asarray([st[3] for st in STEPS], F32))
    mb_rows = merge_bias.reshape(-1, WIDTH)
    tab_spec = pl.BlockSpec((tm, LANES), lambda i, j, *_: (0, 0))
    return pl.pallas_call(
        functools.partial(_proj_kernel, tm=tm),
        out_shape=jax.ShapeDtypeStruct((batch, N_STEPS, seq, WIDTH), BF16),
        grid_spec=pltpu.PrefetchScalarGridSpec(
            num_scalar_prefetch=len(prefetch),
            grid=(rows // tm, N_STEPS),
            in_specs=[
                pl.BlockSpec((tm, dm), lambda i, j, *_: (i, 0)),
                pl.BlockSpec((1, dm), lambda i, j, *_: (0, 0)),
                pl.BlockSpec((dm, WIDTH), lambda i, j, wt, kd, sc: (0, wt[j])),
                pl.BlockSpec(mb_rows.shape, lambda i, j, *_: (0, 0)),
                tab_spec, tab_spec,
            ],
            out_specs=pl.BlockSpec((None, None, tm, WIDTH), lambda i, j, *_: (i, j, 0, 0)),
            scratch_shapes=[pltpu.VMEM((tm, dm), BF16), pltpu.VMEM((dm, WIDTH), BF16)]),
        compiler_params=pltpu.CompilerParams(
            dimension_semantics=("arbitrary", "arbitrary"),
            vmem_limit_bytes=VMEM_LIMIT),
        name="proj",
    )(*prefetch, x2d, gain, w, mb_rows, cos_t, sin_t)


def _rope_tables(seq):
    pos = np.arange(seq, dtype=np.float64)
    inv = ROPE_THETA ** (-np.arange(ROPE_HALF, dtype=np.float64) * 2.0 / ROPE_DIM)
    ang = pos[:, None] * inv[None, :]
    cos, sin = np.cos(ang), np.sin(ang)
    rest = HEAD_DIM - ROPE_DIM
    c = np.concatenate([cos, cos, np.ones((seq, rest))] * (LANES // HEAD_DIM), axis=-1)
    s = np.concatenate([-sin, sin, np.zeros((seq, rest))] * (LANES // HEAD_DIM), axis=-1)

    return jnp.asarray(c, F32), jnp.asarray(s, F32)


def _kv_kernel(x_ref, g_ref, w_ref, o_ref):
    xf = x_ref[...]
    ms = jnp.mean(xf * xf, axis=-1, keepdims=True)
    h = (xf * lax.rsqrt(ms + EPS) * g_ref[...]).astype(BF16)
    o_ref[...] = jnp.dot(h, w_ref[...].astype(BF16), preferred_element_type=F32).astype(BF16)


def _kv_proj(mem2d, gain, w, *, tm):
    rows, dm = mem2d.shape
    n_out = w.shape[1]
    return pl.pallas_call(
        _kv_kernel,
        out_shape=jax.ShapeDtypeStruct((rows, n_out), BF16),
        grid=(rows // tm,),
        in_specs=[pl.BlockSpec((tm, dm), lambda i: (i, 0)),
                  pl.BlockSpec((1, dm), lambda i: (0, 0)),
                  pl.BlockSpec((dm, n_out), lambda i: (0, 0))],
        out_specs=pl.BlockSpec((tm, n_out), lambda i: (i, 0)),
        compiler_params=pltpu.CompilerParams(
            dimension_semantics=("arbitrary",), vmem_limit_bytes=VMEM_LIMIT),
        name="kv_proj",
    )(mem2d, gain, w)


def _dil_kernel(q0, k0, v0, q1, k1, v1, q2, k2, v2, o_ref,
                og0, lg0, og1, lg1, bias_scr, s_scr, p_scr, *, seq, reach):
    left = lax.broadcasted_iota(jnp.int32, (1, LANES), 1) < HEAD_DIM
    keep_a = jnp.where(left, 1.0, 0.0).astype(BF16)
    keep_b = jnp.where(left, 0.0, 1.0).astype(BF16)
    kw_max = 2 * QBLK

    @pl.when(jnp.logical_and(pl.program_id(0) == 0, pl.program_id(1) == 0))
    def _():
        rel = (lax.broadcasted_iota(jnp.int32, (QBLK, kw_max), 0)
               - lax.broadcasted_iota(jnp.int32, (QBLK, kw_max), 1))
        for n in range(3):
            bias_scr[n] = jnp.where(jnp.abs(rel + n * reach) <= reach, 0.0, NEG)

    def pair_tile(col_a, col_b):
        return jnp.where(left, jnp.broadcast_to(col_a, (QBLK, LANES)),
                         jnp.broadcast_to(col_b, (QBLK, LANES)))

    def keep(o_scr, l_scr, length):
        fan = 16 * length // seq
        per = QBLK // fan

        def finish(r, row0, acc, m, den):
            for scr, val in ((o_scr, acc / den), (l_scr, m + jnp.log2(den))):
                by_class = jnp.swapaxes(val.reshape(per, fan, LANES), 0, 1)
                for c in range(fan):
                    cls = c * (16 // fan) + r
                    scr[pl.ds(cls * (seq // 16) + row0 // fan, per), :] = by_class[c]
        return finish

    def combine(r, row0, acc2, m2, den2):
        rows = pl.ds(r * QBLK, QBLK)
        l0, l1 = lg0[rows, :], lg1[rows, :]
        m = jnp.maximum(jnp.maximum(l0, l1), m2)
        w0, w1, w2 = jnp.exp2(l0 - m), jnp.exp2(l1 - m), jnp.exp2(m2 - m)
        num = w0 * og0[rows, :] + w1 * og1[rows, :] + w2 * acc2
        o_ref[r] = (num / (w0 + w1 + w2 * den2)).astype(BF16)

    tiles = []
    for (q_ref, k_ref, v_ref), d, finish in (
            ((q0, k0, v0), 1, keep(og0, lg0, seq)),
            ((q1, k1, v1), 4, keep(og1, lg1, seq // 4)),
            ((q2, k2, v2), 16, combine)):
        length = seq // d
        kw = min(kw_max, length)
        for bi in range(seq // QBLK):
            r, row0 = divmod(bi * QBLK, length)
            ks = min(max(row0 - reach, 0), length - kw)
            tiles.append((q_ref, k_ref, v_ref, r, row0, ks, kw, finish))
    chunks = [tiles[i:i + DIL_CHUNK] for i in range(0, len(tiles), DIL_CHUNK)]

    def scores(c):
        for t, (q_ref, k_ref, _, r, row0, ks, kw, _) in enumerate(chunks[c]):
            q2d = q_ref[r, row0:row0 + QBLK, :]
            qq = jnp.concatenate([q2d * keep_a, q2d * keep_b], axis=0)
            s_scr[c % N_SBUF, t, :, :kw] = lax.dot_general(
                qq, k_ref[r, ks:ks + kw, :], (((1,), (1,)), ((), ())),
                preferred_element_type=F32)

    def softmax_pv(c):
        for t, (_, _, v_ref, r, row0, ks, kw, finish) in enumerate(chunks[c]):
            bias = bias_scr[(row0 - ks) // reach, :, :kw]
            s = s_scr[c % N_SBUF, t, :, :kw] + jnp.concatenate([bias, bias], axis=0)
            m = jnp.max(s, axis=-1, keepdims=True)
            p = jnp.exp2(s - m)
            den = jnp.sum(p, axis=-1, keepdims=True)
            p_scr[c % 2, t, :, :kw] = p.astype(BF16)
            pv = jnp.dot(p_scr[c % 2, t, :, :kw], v_ref[r, ks:ks + kw, :],
                         preferred_element_type=F32)
            finish(r, row0, jnp.where(left, pv[:QBLK], pv[QBLK:]),
                   pair_tile(m[:QBLK], m[QBLK:]), pair_tile(den[:QBLK], den[QBLK:]))

    for c in range(min(DIL_AHEAD, len(chunks))):
        scores(c)
    for c in range(len(chunks)):
        if c + DIL_AHEAD < len(chunks):
            scores(c + DIL_AHEAD)
        softmax_pv(c)


def _dil_attn(proj, *, reach):
    batch, _, seq, _ = proj.shape
    n_hp = WIDTH // LANES

    def spec(first_tile, d, kind):
        return pl.BlockSpec((None, None, d, seq // d, LANES),
                            lambda b, hp: (b, first_tile + kind, 0, 0, hp))

    views = {d: proj.reshape(batch, N_STEPS, d, seq // d, WIDTH) for d in SLOT_DIL}
    f32_rows = pltpu.VMEM((seq, LANES), F32)
    return pl.pallas_call(
        functools.partial(_dil_kernel, seq=seq, reach=reach),
        out_shape=jax.ShapeDtypeStruct((batch, 16, seq // 16, WIDTH), BF16),
        grid=(batch, n_hp),
        in_specs=[spec(t0, d, kind) for t0, d in zip((T_G0, T_G1, T_G2), SLOT_DIL)
                  for kind in range(3)],
        out_specs=pl.BlockSpec((None, 16, seq // 16, LANES), lambda b, hp: (b, 0, 0, hp)),
        scratch_shapes=[f32_rows] * 4 + [
            pltpu.VMEM((3, QBLK, 2 * QBLK), F32),
            pltpu.VMEM((N_SBUF, DIL_CHUNK, 2 * QBLK, 2 * QBLK), F32),
            pltpu.VMEM((2, DIL_CHUNK, 2 * QBLK, 2 * QBLK), BF16)],
        compiler_params=pltpu.CompilerParams(
            dimension_semantics=("arbitrary", "arbitrary"),
            vmem_limit_bytes=VMEM_LIMIT),
        name="dil_attn",
    )(*[views[d] for d in SLOT_DIL for _ in range(3)])


def _na_kernel(q_ref, k_ref, v_ref, rpb_ref, qm_ref, km_ref, vm_ref,
               o_ref, om_ref, bias_scr, s_scr, p_scr, sm_scr, pm_scr, rdm_scr, *, rows):
    kr = min(NA_ROWS, rows)
    nk = kr * GRID_W
    n_var = 2 * NA_ROWS - kr
    left = lax.broadcasted_iota(jnp.int32, (1, LANES), 1) < HEAD_DIM
    keep_a = jnp.where(left, 1.0, 0.0).astype(BF16)
    keep_b = jnp.where(left, 0.0, 1.0).astype(BF16)

    @pl.when(pl.program_id(1) == 0)
    def _():
        qc = lax.broadcasted_iota(jnp.int32, (GRID_W, LANES), 0)
        kc = lax.broadcasted_iota(jnp.int32, (GRID_W, LANES), 1) % GRID_W
        rel = jnp.clip(kc - qc, -(NA_COLS - 1), NA_COLS - 1) + NA_COLS - 1
        c_start = jnp.clip(qc - NA_COLS // 2, 0, GRID_W - NA_COLS)
        valid = jnp.logical_and(kc >= c_start, kc < c_start + NA_COLS)
        for hh in range(LANES // HEAD_DIM):
            head = pl.program_id(0) * (LANES // HEAD_DIM) + hh
            for a in range(2 * NA_ROWS - 1):
                row = rpb_ref[pl.ds(head * (2 * NA_ROWS - 1) + a, 1), :]
                blk = jnp.take_along_axis(jnp.broadcast_to(row, (GRID_W, LANES)), rel, axis=1)
                blk = jnp.where(valid, blk * LOG2E, NEG)
                for j in range(kr):
                    if 0 <= a - j < n_var:
                        half = slice((j % 2) * GRID_W, (j % 2 + 1) * GRID_W)
                        bias_scr[hh, a - j, :, j * GRID_W:(j + 1) * GRID_W] = blk[:, half]

    def window(r):
        r_start = min(max(r - kr // 2, 0), rows - kr)
        return r * GRID_W, r_start * GRID_W, r_start - r + (NA_ROWS - 1)

    def scores(c):
        for t in range(NA_CHUNK):
            row0, ks, _ = window(c * NA_CHUNK + t)
            q2d = q_ref[row0:row0 + GRID_W, :]
            qq = jnp.concatenate([q2d * keep_a, q2d * keep_b], axis=0)
            s_scr[c % 2, t] = lax.dot_general(qq, k_ref[ks:ks + nk, :], (((1,), (1,)), ((), ())),
                                              preferred_element_type=F32)

    def softmax_pv(c):
        for t in range(NA_CHUNK):
            row0, ks, variant = window(c * NA_CHUNK + t)
            dens = []
            for hh in range(LANES // HEAD_DIM):
                rows_h = slice(hh * GRID_W, (hh + 1) * GRID_W)
                s = s_scr[c % 2, t, rows_h, :] + bias_scr[hh, variant]
                m = jnp.max(s, axis=-1, keepdims=True)
                p = jnp.exp2(s - m)
                dens.append(jnp.sum(p, axis=-1, keepdims=True))
                p_scr[t, rows_h, :] = p.astype(BF16)
            pv = jnp.dot(p_scr[t], v_ref[ks:ks + nk, :], preferred_element_type=F32)
            den_t = jnp.where(left, jnp.broadcast_to(dens[0], (GRID_W, LANES)),
                              jnp.broadcast_to(dens[1], (GRID_W, LANES)))
            o_ref[row0:row0 + GRID_W, :] = (
                jnp.where(left, pv[:GRID_W], pv[GRID_W:]) / den_t).astype(BF16)

    sm_scr[...] = lax.dot_general(qm_ref[...], km_ref[...], (((1,), (1,)), ((), ())),
                                  preferred_element_type=F32)

    def mem_softmax(i):
        rows_i = slice(i * QBLK, (i + 1) * QBLK)
        s = sm_scr[rows_i, :]
        m = jnp.max(s, axis=-1, keepdims=True)
        p = jnp.exp2(s - m)
        den = jnp.sum(p, axis=-1, keepdims=True)
        pm_scr[rows_i, :] = p.astype(BF16)
        rdm_scr[rows_i, :] = jnp.broadcast_to(1.0 / den, (QBLK, LANES))

    n_chunks = rows // NA_CHUNK
    mem_tiles = qm_ref.shape[0] // QBLK
    scores(0)
    for c in range(n_chunks):
        if c + 1 < n_chunks:
            scores(c + 1)
        softmax_pv(c)
        for i in range(c * mem_tiles // n_chunks, (c + 1) * mem_tiles // n_chunks):
            mem_softmax(i)
    pv = jnp.dot(pm_scr[...], vm_ref[...], preferred_element_type=F32)
    om_ref[...] = (pv * rdm_scr[...]).astype(BF16)


def _na_mem_attn(proj, rpb, kv_m):
    batch, _, seq, _ = proj.shape
    rows = seq // GRID_W
    kr = min(NA_ROWS, rows)
    nk = kr * GRID_W
    n_hp = WIDTH // LANES
    assert rpb.shape == (WIDTH // HEAD_DIM, 2 * NA_ROWS - 1, 2 * NA_COLS - 1)

    def in_spec(tile):
        return pl.BlockSpec((None, None, seq, LANES), lambda hp, b: (b, tile, 0, hp))

    rpb_rows = jnp.pad(rpb.reshape(-1, rpb.shape[-1]).astype(F32),
                       ((0, 0), (0, LANES - rpb.shape[-1])))
    mem_len = kv_m.shape[1]
    assert MEM_HEADS == n_hp and MEM_HEAD_DIM == LANES
    out_spec = pl.BlockSpec((None, seq, LANES), lambda hp, b: (b, 0, hp))
    return pl.pallas_call(
        functools.partial(_na_kernel, rows=rows),
        out_shape=(jax.ShapeDtypeStruct((batch, seq, WIDTH), BF16),
                   jax.ShapeDtypeStruct((batch, seq, WIDTH), BF16)),
        grid=(n_hp, batch),
        in_specs=[in_spec(T_NA), in_spec(T_NA + 1), in_spec(T_NA + 2),
                  pl.BlockSpec(rpb_rows.shape, lambda hp, b: (0, 0)),
                  in_spec(T_MEMQ),
                  pl.BlockSpec((None, mem_len, LANES), lambda hp, b: (b, 0, hp)),
                  pl.BlockSpec((None, mem_len, LANES), lambda hp, b: (b, 0, MEM_HEADS + hp))],
        out_specs=(out_spec, out_spec),
        scratch_shapes=[pltpu.VMEM((LANES // HEAD_DIM, 2 * NA_ROWS - kr, GRID_W, nk), F32),
                        pltpu.VMEM((2, NA_CHUNK, 2 * GRID_W, nk), F32),
                        pltpu.VMEM((NA_CHUNK, 2 * GRID_W, nk), BF16),
                        pltpu.VMEM((seq, mem_len), F32), pltpu.VMEM((seq, mem_len), BF16),
                        pltpu.VMEM((seq, LANES), F32)],
        compiler_params=pltpu.CompilerParams(
            dimension_semantics=("arbitrary", "arbitrary"), vmem_limit_bytes=VMEM_LIMIT),
        name="na_mem_attn",
    )(proj, proj, proj, rpb_rows, proj, kv_m, kv_m)


def _merge_kernel(oa, ga, ob, oc, gb, gc, za0, za1, zb0, zb1, zc0, zc1,
                  wa_f32, wb_f32, wc_f32, wo_f32, pn_ref, x_ref, out_ref,
                  u_scr, wa_ref, wb_ref, wc_ref, wo_ref, *, tm):
    @pl.when(pl.program_id(0) == 0)
    def _():
        for src, dst in ((wa_f32, wa_ref), (wb_f32, wb_ref), (wc_f32, wc_ref), (wo_f32, wo_ref)):
            dst[...] = src[...].astype(BF16)

    def silu_gated(o, silu_g):
        return o.astype(F32) * silu_g.astype(F32)

    seg = MERGE_ROWS // 16
    for blk in range(tm // MERGE_ROWS):
        rows = slice(blk * MERGE_ROWS, (blk + 1) * MERGE_ROWS)

        def project(u, w_ref, gate_lo, gate_hi):
            yb = jnp.dot(u.astype(BF16), w_ref[...], preferred_element_type=F32)
            gate = jnp.concatenate([gate_lo[rows, :], gate_hi[rows, :]], axis=-1)
            return gate.astype(F32) * yb

        p16_rows = slice(blk * seg, (blk + 1) * seg)
        ua = silu_gated(oa[:, p16_rows, :].reshape(MERGE_ROWS, WIDTH),
                        ga[:, p16_rows, :].reshape(MERGE_ROWS, WIDTH))
        for r in range(16):
            for c in range(WIDTH // LANES):
                u_scr[blk, c, pl.ds(r, seg, stride=16), :] = ua[r * seg:(r + 1) * seg,
                                                                c * LANES:(c + 1) * LANES]
        ua = jnp.concatenate([u_scr[blk, c] for c in range(WIDTH // LANES)], axis=-1)

        y = (project(ua, wa_ref, za0, za1)
             + project(silu_gated(ob[rows, :], gb[rows, :]), wb_ref, zb0, zb1)
             + project(silu_gated(oc[rows, :], gc[rows, :]), wc_ref, zc0, zc1))
        z = jnp.dot(y.astype(BF16), wo_ref[...], preferred_element_type=F32)
        ms = jnp.mean(z * z, axis=-1, keepdims=True)
        out_ref[rows, :] = x_ref[rows, :] + z * lax.rsqrt(ms + EPS) * pn_ref[...]


def _merge(x2d, proj, out_a16, out_b, out_c, wa, wb, wc, wo, post_norm, *, tm):
    rows, dm = x2d.shape
    batch, _, seq, _ = proj.shape
    per_b = seq // tm
    seg = tm // 16
    proj16 = proj.reshape(batch, N_STEPS, 16, seq // 16, WIDTH)

    def p16_spec(tile):
        return pl.BlockSpec((None, None, 16, seg, WIDTH),
                            lambda i: (i // per_b, tile, 0, i % per_b, 0))

    def nat_spec(tile):
        return pl.BlockSpec((None, None, tm, WIDTH), lambda i: (i // per_b, tile, i % per_b, 0))

    def rows_spec(width):
        return pl.BlockSpec((tm, width), lambda i: (i, 0))

    def full(shape):
        return pl.BlockSpec(shape, lambda i: (0, 0), pipeline_mode=pl.Buffered(1))

    return pl.pallas_call(
        functools.partial(_merge_kernel, tm=tm),
        out_shape=jax.ShapeDtypeStruct((rows, dm), F32),
        grid=(rows // tm,),
        in_specs=[pl.BlockSpec((None, 16, seg, WIDTH), lambda i: (i // per_b, 0, i % per_b, 0)),
                  p16_spec(T_GATE_A), rows_spec(WIDTH), rows_spec(WIDTH),
                  nat_spec(T_GATE_B), nat_spec(T_GATE_B + 1),
                  nat_spec(T_MERGE_A), nat_spec(T_MERGE_A + 1),
                  nat_spec(T_MERGE_B), nat_spec(T_MERGE_B + 1),
                  nat_spec(T_MERGE_B + 2), nat_spec(T_MERGE_B + 3),
                  full(wa.shape), full(wb.shape), full(wc.shape),
                  full(wo.shape), full(post_norm.shape), rows_spec(dm)],
        out_specs=pl.BlockSpec((tm, dm), lambda i: (i, 0)),
        scratch_shapes=[pltpu.VMEM((tm // MERGE_ROWS, WIDTH // LANES, MERGE_ROWS, LANES), F32)]
        + [pltpu.VMEM(w.shape, BF16) for w in (wa, wb, wc, wo)],
        compiler_params=pltpu.CompilerParams(
            dimension_semantics=("arbitrary",), vmem_limit_bytes=VMEM_LIMIT),
        name="merge",
    )(out_a16, proj16, out_b, out_c, proj, proj, proj, proj, proj, proj, proj, proj,
      wa, wb, wc, wo, post_norm, x2d)


def kernel(x, mem, pre_norm, w_in, merge_bias, na_rpb, mem_norm, w_mem_kv,
           w_branch_a, w_branch_b, w_branch_c, w_out, post_norm):
    b, s, dm = x.shape
    depth = pre_norm.shape[0]
    reach = (DIL_CONFIGS[0][0] // 2) // DIL_CONFIGS[0][1]
    assert all((w // 2) // d == reach for w, d in DIL_CONFIGS) and 2 * reach == QBLK
    assert tuple(d for _, d in DIL_CONFIGS) == SLOT_DIL
    cos_t, sin_t = _rope_tables(s)
    for layer in range(depth):
        x2d = x.reshape(b * s, dm)
        proj = _proj(x2d, pre_norm[layer][None], w_in[layer], merge_bias[layer], cos_t, sin_t,
                     batch=b, seq=s)

        out_a16 = _dil_attn(proj, reach=reach)
        mem2d = mem.reshape(b * mem.shape[1], dm)
        kv_m = _kv_proj(mem2d, mem_norm[layer][None], w_mem_kv[layer], tm=1024)
        out_b, out_c = _na_mem_attn(proj, na_rpb[layer], kv_m.reshape(b, mem.shape[1], -1))

        y = _merge(x2d, proj, out_a16, out_b.reshape(b * s, WIDTH), out_c.reshape(b * s, WIDTH),
                   w_branch_a[layer], w_branch_b[layer], w_branch_c[layer],
                   w_out[layer], post_norm[layer][None], tm=MERGE_ROWS)
        x = y.reshape(b, s, dm)
    return x
```
